```python
import jax, jax.numpy as jnp
from jax import lax
import numpy as np

D_MODEL = 1024
BATCH = 1
SEQ = 16384
DEPTH = 1

N_MEM = 256
ATT_HEADS = 8
ATT_HD = 64
ATT_W = ATT_HEADS * ATT_HD
IDX_HEADS = 4
IDX_HD = 64
TOPK_MAX = 256
Q_BLOCK = 128
RWKV_HEADS = 8
RWKV_HD = 64
RWKV_W = RWKV_HEADS * RWKV_HD
DECAY_LORA = 64
AAA_LORA = 64
GATE_LORA = 128
GN_EPS = 64e-5
MEM_HEADS = 4
MEM_HD = 128
MEM_W = MEM_HEADS * MEM_HD
N_BRANCH = 3
BRANCH_W = 512
ROPE_THETA = 500000.0
ROPE_FRACTION = 4
NORM_EPS = 1e-6
PEER_KEYS = 128
N_EXPERTS = PEER_KEYS * PEER_KEYS
PEER_HEADS = 8
PEER_QDIM = 256
PEER_TOPK = 16
PEER_BLOCK = 128

ATT_IN = 3 * ATT_W + IDX_HEADS * IDX_HD + IDX_HD + IDX_HEADS
RWKV_IN = 3 * RWKV_W + DECAY_LORA + AAA_LORA + GATE_LORA
GATE_IN = N_BRANCH * D_MODEL
W_IN_COLS = ATT_IN + RWKV_IN + MEM_W + GATE_IN

kernel_name = 'dsa_rwkv7_memory_gated_peer_layer'

F32 = jnp.float32


def split_last(t, sizes):
    offs = np.cumsum(sizes)[:-1].tolist()
    return jnp.split(t, offs, axis=-1)


def rmsnorm(t, gain):
    tf = t.astype(F32)
    out = tf * lax.rsqrt(jnp.mean(tf * tf, axis=-1, keepdims=True) + NORM_EPS) * gain.astype(F32)
    return out.astype(t.dtype)


def rope_tables(positions, head_dim):
    rot = head_dim // ROPE_FRACTION
    inv_freq = 1.0 / (ROPE_THETA ** (jnp.arange(0, rot, 2, dtype=F32) / rot))
    ang = positions.astype(F32)[..., None] * inv_freq
    return jnp.cos(ang)[:, :, None, :], jnp.sin(ang)[:, :, None, :]


def rope_partial(t, cos, sin):
    half = cos.shape[-1]
    t1 = t[..., :half]
    t2 = t[..., half:2 * half]
    out = jnp.concatenate([t1 * cos - t2 * sin, t2 * cos + t1 * sin, t[..., 2 * half:]], axis=-1)
    return out.astype(t.dtype)


def dsa_attention(q, k, v, iq, ik, iw):
    B, S, H, Dh = q.shape
    n_sel = min(TOPK_MAX, S // 4)
    nb = S // Q_BLOCK
    idx_scale = (IDX_HEADS ** -0.5) * (IDX_HD ** -0.5)
    key_pos = jnp.arange(S)
    ikf = ik.astype(F32)

    def blocks(t):
        return jnp.moveaxis(t.reshape(B, nb, Q_BLOCK, *t.shape[2:]), 1, 0)

    def one_block(args):
        qb, iqb, iwb, tq = args
        logits = jnp.einsum('bqhd,bsd->bqhs', iqb.astype(F32), ikf)
        score = jnp.einsum('bqh,bqhs->bqs', iwb.astype(F32) * idx_scale, jax.nn.relu(logits))
        causal = key_pos[None, None, :] <= tq[None, :, None]
        score = jnp.where(causal, score, -jnp.inf)
        _, sel = lax.top_k(score, n_sel)
        valid = sel <= tq[None, :, None]
        k_sel = jax.vmap(lambda kk, ii: kk[ii])(k, sel)
        v_sel = jax.vmap(lambda vv, ii: vv[ii])(v, sel)
        att = jnp.einsum('bqhd,bqnhd->bhqn', qb.astype(F32), k_sel.astype(F32)) * (Dh ** -0.5)
        att = jnp.where(valid[:, None], att, -jnp.inf)
        p = jax.nn.softmax(att, axis=-1)
        out = jnp.einsum('bhqn,bqnhd->bqhd', p, v_sel.astype(F32))
        return out.astype(q.dtype)

    tq_blocks = jnp.arange(S).reshape(nb, Q_BLOCK)
    out = lax.map(one_block, (blocks(q), blocks(iq), blocks(iw), tq_blocks))
    return jnp.moveaxis(out, 0, 1).reshape(B, S, H * Dh)


def wkv7_scan(r, w, k, v, a, b):
    B, S, H, N = r.shape
    tm = lambda t: jnp.moveaxis(t, 1, 0)

    def step(state, inp):
        r_t, w_t, k_t, v_t, a_t, b_t = inp
        sa = jnp.einsum('bhij,bhj->bhi', state, a_t)
        state = state * w_t[:, :, None, :] + sa[..., None] * b_t[:, :, None, :] + v_t[..., :, None] * k_t[:, :, None, :]
        y = jnp.einsum('bhij,bhj->bhi', state, r_t)
        return state, y

    state0 = jnp.zeros((B, H, N, N), F32)
    _, ys = lax.scan(step, state0, (tm(r), tm(w), tm(k), tm(v), tm(a), tm(b)))
    return jnp.moveaxis(ys, 0, 1)


def rwkv7_time_mix(p, mu, w0, w2, a0, a2, g2, k_k, k_a, r_k, ln_w, ln_b):
    B, S, _ = p.shape
    prev = jnp.pad(p, ((0, 0), (1, 0), (0, 0)))[:, :-1]
    p = p + (prev - p) * mu
    r, k, v, wd, ad, gd = [t.astype(F32) for t in split_last(p, (RWKV_W, RWKV_W, RWKV_W, DECAY_LORA, AAA_LORA, GATE_LORA))]
    w_log = -jax.nn.softplus(-(w0.astype(F32) + jnp.tanh(wd) @ w2.astype(F32))) - 0.5
    decay = jnp.exp(-jnp.exp(w_log))
    a = jax.nn.sigmoid(a0.astype(F32) + ad @ a2.astype(F32))
    g = jax.nn.sigmoid(gd) @ g2.astype(F32)
    heads = lambda t: t.reshape(B, S, RWKV_HEADS, RWKV_HD)
    kk = heads(k * k_k.astype(F32))
    kk = kk / jnp.maximum(jnp.sqrt(jnp.sum(kk * kk, axis=-1, keepdims=True)), 1e-12)
    k = k * (1.0 + (a - 1.0) * k_a.astype(F32))
    rh, kh, vh, wh, ah = heads(r), heads(k), heads(v), heads(decay), heads(a)
    y = wkv7_scan(rh, wh, kh, vh, -kk, kk * ah)
    mean = jnp.mean(y, axis=-1, keepdims=True)
    var = jnp.mean(jnp.square(y - mean), axis=-1, keepdims=True)
    y = (y - mean) * lax.rsqrt(var + GN_EPS) * ln_w.astype(F32).reshape(RWKV_HEADS, RWKV_HD) + ln_b.astype(F32).reshape(RWKV_HEADS, RWKV_HD)
    y = y + jnp.sum(rh * kh * r_k.astype(F32), axis=-1, keepdims=True) * vh
    return (y.reshape(B, S, RWKV_W) * g).astype(p.dtype)


def memory_cross_attention(q_p, mem, mem_norm_gain, w_mem_kv, q_gain, k_gain):
    B, S, _ = q_p.shape
    M = mem.shape[1]
    m = rmsnorm(mem, mem_norm_gain)
    km, vm = split_last(m @ w_mem_kv, (MEM_W, MEM_W))
    q = rmsnorm(q_p.reshape(B, S, MEM_HEADS, MEM_HD), q_gain)
    km = rmsnorm(km.reshape(B, M, MEM_HEADS, MEM_HD), k_gain)
    vm = vm.reshape(B, M, MEM_HEADS, MEM_HD)
    s = jnp.einsum('bqhd,bmhd->bhqm', q.astype(F32), km.astype(F32)) * (MEM_HD ** -0.5)
    p = jax.nn.softmax(s, axis=-1)
    out = jnp.einsum('bhqm,bmhd->bqhd', p, vm.astype(F32))
    return out.reshape(B, S, MEM_W).astype(q_p.dtype)


def peer_ffn(xn, w_q, key1, key2, u, v):
    B, S, D = xn.shape
    q = (xn @ w_q).reshape(B, S, PEER_HEADS, 2, PEER_QDIM // 2).astype(F32)
    s1 = jnp.einsum('bshd,kd->bshk', q[..., 0, :], key1.astype(F32))
    s2 = jnp.einsum('bshd,kd->bshk', q[..., 1, :], key2.astype(F32))
    v1, i1 = lax.top_k(s1, PEER_TOPK)
    v2, i2 = lax.top_k(s2, PEER_TOPK)
    cand = (v1[..., :, None] + v2[..., None, :]).reshape(B, S, PEER_HEADS, PEER_TOPK * PEER_TOPK)
    sc, ci = lax.top_k(cand, PEER_TOPK)
    e1 = jnp.take_along_axis(i1, ci // PEER_TOPK, axis=-1)
    e2 = jnp.take_along_axis(i2, ci % PEER_TOPK, axis=-1)
    expert = e1 * PEER_KEYS + e2
    gate = jax.nn.softmax(sc, axis=-1)
    T = B * S
    E = PEER_HEADS * PEER_TOPK
    nb = T // PEER_BLOCK
    xb = xn.reshape(nb, PEER_BLOCK, D)
    eb = expert.reshape(nb, PEER_BLOCK, E)
    gb = gate.reshape(nb, PEER_BLOCK, E)

    def one_block(args):
        xt, et, gt = args
        ue = u[et].astype(F32)
        ve = v[et].astype(F32)
        hid = jax.nn.gelu(jnp.einsum('td,ted->te', xt.astype(F32), ue), approximate=False)
        return jnp.einsum('te,ted->td', gt * hid, ve).astype(xt.dtype)

    out = lax.map(one_block, (xb, eb, gb))
    return out.reshape(B, S, D)


def hybrid_layer(h, mem, cos, sin, mix_norm_gain, w_in, gate_bias, att_q_gain, att_k_gain, mem_norm_gain, w_mem_kv, mem_q_gain, mem_k_gain, rwkv_mu, rwkv_w0, rwkv_w2, rwkv_a0, rwkv_a2, rwkv_g2, rwkv_k_k, rwkv_k_a, rwkv_r_k, rwkv_ln_w, rwkv_ln_b, w_up, w_out, ffn_norm_gain, peer_w_q, peer_key1, peer_key2, peer_u, peer_v):
    B, S, D = h.shape
    n = rmsnorm(h, mix_norm_gain)
    proj = n @ w_in
    att_p, rwkv_p, memq_p, gate_p = split_last(proj, (ATT_IN, RWKV_IN, MEM_W, GATE_IN))
    q, k, v, iq, ik, iw = split_last(att_p, (ATT_W, ATT_W, ATT_W, IDX_HEADS * IDX_HD, IDX_HD, IDX_HEADS))
    q = rope_partial(rmsnorm(q.reshape(B, S, ATT_HEADS, ATT_HD), att_q_gain), cos, sin)
    k = rope_partial(rmsnorm(k.reshape(B, S, ATT_HEADS, ATT_HD), att_k_gain), cos, sin)
    v = v.reshape(B, S, ATT_HEADS, ATT_HD)
    iq = rope_partial(iq.reshape(B, S, IDX_HEADS, IDX_HD), cos, sin)
    ik = rope_partial(ik[:, :, None, :], cos, sin)[:, :, 0]
    y_att = dsa_attention(q, k, v, iq, ik, iw)
    y_rwkv = rwkv7_time_mix(rwkv_p, rwkv_mu, rwkv_w0, rwkv_w2, rwkv_a0, rwkv_a2, rwkv_g2, rwkv_k_k, rwkv_k_a, rwkv_r_k, rwkv_ln_w, rwkv_ln_b)
    y_mem = memory_cross_attention(memq_p, mem, mem_norm_gain, w_mem_kv, mem_q_gain, mem_k_gain)
    branches = jnp.stack([y_att.astype(F32), y_rwkv.astype(F32), y_mem.astype(F32)], axis=2)
    up = jnp.einsum('bscw,cwd->bscd', branches, w_up.astype(F32))
    gates = jax.nn.sigmoid(gate_p.reshape(B, S, N_BRANCH, D).astype(F32) + gate_bias.astype(F32))
    merged = jnp.sum(gates * up, axis=2).astype(h.dtype)
    h = h + merged @ w_out
    h = h + peer_ffn(rmsnorm(h, ffn_norm_gain), peer_w_q, peer_key1, peer_key2, peer_u, peer_v)
    return h


def setup_inputs(seed: int = 0) -> dict:
    key = jax.random.key(seed)
    ks = iter(jax.random.split(key, 48))
    L = DEPTH

    def nrm(shape, scale):
        return jax.random.normal(next(ks), shape, F32) * scale

    def gain(shape):
        return 1.0 + nrm(shape, 0.05)

    x = nrm((BATCH, SEQ, D_MODEL), 1.0)
    mem = nrm((BATCH, N_MEM, D_MODEL), 1.0)
    positions = jax.random.randint(next(ks), (BATCH, 1), 0, 4096, dtype=jnp.int32) + jnp.arange(SEQ, dtype=jnp.int32)[None, :]
    return {
        'x': x,
        'mem': mem,
        'positions': positions,
        'mix_norm_gain': gain((L, D_MODEL)),
        'w_in': nrm((L, D_MODEL, W_IN_COLS), D_MODEL ** -0.5),
        'gate_bias': nrm((L, N_BRANCH, D_MODEL), 0.1),
        'att_q_gain': gain((L, ATT_HD)),
        'att_k_gain': gain((L, ATT_HD)),
        'mem_norm_gain': gain((L, D_MODEL)),
        'w_mem_kv': nrm((L, D_MODEL, 2 * MEM_W), D_MODEL ** -0.5),
        'mem_q_gain': gain((L, MEM_HD)),
        'mem_k_gain': gain((L, MEM_HD)),
        'rwkv_mu': jax.random.uniform(next(ks), (L, RWKV_IN), F32, 0.0, 1.0),
        'rwkv_w0': jax.random.uniform(next(ks), (L, RWKV_W), F32, -3.0, 0.0),
        'rwkv_w2': nrm((L, DECAY_LORA, RWKV_W), 0.5 * DECAY_LORA ** -0.5),
        'rwkv_a0': nrm((L, RWKV_W), 0.1),
        'rwkv_a2': nrm((L, AAA_LORA, RWKV_W), 0.5 * AAA_LORA ** -0.5),
        'rwkv_g2': nrm((L, GATE_LORA, RWKV_W), GATE_LORA ** -0.5),
        'rwkv_k_k': 0.85 + nrm((L, RWKV_W), 0.05),
        'rwkv_k_a': gain((L, RWKV_W)),
        'rwkv_r_k': nrm((L, RWKV_HEADS, RWKV_HD), 0.1),
        'rwkv_ln_w': gain((L, RWKV_W)),
        'rwkv_ln_b': nrm((L, RWKV_W), 0.02),
        'w_up': nrm((L, N_BRANCH, BRANCH_W, D_MODEL), BRANCH_W ** -0.5),
        'w_out': nrm((L, D_MODEL, D_MODEL), D_MODEL ** -0.5),
        'ffn_norm_gain': gain((L, D_MODEL)),
        'peer_w_q': nrm((L, D_MODEL, PEER_HEADS * PEER_QDIM), D_MODEL ** -0.5),
        'peer_key1': nrm((L, PEER_KEYS, PEER_QDIM // 2), (PEER_QDIM // 2) ** -0.5),
        'peer_key2': nrm((L, PEER_KEYS, PEER_QDIM // 2), (PEER_QDIM // 2) ** -0.5),
        'peer_u': nrm((L, N_EXPERTS, D_MODEL), D_MODEL ** -0.5),
        'peer_v': nrm((L, N_EXPERTS, D_MODEL), PEER_HEADS ** -0.5),
    }


def reference(x, mem, positions, mix_norm_gain, w_in, gate_bias, att_q_gain, att_k_gain, mem_norm_gain, w_mem_kv, mem_q_gain, mem_k_gain, rwkv_mu, rwkv_w0, rwkv_w2, rwkv_a0, rwkv_a2, rwkv_g2, rwkv_k_k, rwkv_k_a, rwkv_r_k, rwkv_ln_w, rwkv_ln_b, w_up, w_out, ffn_norm_gain, peer_w_q, peer_key1, peer_key2, peer_u, peer_v):
    cos, sin = rope_tables(positions, ATT_HD)
    h = x
    for layer in range(DEPTH):
        h = hybrid_layer(h, mem, cos, sin, mix_norm_gain[layer], w_in[layer], gate_bias[layer], att_q_gain[layer], att_k_gain[layer], mem_norm_gain[layer], w_mem_kv[layer], mem_q_gain[layer], mem_k_gain[layer], rwkv_mu[layer], rwkv_w0[layer], rwkv_w2[layer], rwkv_a0[layer], rwkv_a2[layer], rwkv_g2[layer], rwkv_k_k[layer], rwkv_k_a[layer], rwkv_r_k[layer], rwkv_ln_w[layer], rwkv_ln_b[layer], w_up[layer], w_out[layer], ffn_norm_gain[layer], peer_w_q[layer], peer_key1[layer], peer_key2[layer], peer_u[layer], peer_v[layer])
    return h
```

```python
import functools

import numpy as np
import jax
import jax.numpy as jnp
from jax import lax
from jax.experimental import pallas as pl
from jax.experimental.pallas import tpu as pltpu

F32 = jnp.float32
I32 = jnp.int32
MXU_DTYPE = jnp.bfloat16
HI = lax.Precision.HIGHEST

D_MODEL = 1024
ATT_HEADS, ATT_HD = 8, 64
ATT_W = ATT_HEADS * ATT_HD
IDX_HEADS, IDX_HD = 4, 64
TOPK_MAX = 256
RWKV_HEADS, RWKV_HD = 8, 64
RWKV_W = RWKV_HEADS * RWKV_HD
DECAY_LORA, AAA_LORA, GATE_LORA = 64, 64, 128
GN_EPS = 64e-5
MEM_HEADS, MEM_HD = 4, 128
MEM_W = MEM_HEADS * MEM_HD
N_BRANCH = 3
ROPE_THETA = 500000.0
ROPE_ROT = ATT_HD // 4
NORM_EPS = 1e-6
PEER_KEYS = 128
PEER_HEADS = 8
PEER_QDIM = 256
PEER_TOPK = 16

LANES = 128
INT_MIN = -(2 ** 31)
NEG_BIG = -1e30
VMEM_LIMIT = 56 * 1024 * 1024

COL_GATE = 0
COL_Q = 3072
COL_K = 3584
COL_V = 4096
COL_R = 4608
COL_RK = 5120
COL_RV = 5632
COL_MEMQ = 6144
COL_IQ = 6656
COL_LORA = 6912
COL_IKIW = 7168
PROJ_COLS = 7680


def _cparams(sem):
    return pltpu.CompilerParams(dimension_semantics=sem, vmem_limit_bytes=VMEM_LIMIT)


def _mm(a, b):
    return jnp.dot(a.astype(MXU_DTYPE), b.astype(MXU_DTYPE), preferred_element_type=F32)


def _mm_hi(a, b):
    return jnp.dot(a.astype(F32), b.astype(F32), preferred_element_type=F32, precision=HI)


def _normproj_kernel(x_ref, g_ref, w_ref, o_ref, xn_ref):
    @pl.when(pl.program_id(1) == 0)
    def _():
        x = x_ref[...]
        ms = jnp.mean(x * x, axis=-1, keepdims=True)
        xn_ref[...] = (x * lax.rsqrt(ms + NORM_EPS) * g_ref[...]).astype(xn_ref.dtype)

    o_ref[...] = jnp.dot(xn_ref[...], w_ref[...], preferred_element_type=F32)


def _norm_proj(x, gain, w, tm, tn):
    S, D = x.shape
    N = w.shape[1]
    return pl.pallas_call(
        _normproj_kernel,
        grid=(S // tm, N // tn),
        in_specs=[
            pl.BlockSpec((tm, D), lambda i, j: (i, 0)),
            pl.BlockSpec((1, D), lambda i, j: (0, 0)),
            pl.BlockSpec((D, tn), lambda i, j: (0, j)),
        ],
        out_specs=pl.BlockSpec((tm, tn), lambda i, j: (i, j)),
        out_shape=jax.ShapeDtypeStruct((S, N), F32),
        scratch_shapes=[pltpu.VMEM((tm, D), w.dtype)],
        compiler_params=_cparams(("parallel", "arbitrary")),
    )(x, gain.reshape(1, D), w)


def _pack_w_in(w_in):
    D = w_in.shape[0]
    o = 0
    parts = {}
    for name, width in (("q", ATT_W), ("k", ATT_W), ("v", ATT_W), ("iq", IDX_HEADS * IDX_HD),
                        ("ik", IDX_HD), ("iw", IDX_HEADS), ("r", RWKV_W), ("rk", RWKV_W),
                        ("rv", RWKV_W), ("lora", DECAY_LORA + AAA_LORA + GATE_LORA),
                        ("memq", MEM_W), ("gate", N_BRANCH * D_MODEL)):
        parts[name] = w_in[:, o:o + width]
        o += width
    ikiw = jnp.concatenate([parts["ik"], parts["iw"]], axis=1)
    ikiw = jnp.pad(ikiw, ((0, 0), (0, LANES - ikiw.shape[1])))
    packed = jnp.concatenate(
        [parts["gate"], parts["q"], parts["k"], parts["v"], parts["r"], parts["rk"], parts["rv"],
         parts["memq"], parts["iq"], parts["lora"], ikiw], axis=1)
    packed = jnp.pad(packed, ((0, 0), (0, PROJ_COLS - packed.shape[1])))
    return packed.astype(MXU_DTYPE)


def _rope_tables():
    d = np.arange(LANES) % ATT_HD
    inv_freq = 1.0 / (ROPE_THETA ** (np.arange(0, ROPE_ROT, 2, dtype=np.float32) / ROPE_ROT))
    half = ROPE_ROT // 2
    tab = np.zeros((8, LANES), np.float32)
    tab[0] = np.where(d < ROPE_ROT, inv_freq[d % half], 0.0)
    tab[1] = np.where(d < half, -1.0, 0.0)
    tab[2] = np.where((d >= half) & (d < ROPE_ROT), 1.0, 0.0)
    return jnp.asarray(tab)


def _seg_mean_matrix(width, seg):
    g = (np.arange(width)[:, None] // seg) == (np.arange(width)[None, :] // seg)
    return jnp.asarray(g.astype(np.float32) / seg)


def _rope(x, cos, sina, sinb):
    W = x.shape[1]
    reps = W // LANES
    if reps > 1:
        cos = jnp.concatenate([cos] * reps, axis=1)
        sina = jnp.concatenate([sina] * reps, axis=1)
        sinb = jnp.concatenate([sinb] * reps, axis=1)
    half = ROPE_ROT // 2
    up = pltpu.roll(x, W - half, axis=1)
    down = pltpu.roll(x, half, axis=1)
    return x * cos + up * sina + down * sinb


def _att_prep_kernel(q_ref, k_ref, v_ref, iq_ref, ikiw_ref, pos_ref, tab_ref, seg_ref,
                     qg_ref, kg_ref, qh_ref, kT_ref, vh_ref, iqh_ref, ikT_ref):
    tab = tab_ref[...]
    ang = pos_ref[...].astype(F32) * tab[0:1, :]
    cos = jnp.cos(ang)
    sin = jnp.sin(ang)
    sina = sin * tab[1:2, :]
    sinb = sin * tab[2:3, :]
    seg = seg_ref[...]

    def head_norm(x, g):
        ms = _mm_hi(x * x, seg)
        return x * lax.rsqrt(ms + NORM_EPS) * g

    q = _rope(head_norm(q_ref[...], qg_ref[...]), cos, sina, sinb) * (ATT_HD ** -0.5)
    k = _rope(head_norm(k_ref[...], kg_ref[...]), cos, sina, sinb)
    v = v_ref[...]
    iq = _rope(iq_ref[...], cos, sina, sinb)
    ik = _rope(ikiw_ref[...], cos, sina, sinb)
    for h in range(ATT_HEADS):
        qh_ref[h] = q[:, h * ATT_HD:(h + 1) * ATT_HD].astype(qh_ref.dtype)
        vh_ref[h] = v[:, h * ATT_HD:(h + 1) * ATT_HD].astype(vh_ref.dtype)
    for h in range(IDX_HEADS):
        iqh_ref[h] = iq[:, h * IDX_HD:(h + 1) * IDX_HD].astype(iqh_ref.dtype)
    kT_ref[...] = k.T.astype(kT_ref.dtype)
    ikT_ref[...] = ik.T[:IDX_HD, :].astype(ikT_ref.dtype)


def _att_prep(proj, positions, att_q_gain, att_k_gain, tq):
    S = proj.shape[0]
    qg = jnp.tile(att_q_gain.reshape(1, ATT_HD), (1, ATT_HEADS))
    kg = jnp.tile(att_k_gain.reshape(1, ATT_HD), (1, ATT_HEADS))
    col = lambda off, w: (lambda i: (i, off // w))
    const = lambda i: (0, 0)
    return pl.pallas_call(
        _att_prep_kernel,
        grid=(S // tq,),
        in_specs=[
            pl.BlockSpec((tq, ATT_W), col(COL_Q, ATT_W)),
            pl.BlockSpec((tq, ATT_W), col(COL_K, ATT_W)),
            pl.BlockSpec((tq, ATT_W), col(COL_V, ATT_W)),
            pl.BlockSpec((tq, IDX_HEADS * IDX_HD), col(COL_IQ, IDX_HEADS * IDX_HD)),
            pl.BlockSpec((tq, LANES), col(COL_IKIW, LANES)),
            pl.BlockSpec((tq, 1), lambda i: (i, 0)),
            pl.BlockSpec((8, LANES), const),
            pl.BlockSpec((ATT_W, ATT_W), const),
            pl.BlockSpec((1, ATT_W), const),
            pl.BlockSpec((1, ATT_W), const),
        ],
        out_specs=[
            pl.BlockSpec((ATT_HEADS, tq, ATT_HD), lambda i: (0, i, 0)),
            pl.BlockSpec((ATT_W, tq), lambda i: (0, i)),
            pl.BlockSpec((ATT_HEADS, tq, ATT_HD), lambda i: (0, i, 0)),
            pl.BlockSpec((IDX_HEADS, tq, IDX_HD), lambda i: (0, i, 0)),
            pl.BlockSpec((IDX_HD, tq), lambda i: (0, i)),
        ],
        out_shape=[
            jax.ShapeDtypeStruct((ATT_HEADS, S, ATT_HD), MXU_DTYPE),
            jax.ShapeDtypeStruct((ATT_W, S), MXU_DTYPE),
            jax.ShapeDtypeStruct((ATT_HEADS, S, ATT_HD), MXU_DTYPE),
            jax.ShapeDtypeStruct((IDX_HEADS, S, IDX_HD), MXU_DTYPE),
            jax.ShapeDtypeStruct((IDX_HD, S), MXU_DTYPE),
        ],
        compiler_params=_cparams(("parallel",)),
    )(proj, proj, proj, proj, proj, positions.reshape(S, 1), _rope_tables(),
      _seg_mean_matrix(ATT_W, ATT_HD), qg, kg)


def _sel_kernel(iq_ref, ikiw_ref, ikT_ref, bias_ref, key_ref, *, tq, tk, seq, nsel):
    i = pl.program_id(0)
    row0 = i * tq
    nc = (row0 + tq + tk - 1) // tk
    nfold = tk // LANES
    rows = row0 + lax.broadcasted_iota(I32, (tq, tk), 0)
    cols = lax.broadcasted_iota(I32, (tq, tk), 1)
    idx_scale = (IDX_HEADS ** -0.5) * (IDX_HD ** -0.5)
    iw = ikiw_ref[...][:, IDX_HD:IDX_HD + IDX_HEADS] * idx_scale

    def score_chunk(c, carry):
        c0 = pl.multiple_of(c * tk, tk)
        ikc = ikT_ref[:, pl.ds(c0, tk)]
        sc = None
        for h in range(IDX_HEADS):
            logit = jnp.dot(iq_ref[h], ikc, preferred_element_type=F32)
            t = jnp.maximum(logit, 0.0) * iw[:, h:h + 1]
            sc = t if sc is None else sc + t
        bits = pltpu.bitcast(sc, I32)
        key = bits ^ ((bits >> 31) & 0x7FFFFFFF)
        key_ref[:, pl.ds(c0, tk)] = jnp.where(c0 + cols <= rows, key, INT_MIN)
        return carry

    lax.fori_loop(0, nc, score_chunk, 0)

    def count(pred):
        def body(c, acc):
            c0 = pl.multiple_of(c * tk, tk)
            m = pred(key_ref[:, pl.ds(c0, tk)], c0).astype(I32)
            part = m[:, :LANES]
            for f in range(1, nfold):
                part = part + m[:, f * LANES:(f + 1) * LANES]
            return acc + part
        acc = lax.fori_loop(0, nc, body, jnp.zeros((tq, LANES), I32))
        return jnp.sum(acc, axis=1, keepdims=True)

    def count_ge(cand):
        return count(lambda kc, c0: kc >= cand)

    zero = jnp.zeros((tq, 1), I32)
    prefix = jnp.where(count_ge(zero) >= nsel, zero, zero + INT_MIN)

    def bit_body(b, prefix):
        cand = prefix | jnp.left_shift(jnp.int32(1), 30 - b)
        return jnp.where(count_ge(cand) >= nsel, cand, prefix)

    tau = lax.fori_loop(0, 31, bit_body, prefix)
    n_gt = count(lambda kc, c0: kc > tau)
    n_ge = count_ge(tau)
    need = nsel - n_gt

    def tie_cut():
        def body(b, pj):
            cand = pj | jnp.left_shift(jnp.int32(1), 14 - b)
            cnt = count(lambda kc, c0: (kc == tau) & (c0 + cols < cand))
            return jnp.where(cnt < need, cand, pj)
        return lax.fori_loop(0, 15, body, zero)

    jstar = lax.cond(jnp.max(n_ge) > nsel, tie_cut, lambda: zero + seq)

    def write_chunk(c, carry):
        c0 = pl.multiple_of(c * tk, tk)
        kc = key_ref[:, pl.ds(c0, tk)]
        col = c0 + cols
        sel = ((kc > tau) | ((kc == tau) & (col <= jstar))) & (col <= rows)
        bias_ref[:, pl.ds(c0, tk)] = jnp.where(sel, 0.0, NEG_BIG).astype(bias_ref.dtype)
        return carry

    lax.fori_loop(0, nc, write_chunk, 0)

    def fill_chunk(c, carry):
        c0 = pl.multiple_of(c * tk, tk)
        bias_ref[:, pl.ds(c0, tk)] = jnp.full((tq, tk), NEG_BIG, bias_ref.dtype)
        return carry

    lax.fori_loop(nc, seq // tk, fill_chunk, 0)


def _dsa_select(iqh, proj, ikT, tq, tk):
    S = ikT.shape[1]
    nsel = min(TOPK_MAX, S // 4)
    kern = functools.partial(_sel_kernel, tq=tq, tk=tk, seq=S, nsel=nsel)
    return pl.pallas_call(
        kern,
        grid=(S // tq,),
        in_specs=[
            pl.BlockSpec((IDX_HEADS, tq, IDX_HD), lambda i: (0, i, 0)),
            pl.BlockSpec((tq, LANES), lambda i: (i, COL_IKIW // LANES)),
            pl.BlockSpec((IDX_HD, S), lambda i: (0, 0)),
        ],
        out_specs=pl.BlockSpec((tq, S), lambda i: (i, 0)),
        out_shape=jax.ShapeDtypeStruct((S, S), jnp.bfloat16),
        scratch_shapes=[pltpu.VMEM((tq, S), I32)],
        compiler_params=_cparams(("parallel",)),
    )(iqh, proj, ikT)


def _att_kernel(q_ref, kT_ref, v_ref, bias_ref, o_ref, m_ref, l_ref, acc_ref, *, tq, tk, nk):
    i = pl.program_id(0)
    j = pl.program_id(1)
    last = ((i + 1) * tq - 1) // tk

    @pl.when(j == 0)
    def _():
        m_ref[...] = jnp.full(m_ref.shape, -jnp.inf, F32)
        l_ref[...] = jnp.zeros(l_ref.shape, F32)
        acc_ref[...] = jnp.zeros(acc_ref.shape, F32)

    @pl.when(j <= last)
    def _():
        bias = bias_ref[...].astype(F32)
        for h in range(ATT_HEADS):
            s = jnp.dot(q_ref[h], kT_ref[h * ATT_HD:(h + 1) * ATT_HD, :],
                        preferred_element_type=F32) + bias
            m_prev = m_ref[h]
            m_new = jnp.maximum(m_prev, jnp.max(s, axis=1, keepdims=True))
            alpha = jnp.exp(m_prev - m_new)
            p = jnp.exp(s - m_new[:, 0:1])
            l_ref[h] = alpha * l_ref[h] + jnp.sum(p, axis=1, keepdims=True)
            acc_ref[h] = alpha[:, :ATT_HD] * acc_ref[h] + jnp.dot(
                p.astype(v_ref.dtype), v_ref[h], preferred_element_type=F32)
            m_ref[h] = m_new

    @pl.when(j == nk - 1)
    def _():
        o_ref[...] = jnp.concatenate(
            [acc_ref[h] / l_ref[h][:, :ATT_HD] for h in range(ATT_HEADS)], axis=1)


def _dsa_attend(qh, kT, vh, bias, tq, tk):
    S = kT.shape[1]
    nk = S // tk
    last = lambda i: ((i + 1) * tq - 1) // tk
    kern = functools.partial(_att_kernel, tq=tq, tk=tk, nk=nk)
    return pl.pallas_call(
        kern,
        grid=(S // tq, nk),
        in_specs=[
            pl.BlockSpec((ATT_HEADS, tq, ATT_HD), lambda i, j: (0, i, 0)),
            pl.BlockSpec((ATT_W, tk), lambda i, j: (0, jnp.minimum(j, last(i)))),
            pl.BlockSpec((ATT_HEADS, tk, ATT_HD), lambda i, j: (0, jnp.minimum(j, last(i)), 0)),
            pl.BlockSpec((tq, tk), lambda i, j: (i, jnp.minimum(j, last(i)))),
        ],
        out_specs=pl.BlockSpec((tq, ATT_W), lambda i, j: (i, 0)),
        out_shape=jax.ShapeDtypeStruct((S, ATT_W), F32),
        scratch_shapes=[
            pltpu.VMEM((ATT_HEADS, tq, LANES), F32),
            pltpu.VMEM((ATT_HEADS, tq, LANES), F32),
            pltpu.VMEM((ATT_HEADS, tq, ATT_HD), F32),
        ],
        compiler_params=_cparams(("parallel", "arbitrary")),
    )(qh, kT, vh, bias)


CHUNK = 64


def _shift_rows(x, prev8, first):
    prev_row = jnp.where(first, 0.0, prev8[7:8, :])
    row = lax.broadcasted_iota(I32, x.shape, 0)
    return jnp.where(row == 0, prev_row, pltpu.roll(x, 1, axis=0))


def _softplus(z):
    return jnp.maximum(z, 0.0) + jnp.log1p(jnp.exp(-jnp.abs(z)))


def _rwkv_prep_kernel(r_ref, k_ref, v_ref, lo_ref, rp_ref, kp_ref, vp_ref, lop_ref,
                      mur_ref, muk_ref, muv_ref, mulo_ref, w0_ref, w2_ref, a0_ref, a2_ref,
                      g2_ref, kk_ref, ka_ref, rk_ref, seg_ref, tri_ref, ones_ref, end_ref,
                      rt_ref, kt_ref, bt_ref, at_ref, kh_ref, bh_ref, vh_ref, gam_ref,
                      bonus_ref, g_ref):
    first = pl.program_id(0) == 0

    def mix(x_ref, p_ref, mu_ref):
        x = x_ref[...]
        return x + (_shift_rows(x, p_ref[...], first) - x) * mu_ref[...]

    r = mix(r_ref, rp_ref, mur_ref)
    k = mix(k_ref, kp_ref, muk_ref)
    v = mix(v_ref, vp_ref, muv_ref)
    lo = mix(lo_ref, lop_ref, mulo_ref)
    wd = lo[:, :DECAY_LORA]
    ad = lo[:, DECAY_LORA:DECAY_LORA + AAA_LORA]
    gd = lo[:, DECAY_LORA + AAA_LORA:]
    w_log = -_softplus(-(w0_ref[...] + _mm(jnp.tanh(wd), w2_ref[...]))) - 0.5
    lw = -jnp.exp(w_log)
    a = jax.nn.sigmoid(a0_ref[...] + _mm(ad, a2_ref[...]))
    g_ref[...] = _mm(jax.nn.sigmoid(gd), g2_ref[...])
    seg = seg_ref[...]
    kk = k * kk_ref[...]
    kk = kk / jnp.maximum(jnp.sqrt(_mm_hi(kk * kk, seg)), 1e-12)
    k = k * (1.0 + (a - 1.0) * ka_ref[...])
    bonus_ref[...] = _mm_hi(r * k * rk_ref[...], seg) * v
    avec = -kk
    bvec = kk * a
    cs = _mm_hi(tri_ref[...], lw)
    cs_end = _mm_hi(ones_ref[...], lw)
    e_neg = jnp.exp(-cs)
    e_end = jnp.exp(cs_end - cs)
    outs = ((rt_ref, r * jnp.exp(cs)), (kt_ref, k * e_neg), (bt_ref, bvec * e_neg),
            (at_ref, avec * jnp.exp(cs - lw)), (kh_ref, k * e_end), (bh_ref, bvec * e_end),
            (vh_ref, v), (gam_ref, jnp.exp(_mm_hi(end_ref[...], lw))))
    for ref, val in outs:
        for h in range(RWKV_HEADS):
            ref[h] = val[:, h * RWKV_HD:(h + 1) * RWKV_HD]


def _rwkv_prep(proj, p, tq):
    S = proj.shape[0]
    nch = tq // CHUNK
    col = lambda off, w: (lambda i: (i, off // w))
    prev = lambda off, w: (lambda i: (jnp.maximum(i * (tq // 8) - 1, 0), off // w))
    const = lambda i: (0, 0)
    lw_ = DECAY_LORA + AAA_LORA + GATE_LORA
    mu = p["rwkv_mu"]
    row = lambda t: t.reshape(1, -1)
    t_idx = np.arange(tq)
    same = (t_idx[:, None] // CHUNK) == (t_idx[None, :] // CHUNK)
    tri = jnp.asarray((same & (t_idx[None, :] <= t_idx[:, None])).astype(np.float32))
    ones = jnp.asarray(same.astype(np.float32))
    end = jnp.asarray(((t_idx[None, :] // CHUNK) == np.arange(nch)[:, None]).astype(np.float32))
    seg = _seg_mean_matrix(RWKV_W, RWKV_HD) * RWKV_HD
    hm = jax.ShapeDtypeStruct((RWKV_HEADS, S, RWKV_HD), F32)
    hm_spec = pl.BlockSpec((RWKV_HEADS, tq, RWKV_HD), lambda i: (0, i, 0))
    wide = jax.ShapeDtypeStruct((S, RWKV_W), F32)
    wide_spec = pl.BlockSpec((tq, RWKV_W), lambda i: (i, 0))
    vec = lambda w: pl.BlockSpec((1, w), const)
    return pl.pallas_call(
        _rwkv_prep_kernel,
        grid=(S // tq,),
        in_specs=[
            pl.BlockSpec((tq, RWKV_W), col(COL_R, RWKV_W)),
            pl.BlockSpec((tq, RWKV_W), col(COL_RK, RWKV_W)),
            pl.BlockSpec((tq, RWKV_W), col(COL_RV, RWKV_W)),
            pl.BlockSpec((tq, lw_), col(COL_LORA, lw_)),
            pl.BlockSpec((8, RWKV_W), prev(COL_R, RWKV_W)),
            pl.BlockSpec((8, RWKV_W), prev(COL_RK, RWKV_W)),
            pl.BlockSpec((8, RWKV_W), prev(COL_RV, RWKV_W)),
            pl.BlockSpec((8, lw_), prev(COL_LORA, lw_)),
            vec(RWKV_W), vec(RWKV_W), vec(RWKV_W), vec(lw_),
            vec(RWKV_W), pl.BlockSpec((DECAY_LORA, RWKV_W), const),
            vec(RWKV_W), pl.BlockSpec((AAA_LORA, RWKV_W), const),
            pl.BlockSpec((GATE_LORA, RWKV_W), const),
            vec(RWKV_W), vec(RWKV_W), vec(RWKV_W),
            pl.BlockSpec((RWKV_W, RWKV_W), const),
            pl.BlockSpec((tq, tq), const), pl.BlockSpec((tq, tq), const),
            pl.BlockSpec((nch, tq), const),
        ],
        out_specs=[hm_spec] * 7 + [pl.BlockSpec((RWKV_HEADS, nch, RWKV_HD), lambda i: (0, i, 0)),
                                   wide_spec, wide_spec],
        out_shape=[hm] * 7 + [jax.ShapeDtypeStruct((RWKV_HEADS, S // CHUNK, RWKV_HD), F32),
                              wide, wide],
        compiler_params=_cparams(("parallel",)),
    )(proj, proj, proj, proj, proj, proj, proj, proj,
      row(mu[:RWKV_W]), row(mu[RWKV_W:2 * RWKV_W]), row(mu[2 * RWKV_W:3 * RWKV_W]),
      row(mu[3 * RWKV_W:]), row(p["rwkv_w0"]), p["rwkv_w2"], row(p["rwkv_a0"]), p["rwkv_a2"],
      p["rwkv_g2"], row(p["rwkv_k_k"]), row(p["rwkv_k_a"]), row(p["rwkv_r_k"]), seg, tri, ones, end)


def _bmm(a, b, dims):
    return jnp.einsum(dims, a, b, preferred_element_type=F32, precision=HI)


def _rwkv_chunk_kernel(rt_ref, kt_ref, bt_ref, at_ref, kh_ref, bh_ref, v_ref, gam_ref,
                       p_ref, q_ref, rw_ref, y0_ref, *, nch):
    L, N = CHUNK, RWKV_HD
    ri = lax.broadcasted_iota(I32, (nch, L, L), 1)
    ci = lax.broadcasted_iota(I32, (nch, L, L), 2)
    eye_n = (lax.broadcasted_iota(I32, (nch, N, N), 1)
             == lax.broadcasted_iota(I32, (nch, N, N), 2)).astype(F32)
    for h in range(RWKV_HEADS):
        ld = lambda ref: ref[h].reshape(nch, L, N)
        rt, kt, bt, at, kh, bh, v = (ld(x) for x in (rt_ref, kt_ref, bt_ref, at_ref, kh_ref,
                                                      bh_ref, v_ref))
        gam = gam_ref[h].reshape(nch, 1, N)
        a_ab = jnp.where(ci < ri, _bmm(at, bt, "cld,cmd->clm"), 0.0)
        a_ak = jnp.where(ci < ri, _bmm(at, kt, "cld,cmd->clm"), 0.0)
        m_rk = jnp.where(ci <= ri, _bmm(rt, kt, "cld,cmd->clm"), 0.0)
        m_rb = jnp.where(ci <= ri, _bmm(rt, bt, "cld,cmd->clm"), 0.0)
        rhs = jnp.concatenate([at, _bmm(a_ak, v, "clm,cmd->cld")], axis=2)
        pw = a_ab
        step = 1
        while True:
            rhs = rhs + _bmm(pw, rhs, "clm,cmd->cld")
            step *= 2
            if step >= L:
                break
            pw = _bmm(pw, pw, "clm,cmn->cln")
        w, u0 = rhs[:, :, :N], rhs[:, :, N:]
        p_ref[:, h] = eye_n * gam + _bmm(bh, w, "cld,cle->cde")
        q_ref[:, h] = _bmm(kh, v, "cld,cle->cde") + _bmm(bh, u0, "cld,cle->cde")
        rw_ref[h] = (rt + _bmm(m_rb, w, "clm,cmd->cld")).reshape(nch * L, N)
        y0_ref[h] = (_bmm(m_rk, v, "clm,cmd->cld") + _bmm(m_rb, u0, "clm,cmd->cld")).reshape(nch * L, N)


def _rwkv_chunks(rt, kt, bt, at, kh, bh, vh, gam, tt):
    S = rt.shape[1]
    nch = tt // CHUNK
    hm_spec = pl.BlockSpec((RWKV_HEADS, tt, RWKV_HD), lambda i: (0, i, 0))
    hm = jax.ShapeDtypeStruct((RWKV_HEADS, S, RWKV_HD), F32)
    sq_spec = pl.BlockSpec((nch, RWKV_HEADS, RWKV_HD, RWKV_HD), lambda i: (i, 0, 0, 0))
    sq = jax.ShapeDtypeStruct((S // CHUNK, RWKV_HEADS, RWKV_HD, RWKV_HD), F32)
    return pl.pallas_call(
        functools.partial(_rwkv_chunk_kernel, nch=nch),
        grid=(S // tt,),
        in_specs=[hm_spec] * 7 + [pl.BlockSpec((RWKV_HEADS, nch, RWKV_HD), lambda i: (0, i, 0))],
        out_specs=[sq_spec, sq_spec, hm_spec, hm_spec],
        out_shape=[sq, sq, hm, hm],
        compiler_params=_cparams(("parallel",)),
    )(rt, kt, bt, at, kh, bh, vh, gam)


def _rwkv_scan_kernel(p_ref, q_ref, rw_ref, y0_ref, bonus_ref, g_ref, seg_ref, lnw_ref, lnb_ref,
                      o_ref, st_ref, *, nch):
    @pl.when(pl.program_id(0) == 0)
    def _():
        st_ref[...] = jnp.zeros(st_ref.shape, F32)

    L = CHUNK
    st = st_ref[...]
    ys = []
    for c in range(nch):
        rw = rw_ref[:, c * L:(c + 1) * L, :]
        ys.append(_bmm(rw, st, "hld,hde->hle") + y0_ref[:, c * L:(c + 1) * L, :])
        st = _bmm(p_ref[c], st, "hjk,hki->hji") + q_ref[c]
    st_ref[...] = st
    y = jnp.concatenate(ys, axis=1)
    y = jnp.concatenate([y[h] for h in range(RWKV_HEADS)], axis=1)
    seg = seg_ref[...]
    mean = _mm_hi(y, seg)
    d = y - mean
    var = _mm_hi(d * d, seg)
    y = d * lax.rsqrt(var + GN_EPS) * lnw_ref[...] + lnb_ref[...]
    o_ref[...] = (y + bonus_ref[...]) * g_ref[...]


def _rwkv_scan(pm, qm, rw, y0, bonus, g, ln_w, ln_b, tt):
    S = rw.shape[1]
    nch = tt // CHUNK
    hm_spec = pl.BlockSpec((RWKV_HEADS, tt, RWKV_HD), lambda i: (0, i, 0))
    sq_spec = pl.BlockSpec((nch, RWKV_HEADS, RWKV_HD, RWKV_HD), lambda i: (i, 0, 0, 0))
    wide_spec = pl.BlockSpec((tt, RWKV_W), lambda i: (i, 0))
    const = lambda i: (0, 0)
    return pl.pallas_call(
        functools.partial(_rwkv_scan_kernel, nch=nch),
        grid=(S // tt,),
        in_specs=[sq_spec, sq_spec, hm_spec, hm_spec, wide_spec, wide_spec,
                  pl.BlockSpec((RWKV_W, RWKV_W), const), pl.BlockSpec((1, RWKV_W), const),
                  pl.BlockSpec((1, RWKV_W), const)],
        out_specs=wide_spec,
        out_shape=jax.ShapeDtypeStruct((S, RWKV_W), F32),
        scratch_shapes=[pltpu.VMEM((RWKV_HEADS, RWKV_HD, RWKV_HD), F32)],
        compiler_params=_cparams(("arbitrary",)),
    )(pm, qm, rw, y0, bonus, g, _seg_mean_matrix(RWKV_W, RWKV_HD), ln_w.reshape(1, -1),
      ln_b.reshape(1, -1))


def _rwkv_time_mix(proj, p):
    outs = _rwkv_prep(proj, p, 512)
    rt, kt, bt, at, kh, bh, vh, gam, bonus, g = outs
    pm, qm, rw, y0 = _rwkv_chunks(rt, kt, bt, at, kh, bh, vh, gam, 512)
    return _rwkv_scan(pm, qm, rw, y0, bonus, g, p["rwkv_ln_w"], p["rwkv_ln_b"], 512)


def _lane_rmsnorm(x, gain):
    return x * lax.rsqrt(jnp.mean(x * x, axis=-1, keepdims=True) + NORM_EPS) * gain


def _mem_kv_kernel(mem_ref, g_ref, w_ref, kg_ref, kmT_ref, vm_ref):
    m = _lane_rmsnorm(mem_ref[...], g_ref[...])
    kv = _mm(m, w_ref[...])
    km = jnp.concatenate(
        [_lane_rmsnorm(kv[:, h * MEM_HD:(h + 1) * MEM_HD], kg_ref[...]) for h in range(MEM_HEADS)],
        axis=1)
    kmT_ref[...] = km.T.astype(kmT_ref.dtype)
    vm_ref[...] = kv[:, MEM_W:].astype(vm_ref.dtype)


def _mem_kv(mem, gain, w_kv, k_gain):
    M = mem.shape[0]
    return pl.pallas_call(
        _mem_kv_kernel,
        out_shape=[jax.ShapeDtypeStruct((MEM_W, M), MXU_DTYPE),
                   jax.ShapeDtypeStruct((M, MEM_W), MXU_DTYPE)],
        compiler_params=pltpu.CompilerParams(vmem_limit_bytes=VMEM_LIMIT),
    )(mem, gain.reshape(1, -1), w_kv.astype(MXU_DTYPE), k_gain.reshape(1, -1))


def _merge_kernel(x_ref, yatt_ref, yrwkv_ref, memq_ref, gate_ref, kmT_ref, vm_ref, qg_ref,
                  gb_ref, wup_ref, wout_ref, o_ref):
    q = memq_ref[...]
    heads = []
    for h in range(MEM_HEADS):
        sl = slice(h * MEM_HD, (h + 1) * MEM_HD)
        qn = _lane_rmsnorm(q[:, sl], qg_ref[...])
        s = _mm(qn, kmT_ref[sl, :]) * (MEM_HD ** -0.5)
        s = s - jnp.max(s, axis=-1, keepdims=True)
        e = jnp.exp(s)
        p = e / jnp.sum(e, axis=-1, keepdims=True)
        heads.append(_mm(p, vm_ref[:, sl]))
    ymem = jnp.concatenate(heads, axis=1)
    merged = None
    for c, y in enumerate((yatt_ref[...], yrwkv_ref[...], ymem)):
        up = _mm(y, wup_ref[c])
        gate = jax.nn.sigmoid(gate_ref[:, c * D_MODEL:(c + 1) * D_MODEL] + gb_ref[c:c + 1, :])
        merged = gate * up if merged is None else merged + gate * up
    o_ref[...] = x_ref[...] + _mm(merged, wout_ref[...])


def _merge(x, y_att, y_rwkv, proj, kmT, vm, mem_q_gain, gate_bias, w_up, w_out, tq):
    S = x.shape[0]
    M = vm.shape[0]
    gw = N_BRANCH * D_MODEL
    const2 = lambda i: (0, 0)
    return pl.pallas_call(
        _merge_kernel,
        grid=(S // tq,),
        in_specs=[
            pl.BlockSpec((tq, D_MODEL), lambda i: (i, 0)),
            pl.BlockSpec((tq, ATT_W), lambda i: (i, 0)),
            pl.BlockSpec((tq, RWKV_W), lambda i: (i, 0)),
            pl.BlockSpec((tq, MEM_W), lambda i: (i, COL_MEMQ // MEM_W)),
            pl.BlockSpec((tq, gw), lambda i: (i, COL_GATE // gw)),
            pl.BlockSpec((MEM_W, M), const2),
            pl.BlockSpec((M, MEM_W), const2),
            pl.BlockSpec((1, MEM_HD), const2),
            pl.BlockSpec((N_BRANCH, D_MODEL), const2),
            pl.BlockSpec((N_BRANCH, ATT_W, D_MODEL), lambda i: (0, 0, 0)),
            pl.BlockSpec((D_MODEL, D_MODEL), const2),
        ],
        out_specs=pl.BlockSpec((tq, D_MODEL), lambda i: (i, 0)),
        out_shape=jax.ShapeDtypeStruct((S, D_MODEL), F32),
        compiler_params=_cparams(("parallel",)),
    )(x, y_att, y_rwkv, proj, proj, kmT, vm, mem_q_gain.reshape(1, -1), gate_bias,
      w_up.astype(MXU_DTYPE), w_out.astype(MXU_DTYPE))


def _pop_max(cur, rows):
    m = jnp.max(cur, axis=0, keepdims=True)
    first = jnp.min(jnp.where(cur == m, rows, cur.shape[0]), axis=0, keepdims=True)
    return m, jnp.where(rows == first, -jnp.inf, cur)


def _peer_route_kernel(h_ref, g_ref, wq_ref, k1_ref, k2_ref,
                       xn_ref, s1_ref, s2_ref, p1_ref, p2_ref, tau_ref):
    tq = h_ref.shape[0]
    xn = _lane_rmsnorm(h_ref[...], g_ref[...]).astype(xn_ref.dtype)
    xn_ref[...] = xn
    qp = jnp.dot(xn, wq_ref[...], preferred_element_type=F32)
    nt = (((1,), (1,)), ((), ()))
    half = PEER_QDIM // 2
    rows = lax.broadcasted_iota(I32, (PEER_KEYS, tq), 0)
    pairs = [(a, b) for a in range(PEER_TOPK) for b in range(PEER_TOPK)
             if (a + 1) * (b + 1) <= PEER_TOPK]
    npad = -len(pairs) % 8
    crow = lax.broadcasted_iota(I32, (len(pairs) + npad, tq), 0)
    taus = []
    for h in range(PEER_HEADS):
        q1 = qp[:, h * PEER_QDIM:h * PEER_QDIM + half].astype(MXU_DTYPE)
        q2 = qp[:, h * PEER_QDIM + half:(h + 1) * PEER_QDIM].astype(MXU_DTYPE)
        s1 = lax.dot_general(k1_ref[...], q1, nt, preferred_element_type=F32)
        s2 = lax.dot_general(k2_ref[...], q2, nt, preferred_element_type=F32)
        tops = []
        for s in (s1, s2):
            cur, vals = s, []
            for _ in range(PEER_TOPK):
                m, cur = _pop_max(cur, rows)
                vals.append(m)
            tops.append(vals)
        v1, v2 = tops
        cand = jnp.concatenate([v1[a] + v2[b] for a, b in pairs]
                               + [jnp.full((npad, tq), -jnp.inf, F32)], axis=0)
        top = v1[0] + v2[0]
        z = jnp.zeros((1, tq), F32)
        for _ in range(PEER_TOPK):
            m, cand = _pop_max(cand, crow)
            z = z + jnp.exp(m - top)
        taus.append(m)
        s1_ref[h] = s1
        s2_ref[h] = s2
        p1_ref[h] = jnp.exp(s1 - v1[0]) / z
        p2_ref[h] = jnp.exp(s2 - v2[0])
    tau_ref[...] = jnp.concatenate(taus, axis=0)


def _peer_route(h2, gain, w_q, key1, key2, tq):
    S = h2.shape[0]
    const2 = lambda i: (0, 0)
    half = PEER_QDIM // 2
    kt = jax.ShapeDtypeStruct((PEER_HEADS, PEER_KEYS, S), F32)
    kt_spec = pl.BlockSpec((PEER_HEADS, PEER_KEYS, tq), lambda i: (0, 0, i))
    return pl.pallas_call(
        _peer_route_kernel,
        grid=(S // tq,),
        in_specs=[
            pl.BlockSpec((tq, D_MODEL), lambda i: (i, 0)),
            pl.BlockSpec((1, D_MODEL), const2),
            pl.BlockSpec((D_MODEL, PEER_HEADS * PEER_QDIM), const2),
            pl.BlockSpec((PEER_KEYS, half), const2),
            pl.BlockSpec((PEER_KEYS, half), const2),
        ],
        out_specs=[pl.BlockSpec((tq, D_MODEL), lambda i: (i, 0)), kt_spec, kt_spec, kt_spec,
                   kt_spec, pl.BlockSpec((PEER_HEADS, tq), lambda i: (0, i))],
        out_shape=[jax.ShapeDtypeStruct((S, D_MODEL), MXU_DTYPE), kt, kt, kt, kt,
                   jax.ShapeDtypeStruct((PEER_HEADS, S), F32)],
        compiler_params=_cparams(("parallel",)),
    )(h2, gain.reshape(1, -1), w_q.astype(MXU_DTYPE), key1.astype(MXU_DTYPE),
      key2.astype(MXU_DTYPE))


def _peer_dense_kernel(xn_ref, u_ref, vT_ref, s1_ref, s2_ref, p1_ref, p2_ref, tau_ref, h_ref,
                       o_ref, acc_ref, x_ref, *, te, ne):
    j = pl.program_id(1)

    @pl.when(j == 0)
    def _():
        acc_ref[...] = jnp.zeros(acc_ref.shape, F32)

    nt = (((1,), (1,)), ((), ()))
    hid = lax.dot_general(u_ref[...], xn_ref[...], nt, preferred_element_type=F32)
    act = 0.5 * hid * (1.0 + lax.erf(hid * (2.0 ** -0.5)))
    tau = tau_ref[...]
    for el in range(te // PEER_KEYS):
        e1 = j * (te // PEER_KEYS) + el
        g = None
        for h in range(PEER_HEADS):
            s1r = s1_ref[h, pl.ds(e1, 1), :]
            p1r = p1_ref[h, pl.ds(e1, 1), :]
            keep = (s1r + s2_ref[h]) >= tau[h:h + 1, :]
            t = jnp.where(keep, p2_ref[h], 0.0) * p1r
            g = t if g is None else g + t
        sl = slice(el * PEER_KEYS, (el + 1) * PEER_KEYS)
        x_ref[sl, :] = (g * act[sl, :]).astype(x_ref.dtype)
    acc_ref[...] += jnp.dot(vT_ref[...], x_ref[...], preferred_element_type=F32)

    @pl.when(j == ne - 1)
    def _():
        o_ref[...] = h_ref[...] + acc_ref[...].T


def _peer_dense(h2, xn, u, v, s1, s2, p1, p2, tau, tm, te):
    S = h2.shape[0]
    NE = u.shape[0]
    ne = NE // te
    kt_spec = pl.BlockSpec((PEER_HEADS, PEER_KEYS, tm), lambda i, j: (0, 0, i))
    return pl.pallas_call(
        functools.partial(_peer_dense_kernel, te=te, ne=ne),
        grid=(S // tm, ne),
        in_specs=[
            pl.BlockSpec((tm, D_MODEL), lambda i, j: (i, 0)),
            pl.BlockSpec((te, D_MODEL), lambda i, j: (j, 0)),
            pl.BlockSpec((D_MODEL, te), lambda i, j: (0, j)),
            kt_spec, kt_spec, kt_spec, kt_spec,
            pl.BlockSpec((PEER_HEADS, tm), lambda i, j: (0, i)),
            pl.BlockSpec((tm, D_MODEL), lambda i, j: (i, 0)),
        ],
        out_specs=pl.BlockSpec((tm, D_MODEL), lambda i, j: (i, 0)),
        out_shape=jax.ShapeDtypeStruct((S, D_MODEL), F32),
        scratch_shapes=[pltpu.VMEM((D_MODEL, tm), F32), pltpu.VMEM((te, tm), MXU_DTYPE)],
        compiler_params=_cparams(("parallel", "arbitrary")),
    )(xn, u.astype(MXU_DTYPE), v.T.astype(MXU_DTYPE), s1, s2, p1, p2, tau, h2)


def _peer(h2, gain, w_q, key1, key2, u, v):
    S = h2.shape[0]
    xn, s1, s2, p1, p2, tau = _peer_route(h2, gain, w_q, key1, key2, min(256, S))
    return _peer_dense(h2, xn, u, v, s1, s2, p1, p2, tau, min(512, S), 1024)


def _layer(x, mem, positions, p):
    S = x.shape[0]
    proj = _norm_proj(x, p["mix_norm_gain"], _pack_w_in(p["w_in"]), min(1024, S), 768)
    qh, kT, vh, iqh, ikT = _att_prep(proj, positions, p["att_q_gain"], p["att_k_gain"], 256)
    bias = _dsa_select(iqh, proj, ikT, 128, 512)
    y_att = _dsa_attend(qh, kT, vh, bias, 256, 512)
    y_rwkv = _rwkv_time_mix(proj, p)
    kmT, vm = _mem_kv(mem, p["mem_norm_gain"], p["w_mem_kv"], p["mem_k_gain"])
    h2 = _merge(x, y_att, y_rwkv, proj, kmT, vm, p["mem_q_gain"], p["gate_bias"], p["w_up"],
                p["w_out"], 256)
    return _peer(h2, p["ffn_norm_gain"], p["peer_w_q"], p["peer_key1"], p["peer_key2"],
                 p["peer_u"], p["peer_v"])


_PARAM_NAMES = ("mix_norm_gain", "w_in", "gate_bias", "att_q_gain", "att_k_gain", "mem_norm_gain",
                "w_mem_kv", "mem_q_gain", "mem_k_gain", "rwkv_mu", "rwkv_w0", "rwkv_w2", "rwkv_a0",
                "rwkv_a2", "rwkv_g2", "rwkv_k_k", "rwkv_k_a", "rwkv_r_k", "rwkv_ln_w", "rwkv_ln_b",
                "w_up", "w_out", "ffn_norm_gain", "peer_w_q", "peer_key1", "peer_key2", "peer_u",
                "peer_v")


def kernel(x, mem, positions, mix_norm_gain, w_in, gate_bias, att_q_gain, att_k_gain,
           mem_norm_gain, w_mem_kv, mem_q_gain, mem_k_gain, rwkv_mu, rwkv_w0, rwkv_w2, rwkv_a0,
           rwkv_a2, rwkv_g2, rwkv_k_k, rwkv_k_a, rwkv_r_k, rwkv_ln_w, rwkv_ln_b, w_up, w_out,
           ffn_norm_gain, peer_w_q, peer_key1, peer_key2, peer_u, peer_v):
    params = (mix_norm_gain, w_in, gate_bias, att_q_gain, att_k_gain, mem_norm_gain, w_mem_kv,
              mem_q_gain, mem_k_gain, rwkv_mu, rwkv_w0, rwkv_w2, rwkv_a0, rwkv_a2, rwkv_g2,
              rwkv_k_k, rwkv_k_a, rwkv_r_k, rwkv_ln_w, rwkv_ln_b, w_up, w_out, ffn_norm_gain,
              peer_w_q, peer_key1, peer_key2, peer_u, peer_v)
    assert x.shape[0] == 1 and all(t.shape[0] == 1 for t in params)
    p = {name: t[0] for name, t in zip(_PARAM_NAMES, params)}
    return _layer(x[0], mem[0], positions[0], p)[None]
```

```python
import functools

import numpy as np
import jax
import jax.numpy as jnp
from jax import lax
from jax.experimental import pallas as pl
from jax.experimental.pallas import tpu as pltpu

F32 = jnp.float32
I32 = jnp.int32
MXU_DTYPE = jnp.bfloat16
HI = lax.Precision.HIGHEST

D_MODEL = 1024
ATT_HEADS, ATT_HD = 8, 64
ATT_W = ATT_HEADS * ATT_HD
IDX_HEADS, IDX_HD = 4, 64
TOPK_MAX = 256
RWKV_HEADS, RWKV_HD = 8, 64
RWKV_W = RWKV_HEADS * RWKV_HD
DECAY_LORA, AAA_LORA, GATE_LORA = 64, 64, 128
GN_EPS = 64e-5
MEM_HEADS, MEM_HD = 4, 128
MEM_W = MEM_HEADS * MEM_HD
N_BRANCH = 3
ROPE_THETA = 500000.0
ROPE_ROT = ATT_HD // 4
NORM_EPS = 1e-6
PEER_KEYS = 128
PEER_HEADS = 8
PEER_QDIM = 256
PEER_TOPK = 16

LANES = 128
INT_MIN = -(2 ** 31)
NEG_BIG = -1e30
VMEM_LIMIT = 56 * 1024 * 1024

COL_GATE = 0
COL_Q = 3072
COL_K = 3584
COL_V = 4096
COL_R = 4608
COL_RK = 5120
COL_RV = 5632
COL_MEMQ = 6144
COL_IQ = 6656
COL_LORA = 6912
COL_IKIW = 7168
PROJ_COLS = 7680


def _cparams(sem):
    return pltpu.CompilerParams(dimension_semantics=sem, vmem_limit_bytes=VMEM_LIMIT)


def _mm(a, b):
    return jnp.dot(a.astype(MXU_DTYPE), b.astype(MXU_DTYPE), preferred_element_type=F32)


def _mm_hi(a, b):
    return jnp.dot(a.astype(F32), b.astype(F32), preferred_element_type=F32, precision=HI)


def _normproj_kernel(x_ref, g_ref, w_ref, o_ref, xn_ref):
    @pl.when(pl.program_id(1) == 0)
    def _():
        x = x_ref[...]
        ms = jnp.mean(x * x, axis=-1, keepdims=True)
        xn_ref[...] = (x * lax.rsqrt(ms + NORM_EPS) * g_ref[...]).astype(xn_ref.dtype)

    o_ref[...] = jnp.dot(xn_ref[...], w_ref[...], preferred_element_type=F32)


def _norm_proj(x, gain, w, tm, tn):
    S, D = x.shape
    N = w.shape[1]
    return pl.pallas_call(
        _normproj_kernel,
        grid=(S // tm, N // tn),
        in_specs=[
            pl.BlockSpec((tm, D), lambda i, j: (i, 0)),
            pl.BlockSpec((1, D), lambda i, j: (0, 0)),
            pl.BlockSpec((D, tn), lambda i, j: (0, j)),
        ],
        out_specs=pl.BlockSpec((tm, tn), lambda i, j: (i, j)),
        out_shape=jax.ShapeDtypeStruct((S, N), F32),
        scratch_shapes=[pltpu.VMEM((tm, D), w.dtype)],
        compiler_params=_cparams(("parallel", "arbitrary")),
    )(x, gain.reshape(1, D), w)


def _pack_w_in(w_in):
    D = w_in.shape[0]
    o = 0
    parts = {}
    for name, width in (("q", ATT_W), ("k", ATT_W), ("v", ATT_W), ("iq", IDX_HEADS * IDX_HD),
                        ("ik", IDX_HD), ("iw", IDX_HEADS), ("r", RWKV_W), ("rk", RWKV_W),
                        ("rv", RWKV_W), ("lora", DECAY_LORA + AAA_LORA + GATE_LORA),
                        ("memq", MEM_W), ("gate", N_BRANCH * D_MODEL)):
        parts[name] = w_in[:, o:o + width]
        o += width
    ikiw = jnp.concatenate([parts["ik"], parts["iw"]], axis=1)
    ikiw = jnp.pad(ikiw, ((0, 0), (0, LANES - ikiw.shape[1])))
    packed = jnp.concatenate(
        [parts["gate"], parts["q"], parts["k"], parts["v"], parts["r"], parts["rk"], parts["rv"],
         parts["memq"], parts["iq"], parts["lora"], ikiw], axis=1)
    packed = jnp.pad(packed, ((0, 0), (0, PROJ_COLS - packed.shape[1])))
    return packed.astype(MXU_DTYPE)


def _rope_tables():
    d = np.arange(LANES) % ATT_HD
    inv_freq = 1.0 / (ROPE_THETA ** (np.arange(0, ROPE_ROT, 2, dtype=np.float32) / ROPE_ROT))
    half = ROPE_ROT // 2
    tab = np.zeros((8, LANES), np.float32)
    tab[0] = np.where(d < ROPE_ROT, inv_freq[d % half], 0.0)
    tab[1] = np.where(d < half, -1.0, 0.0)
    tab[2] = np.where((d >= half) & (d < ROPE_ROT), 1.0, 0.0)
    return jnp.asarray(tab)


def _seg_mean_matrix(width, seg):
    g = (np.arange(width)[:, None] // seg) == (np.arange(width)[None, :] // seg)
    return jnp.asarray(g.astype(np.float32) / seg)


def _rope(x, cos, sina, sinb):
    W = x.shape[1]
    reps = W // LANES
    if reps > 1:
        cos = jnp.concatenate([cos] * reps, axis=1)
        sina = jnp.concatenate([sina] * reps, axis=1)
        sinb = jnp.concatenate([sinb] * reps, axis=1)
    half = ROPE_ROT // 2
    up = pltpu.roll(x, W - half, axis=1)
    down = pltpu.roll(x, half, axis=1)
    return x * cos + up * sina + down * sinb


def _att_prep_kernel(q_ref, k_ref, v_ref, iq_ref, ikiw_ref, pos_ref, tab_ref, seg_ref,
                     qg_ref, kg_ref, qT_ref, kh_ref, vT_ref, iqT_ref, ik_ref, iwT_ref):
    tab = tab_ref[...]
    ang = pos_ref[...].astype(F32) * tab[0:1, :]
    cos = jnp.cos(ang)
    sin = jnp.sin(ang)
    sina = sin * tab[1:2, :]
    sinb = sin * tab[2:3, :]
    seg = seg_ref[...]

    def head_norm(x, g):
        ms = _mm_hi(x * x, seg)
        return x * lax.rsqrt(ms + NORM_EPS) * g

    q = _rope(head_norm(q_ref[...], qg_ref[...]), cos, sina, sinb) * (ATT_HD ** -0.5)
    k = _rope(head_norm(k_ref[...], kg_ref[...]), cos, sina, sinb)
    v = v_ref[...]
    iq = _rope(iq_ref[...], cos, sina, sinb)
    ik = _rope(ikiw_ref[...], cos, sina, sinb)
    for h in range(ATT_HEADS):
        kh_ref[h] = k[:, h * ATT_HD:(h + 1) * ATT_HD].astype(kh_ref.dtype)
    qT_ref[...] = q.T.astype(qT_ref.dtype)
    vT_ref[...] = v.T.astype(vT_ref.dtype)
    iqT_ref[...] = iq.T.astype(iqT_ref.dtype)
    ik_ref[...] = ik[:, :IDX_HD].astype(ik_ref.dtype)
    idx_scale = (IDX_HEADS ** -0.5) * (IDX_HD ** -0.5)
    iwT_ref[...] = ikiw_ref[...].T[IDX_HD:IDX_HD + 8, :] * idx_scale


def _att_prep(proj, positions, att_q_gain, att_k_gain, tq):
    S = proj.shape[0]
    qg = jnp.tile(att_q_gain.reshape(1, ATT_HD), (1, ATT_HEADS))
    kg = jnp.tile(att_k_gain.reshape(1, ATT_HD), (1, ATT_HEADS))
    col = lambda off, w: (lambda i: (i, off // w))
    const = lambda i: (0, 0)
    return pl.pallas_call(
        _att_prep_kernel,
        grid=(S // tq,),
        in_specs=[
            pl.BlockSpec((tq, ATT_W), col(COL_Q, ATT_W)),
            pl.BlockSpec((tq, ATT_W), col(COL_K, ATT_W)),
            pl.BlockSpec((tq, ATT_W), col(COL_V, ATT_W)),
            pl.BlockSpec((tq, IDX_HEADS * IDX_HD), col(COL_IQ, IDX_HEADS * IDX_HD)),
            pl.BlockSpec((tq, LANES), col(COL_IKIW, LANES)),
            pl.BlockSpec((tq, 1), lambda i: (i, 0)),
            pl.BlockSpec((8, LANES), const),
            pl.BlockSpec((ATT_W, ATT_W), const),
            pl.BlockSpec((1, ATT_W), const),
            pl.BlockSpec((1, ATT_W), const),
        ],
        out_specs=[
            pl.BlockSpec((ATT_W, tq), lambda i: (0, i)),
            pl.BlockSpec((ATT_HEADS, tq, ATT_HD), lambda i: (0, i, 0)),
            pl.BlockSpec((ATT_W, tq), lambda i: (0, i)),
            pl.BlockSpec((IDX_HEADS * IDX_HD, tq), lambda i: (0, i)),
            pl.BlockSpec((tq, IDX_HD), lambda i: (i, 0)),
            pl.BlockSpec((8, tq), lambda i: (0, i)),
        ],
        out_shape=[
            jax.ShapeDtypeStruct((ATT_W, S), MXU_DTYPE),
            jax.ShapeDtypeStruct((ATT_HEADS, S, ATT_HD), MXU_DTYPE),
            jax.ShapeDtypeStruct((ATT_W, S), MXU_DTYPE),
            jax.ShapeDtypeStruct((IDX_HEADS * IDX_HD, S), MXU_DTYPE),
            jax.ShapeDtypeStruct((S, IDX_HD), MXU_DTYPE),
            jax.ShapeDtypeStruct((8, S), F32),
        ],
        compiler_params=_cparams(("parallel",)),
    )(proj, proj, proj, proj, proj, positions.reshape(S, 1), _rope_tables(),
      _seg_mean_matrix(ATT_W, ATT_HD), qg, kg)


KEY_NEG_INF = -2139095041


def _key_to_float(key):
    f = pltpu.bitcast(key ^ ((key >> 31) & 0x7FFFFFFF), F32)
    return jnp.where(key < KEY_NEG_INF, -jnp.inf, f)


def _sel_kernel(iqT_ref, iwT_ref, ik_ref, bias_ref, sc_ref, *, tq, tk, seq, nsel):
    i = pl.program_id(0)
    q0 = i * tq
    nc = (q0 + tq + tk - 1) // tk
    qidx = q0 + lax.broadcasted_iota(I32, (tk, tq), 1)
    krow = lax.broadcasted_iota(I32, (tk, tq), 0)
    iw = iwT_ref[...]

    def score_chunk(c, carry):
        c0 = pl.multiple_of(c * tk, tk)
        ikc = ik_ref[pl.ds(c0, tk), :]
        sc = None
        for h in range(IDX_HEADS):
            logit = jnp.dot(ikc, iqT_ref[h * IDX_HD:(h + 1) * IDX_HD, :],
                            preferred_element_type=F32)
            t = jnp.maximum(logit, 0.0) * iw[h:h + 1, :]
            sc = t if sc is None else sc + t
        sc_ref[pl.ds(c0, tk), :] = jnp.where(c0 + krow <= qidx, sc, -jnp.inf)
        return carry

    lax.fori_loop(0, nc, score_chunk, 0)

    def count(pred):
        def body(c, acc):
            c0 = pl.multiple_of(c * tk, tk)
            m = jnp.where(pred(sc_ref[pl.ds(c0, tk), :], c0 + krow), 1.0, 0.0)
            return acc + jnp.sum(m.reshape(tk // 8, 8, tq), axis=0)
        acc = lax.fori_loop(0, nc, body, jnp.zeros((8, tq), F32))
        return jnp.sum(acc, axis=0, keepdims=True)

    zero = jnp.zeros((1, tq), I32)
    n_nonneg = count(lambda s, k: s >= 0.0)
    prefix = jnp.where(n_nonneg >= nsel, zero, zero + INT_MIN)

    def bit_body(b, prefix):
        cand = prefix | jnp.left_shift(jnp.int32(1), 30 - b)
        cand_f = _key_to_float(cand)
        return jnp.where(count(lambda s, k: s >= cand_f) >= nsel, cand, prefix)

    tau = _key_to_float(lax.fori_loop(0, 31, bit_body, prefix))
    n_gt = count(lambda s, k: s > tau)
    n_ge = count(lambda s, k: s >= tau)
    need = nsel - n_gt

    def is_neg(s):
        return pltpu.bitcast(s, I32) < 0

    def tie_cuts():
        n_pos = count(lambda s, k: (s == tau) & jnp.logical_not(is_neg(s)))
        need_pos = jnp.minimum(need, n_pos)
        need_neg = need - need_pos

        def cut(pred, nd):
            def body(b, pj):
                cand = pj | jnp.left_shift(jnp.int32(1), 14 - b)
                cnt = count(lambda s, k: pred(s) & (k < cand))
                return jnp.where(cnt < nd, cand, pj)
            return jnp.where(nd > 0, lax.fori_loop(0, 15, body, zero), -1)

        return (cut(lambda s: (s == tau) & jnp.logical_not(is_neg(s)), need_pos),
                cut(lambda s: (s == tau) & is_neg(s), need_neg))

    j_pos, j_neg = lax.cond(jnp.max(n_ge) > nsel, tie_cuts, lambda: (zero + seq, zero + seq))

    def write_chunk(c, carry):
        c0 = pl.multiple_of(c * tk, tk)
        s = sc_ref[pl.ds(c0, tk), :]
        k = c0 + krow
        tie = jnp.where(k <= jnp.where(is_neg(s), j_neg, j_pos), 0.0, NEG_BIG)
        b = jnp.where(s > tau, 0.0, jnp.where(s == tau, tie, NEG_BIG))
        bias_ref[pl.ds(c0, tk), :] = jnp.where(k <= qidx, b, NEG_BIG).astype(bias_ref.dtype)
        return carry

    lax.fori_loop(0, nc, write_chunk, 0)

    def fill_chunk(c, carry):
        c0 = pl.multiple_of(c * tk, tk)
        bias_ref[pl.ds(c0, tk), :] = jnp.full((tk, tq), NEG_BIG, bias_ref.dtype)
        return carry

    lax.fori_loop(nc, seq // tk, fill_chunk, 0)


def _dsa_select(iqT, iwT, ik, tq, tk):
    S = ik.shape[0]
    nsel = min(TOPK_MAX, S // 4)
    kern = functools.partial(_sel_kernel, tq=tq, tk=tk, seq=S, nsel=nsel)
    return pl.pallas_call(
        kern,
        grid=(S // tq,),
        in_specs=[
            pl.BlockSpec((IDX_HEADS * IDX_HD, tq), lambda i: (0, i)),
            pl.BlockSpec((8, tq), lambda i: (0, i)),
            pl.BlockSpec((S, IDX_HD), lambda i: (0, 0)),
        ],
        out_specs=pl.BlockSpec((S, tq), lambda i: (0, i)),
        out_shape=jax.ShapeDtypeStruct((S, S), jnp.bfloat16),
        scratch_shapes=[pltpu.VMEM((S, tq), F32)],
        compiler_params=_cparams(("parallel",)),
    )(iqT, iwT, ik)


def _att_kernel(qT_ref, k_ref, vT_ref, bias_ref, o_ref, m_ref, l_ref, acc_ref, *, tq, tk, nk):
    i = pl.program_id(0)
    j = pl.program_id(1)
    last = ((i + 1) * tq - 1) // tk

    @pl.when(j == 0)
    def _():
        m_ref[...] = jnp.full(m_ref.shape, -jnp.inf, F32)
        l_ref[...] = jnp.zeros(l_ref.shape, F32)
        acc_ref[...] = jnp.zeros(acc_ref.shape, F32)

    @pl.when(j <= last)
    def _():
        bias = bias_ref[...].astype(F32)
        for h in range(ATT_HEADS):
            hs = slice(h * ATT_HD, (h + 1) * ATT_HD)
            s = jnp.dot(k_ref[h], qT_ref[hs, :], preferred_element_type=F32) + bias
            m_prev = m_ref[h:h + 1, :]
            m_new = jnp.maximum(m_prev, jnp.max(s, axis=0, keepdims=True))
            alpha = jnp.exp(m_prev - m_new)
            p = jnp.exp(s - m_new)
            l_ref[h:h + 1, :] = alpha * l_ref[h:h + 1, :] + jnp.sum(p, axis=0, keepdims=True)
            acc_ref[hs, :] = alpha * acc_ref[hs, :] + jnp.dot(
                vT_ref[hs, :], p.astype(vT_ref.dtype), preferred_element_type=F32)
            m_ref[h:h + 1, :] = m_new

    @pl.when(j == nk - 1)
    def _():
        out = jnp.concatenate(
            [acc_ref[h * ATT_HD:(h + 1) * ATT_HD, :] / l_ref[h:h + 1, :]
             for h in range(ATT_HEADS)], axis=0)
        o_ref[...] = out.T


def _dsa_attend(qT, kh, vT, bias, tq, tk):
    S = qT.shape[1]
    nk = S // tk
    last = lambda i: ((i + 1) * tq - 1) // tk
    kern = functools.partial(_att_kernel, tq=tq, tk=tk, nk=nk)
    return pl.pallas_call(
        kern,
        grid=(S // tq, nk),
        in_specs=[
            pl.BlockSpec((ATT_W, tq), lambda i, j: (0, i)),
            pl.BlockSpec((ATT_HEADS, tk, ATT_HD), lambda i, j: (0, jnp.minimum(j, last(i)), 0)),
            pl.BlockSpec((ATT_W, tk), lambda i, j: (0, jnp.minimum(j, last(i)))),
            pl.BlockSpec((tk, tq), lambda i, j: (jnp.minimum(j, last(i)), i)),
        ],
        out_specs=pl.BlockSpec((tq, ATT_W), lambda i, j: (i, 0)),
        out_shape=jax.ShapeDtypeStruct((S, ATT_W), F32),
        scratch_shapes=[
            pltpu.VMEM((ATT_HEADS, tq), F32),
            pltpu.VMEM((ATT_HEADS, tq), F32),
            pltpu.VMEM((ATT_W, tq), F32),
        ],
        compiler_params=_cparams(("parallel", "arbitrary")),
    )(qT, kh, vT, bias)


CHUNK = 64


def _shift_rows(x, prev8, first):
    prev_row = jnp.where(first, 0.0, prev8[7:8, :])
    row = lax.broadcasted_iota(I32, x.shape, 0)
    return jnp.where(row == 0, prev_row, pltpu.roll(x, 1, axis=0))


def _softplus(z):
    return jnp.maximum(z, 0.0) + jnp.log1p(jnp.exp(-jnp.abs(z)))


def _rwkv_prep_kernel(r_ref, k_ref, v_ref, lo_ref, rp_ref, kp_ref, vp_ref, lop_ref,
                      mur_ref, muk_ref, muv_ref, mulo_ref, w0_ref, w2_ref, a0_ref, a2_ref,
                      g2_ref, kk_ref, ka_ref, rk_ref, seg_ref, tri_ref, ones_ref, end_ref,
                      rt_ref, kt_ref, bt_ref, at_ref, kh_ref, bh_ref, vh_ref, gam_ref,
                      bonus_ref, g_ref):
    first = pl.program_id(0) == 0

    def mix(x_ref, p_ref, mu_ref):
        x = x_ref[...]
        return x + (_shift_rows(x, p_ref[...], first) - x) * mu_ref[...]

    r = mix(r_ref, rp_ref, mur_ref)
    k = mix(k_ref, kp_ref, muk_ref)
    v = mix(v_ref, vp_ref, muv_ref)
    lo = mix(lo_ref, lop_ref, mulo_ref)
    wd = lo[:, :DECAY_LORA]
    ad = lo[:, DECAY_LORA:DECAY_LORA + AAA_LORA]
    gd = lo[:, DECAY_LORA + AAA_LORA:]
    w_log = -_softplus(-(w0_ref[...] + _mm(jnp.tanh(wd), w2_ref[...]))) - 0.5
    lw = -jnp.exp(w_log)
    a = jax.nn.sigmoid(a0_ref[...] + _mm(ad, a2_ref[...]))
    g_ref[...] = _mm(jax.nn.sigmoid(gd), g2_ref[...])
    seg = seg_ref[...]
    kk = k * kk_ref[...]
    kk = kk / jnp.maximum(jnp.sqrt(_mm_hi(kk * kk, seg)), 1e-12)
    k = k * (1.0 + (a - 1.0) * ka_ref[...])
    bonus_ref[...] = _mm_hi(r * k * rk_ref[...], seg) * v
    avec = -kk
    bvec = kk * a
    cs = _mm_hi(tri_ref[...], lw)
    cs_end = _mm_hi(ones_ref[...], lw)
    e_neg = jnp.exp(-cs)
    e_end = jnp.exp(cs_end - cs)
    outs = ((rt_ref, r * jnp.exp(cs)), (kt_ref, k * e_neg), (bt_ref, bvec * e_neg),
            (at_ref, avec * jnp.exp(cs - lw)), (kh_ref, k * e_end), (bh_ref, bvec * e_end),
            (vh_ref, v), (gam_ref, jnp.exp(_mm_hi(end_ref[...], lw))))
    for ref, val in outs:
        for h in range(RWKV_HEADS):
            ref[h] = val[:, h * RWKV_HD:(h + 1) * RWKV_HD]


def _rwkv_prep(proj, p, tq):
    S = proj.shape[0]
    nch = tq // CHUNK
    col = lambda off, w: (lambda i: (i, off // w))
    prev = lambda off, w: (lambda i: (jnp.maximum(i * (tq // 8) - 1, 0), off // w))
    const = lambda i: (0, 0)
    lw_ = DECAY_LORA + AAA_LORA + GATE_LORA
    mu = p["rwkv_mu"]
    row = lambda t: t.reshape(1, -1)
    t_idx = np.arange(tq)
    same = (t_idx[:, None] // CHUNK) == (t_idx[None, :] // CHUNK)
    tri = jnp.asarray((same & (t_idx[None, :] <= t_idx[:, None])).astype(np.float32))
    ones = jnp.asarray(same.astype(np.float32))
    end = jnp.asarray(((t_idx[None, :] // CHUNK) == np.arange(nch)[:, None]).astype(np.float32))
    seg = _seg_mean_matrix(RWKV_W, RWKV_HD) * RWKV_HD
    hm = jax.ShapeDtypeStruct((RWKV_HEADS, S, RWKV_HD), F32)
    hm_spec = pl.BlockSpec((RWKV_HEADS, tq, RWKV_HD), lambda i: (0, i, 0))
    wide = jax.ShapeDtypeStruct((S, RWKV_W), F32)
    wide_spec = pl.BlockSpec((tq, RWKV_W), lambda i: (i, 0))
    vec = lambda w: pl.BlockSpec((1, w), const)
    return pl.pallas_call(
        _rwkv_prep_kernel,
        grid=(S // tq,),
        in_specs=[
            pl.BlockSpec((tq, RWKV_W), col(COL_R, RWKV_W)),
            pl.BlockSpec((tq, RWKV_W), col(COL_RK, RWKV_W)),
            pl.BlockSpec((tq, RWKV_W), col(COL_RV, RWKV_W)),
            pl.BlockSpec((tq, lw_), col(COL_LORA, lw_)),
            pl.BlockSpec((8, RWKV_W), prev(COL_R, RWKV_W)),
            pl.BlockSpec((8, RWKV_W), prev(COL_RK, RWKV_W)),
            pl.BlockSpec((8, RWKV_W), prev(COL_RV, RWKV_W)),
            pl.BlockSpec((8, lw_), prev(COL_LORA, lw_)),
            vec(RWKV_W), vec(RWKV_W), vec(RWKV_W), vec(lw_),
            vec(RWKV_W), pl.BlockSpec((DECAY_LORA, RWKV_W), const),
            vec(RWKV_W), pl.BlockSpec((AAA_LORA, RWKV_W), const),
            pl.BlockSpec((GATE_LORA, RWKV_W), const),
            vec(RWKV_W), vec(RWKV_W), vec(RWKV_W),
            pl.BlockSpec((RWKV_W, RWKV_W), const),
            pl.BlockSpec((tq, tq), const), pl.BlockSpec((tq, tq), const),
            pl.BlockSpec((nch, tq), const),
        ],
        out_specs=[hm_spec] * 7 + [pl.BlockSpec((RWKV_HEADS, nch, RWKV_HD), lambda i: (0, i, 0)),
                                   wide_spec, wide_spec],
        out_shape=[hm] * 7 + [jax.ShapeDtypeStruct((RWKV_HEADS, S // CHUNK, RWKV_HD), F32),
                              wide, wide],
        compiler_params=_cparams(("parallel",)),
    )(proj, proj, proj, proj, proj, proj, proj, proj,
      row(mu[:RWKV_W]), row(mu[RWKV_W:2 * RWKV_W]), row(mu[2 * RWKV_W:3 * RWKV_W]),
      row(mu[3 * RWKV_W:]), row(p["rwkv_w0"]), p["rwkv_w2"], row(p["rwkv_a0"]), p["rwkv_a2"],
      p["rwkv_g2"], row(p["rwkv_k_k"]), row(p["rwkv_k_a"]), row(p["rwkv_r_k"]), seg, tri, ones, end)


def _bmm(a, b, dims):
    return jnp.einsum(dims, a, b, preferred_element_type=F32, precision=HI)


def _bmm1(a, b, dims):
    return jnp.einsum(dims, a.astype(MXU_DTYPE), b.astype(MXU_DTYPE), preferred_element_type=F32)


def _rwkv_chunk_kernel(rt_ref, kt_ref, bt_ref, at_ref, kh_ref, bh_ref, v_ref, gam_ref,
                       p_ref, q_ref, rw_ref, y0_ref, *, nch):
    L, N = CHUNK, RWKV_HD
    ri = lax.broadcasted_iota(I32, (nch, L, L), 1)
    ci = lax.broadcasted_iota(I32, (nch, L, L), 2)
    eye_n = (lax.broadcasted_iota(I32, (nch, N, N), 1)
             == lax.broadcasted_iota(I32, (nch, N, N), 2)).astype(F32)
    for h in range(RWKV_HEADS):
        ld = lambda ref: ref[h].reshape(nch, L, N)
        rt, kt, bt, at, kh, bh, v = (ld(x) for x in (rt_ref, kt_ref, bt_ref, at_ref, kh_ref,
                                                      bh_ref, v_ref))
        gam = gam_ref[h].reshape(nch, 1, N)
        mm = _bmm1
        a_ab = jnp.where(ci < ri, mm(at, bt, "cld,cmd->clm"), 0.0)
        a_ak = jnp.where(ci < ri, mm(at, kt, "cld,cmd->clm"), 0.0)
        m_rk = jnp.where(ci <= ri, mm(rt, kt, "cld,cmd->clm"), 0.0)
        m_rb = jnp.where(ci <= ri, mm(rt, bt, "cld,cmd->clm"), 0.0)
        rhs = jnp.concatenate([at, mm(a_ak, v, "clm,cmd->cld")], axis=2)
        pw = a_ab
        step = 1
        while True:
            rhs = rhs + mm(pw, rhs, "clm,cmd->cld")
            step *= 2
            if step >= L:
                break
            pw = mm(pw, pw, "clm,cmn->cln")
        w, u0 = rhs[:, :, :N], rhs[:, :, N:]
        p_ref[:, h] = eye_n * gam + mm(bh, w, "cld,cle->cde")
        q_ref[:, h] = mm(kh, v, "cld,cle->cde") + mm(bh, u0, "cld,cle->cde")
        rw_ref[h] = (rt + mm(m_rb, w, "clm,cmd->cld")).reshape(nch * L, N)
        y0_ref[h] = (mm(m_rk, v, "clm,cmd->cld") + mm(m_rb, u0, "clm,cmd->cld")).reshape(nch * L, N)


def _rwkv_chunks(rt, kt, bt, at, kh, bh, vh, gam, tt):
    S = rt.shape[1]
    nch = tt // CHUNK
    hm_spec = pl.BlockSpec((RWKV_HEADS, tt, RWKV_HD), lambda i: (0, i, 0))
    hm = jax.ShapeDtypeStruct((RWKV_HEADS, S, RWKV_HD), F32)
    sq_spec = pl.BlockSpec((nch, RWKV_HEADS, RWKV_HD, RWKV_HD), lambda i: (i, 0, 0, 0))
    sq = jax.ShapeDtypeStruct((S // CHUNK, RWKV_HEADS, RWKV_HD, RWKV_HD), F32)
    return pl.pallas_call(
        functools.partial(_rwkv_chunk_kernel, nch=nch),
        grid=(S // tt,),
        in_specs=[hm_spec] * 7 + [pl.BlockSpec((RWKV_HEADS, nch, RWKV_HD), lambda i: (0, i, 0))],
        out_specs=[sq_spec, sq_spec, hm_spec, hm_spec],
        out_shape=[sq, sq, hm, hm],
        compiler_params=_cparams(("parallel",)),
    )(rt, kt, bt, at, kh, bh, vh, gam)


def _rwkv_scan_kernel(p_ref, q_ref, rw_ref, y0_ref, bonus_ref, g_ref, seg_ref, lnw_ref, lnb_ref,
                      o_ref, st_ref, *, nch):
    @pl.when(pl.program_id(0) == 0)
    def _():
        st_ref[...] = jnp.zeros(st_ref.shape, F32)

    L = CHUNK
    st = st_ref[...]
    ys = []
    for c in range(nch):
        rw = rw_ref[:, c * L:(c + 1) * L, :]
        ys.append(_bmm(rw, st, "hld,hde->hle") + y0_ref[:, c * L:(c + 1) * L, :])
        st = _bmm(p_ref[c], st, "hjk,hki->hji") + q_ref[c]
    st_ref[...] = st
    y = jnp.concatenate(ys, axis=1)
    y = jnp.concatenate([y[h] for h in range(RWKV_HEADS)], axis=1)
    seg = seg_ref[...]
    mean = _mm_hi(y, seg)
    d = y - mean
    var = _mm_hi(d * d, seg)
    y = d * lax.rsqrt(var + GN_EPS) * lnw_ref[...] + lnb_ref[...]
    o_ref[...] = (y + bonus_ref[...]) * g_ref[...]


def _rwkv_scan(pm, qm, rw, y0, bonus, g, ln_w, ln_b, tt):
    S = rw.shape[1]
    nch = tt // CHUNK
    hm_spec = pl.BlockSpec((RWKV_HEADS, tt, RWKV_HD), lambda i: (0, i, 0))
    sq_spec = pl.BlockSpec((nch, RWKV_HEADS, RWKV_HD, RWKV_HD), lambda i: (i, 0, 0, 0))
    wide_spec = pl.BlockSpec((tt, RWKV_W), lambda i: (i, 0))
    const = lambda i: (0, 0)
    return pl.pallas_call(
        functools.partial(_rwkv_scan_kernel, nch=nch),
        grid=(S // tt,),
        in_specs=[sq_spec, sq_spec, hm_spec, hm_spec, wide_spec, wide_spec,
                  pl.BlockSpec((RWKV_W, RWKV_W), const), pl.BlockSpec((1, RWKV_W), const),
                  pl.BlockSpec((1, RWKV_W), const)],
        out_specs=wide_spec,
        out_shape=jax.ShapeDtypeStruct((S, RWKV_W), F32),
        scratch_shapes=[pltpu.VMEM((RWKV_HEADS, RWKV_HD, RWKV_HD), F32)],
        compiler_params=_cparams(("arbitrary",)),
    )(pm, qm, rw, y0, bonus, g, _seg_mean_matrix(RWKV_W, RWKV_HD), ln_w.reshape(1, -1),
      ln_b.reshape(1, -1))


def _rwkv_time_mix(proj, p):
    outs = _rwkv_prep(proj, p, 512)
    rt, kt, bt, at, kh, bh, vh, gam, bonus, g = outs
    pm, qm, rw, y0 = _rwkv_chunks(rt, kt, bt, at, kh, bh, vh, gam, 512)
    return _rwkv_scan(pm, qm, rw, y0, bonus, g, p["rwkv_ln_w"], p["rwkv_ln_b"], 512)


def _lane_rmsnorm(x, gain):
    return x * lax.rsqrt(jnp.mean(x * x, axis=-1, keepdims=True) + NORM_EPS) * gain


def _mem_kv_kernel(mem_ref, g_ref, w_ref, kg_ref, kmT_ref, vm_ref):
    m = _lane_rmsnorm(mem_ref[...], g_ref[...])
    kv = _mm(m, w_ref[...])
    km = jnp.concatenate(
        [_lane_rmsnorm(kv[:, h * MEM_HD:(h + 1) * MEM_HD], kg_ref[...]) for h in range(MEM_HEADS)],
        axis=1)
    kmT_ref[...] = km.T.astype(kmT_ref.dtype)
    vm_ref[...] = kv[:, MEM_W:].astype(vm_ref.dtype)


def _mem_kv(mem, gain, w_kv, k_gain):
    M = mem.shape[0]
    return pl.pallas_call(
        _mem_kv_kernel,
        out_shape=[jax.ShapeDtypeStruct((MEM_W, M), MXU_DTYPE),
                   jax.ShapeDtypeStruct((M, MEM_W), MXU_DTYPE)],
        compiler_params=pltpu.CompilerParams(vmem_limit_bytes=VMEM_LIMIT),
    )(mem, gain.reshape(1, -1), w_kv.astype(MXU_DTYPE), k_gain.reshape(1, -1))


def _merge_kernel(x_ref, yatt_ref, yrwkv_ref, memq_ref, gate_ref, kmT_ref, vm_ref, qg_ref,
                  gb_ref, wup_ref, wout_ref, o_ref):
    q = memq_ref[...]
    heads = []
    for h in range(MEM_HEADS):
        sl = slice(h * MEM_HD, (h + 1) * MEM_HD)
        qn = _lane_rmsnorm(q[:, sl], qg_ref[...])
        s = _mm(qn, kmT_ref[sl, :]) * (MEM_HD ** -0.5)
        s = s - jnp.max(s, axis=-1, keepdims=True)
        e = jnp.exp(s)
        p = e / jnp.sum(e, axis=-1, keepdims=True)
        heads.append(_mm(p, vm_ref[:, sl]))
    ymem = jnp.concatenate(heads, axis=1)
    merged = None
    for c, y in enumerate((yatt_ref[...], yrwkv_ref[...], ymem)):
        up = _mm(y, wup_ref[c])
        gate = jax.nn.sigmoid(gate_ref[:, c * D_MODEL:(c + 1) * D_MODEL] + gb_ref[c:c + 1, :])
        merged = gate * up if merged is None else merged + gate * up
    o_ref[...] = x_ref[...] + _mm(merged, wout_ref[...])


def _merge(x, y_att, y_rwkv, proj, kmT, vm, mem_q_gain, gate_bias, w_up, w_out, tq):
    S = x.shape[0]
    M = vm.shape[0]
    gw = N_BRANCH * D_MODEL
    const2 = lambda i: (0, 0)
    return pl.pallas_call(
        _merge_kernel,
        grid=(S // tq,),
        in_specs=[
            pl.BlockSpec((tq, D_MODEL), lambda i: (i, 0)),
            pl.BlockSpec((tq, ATT_W), lambda i: (i, 0)),
            pl.BlockSpec((tq, RWKV_W), lambda i: (i, 0)),
            pl.BlockSpec((tq, MEM_W), lambda i: (i, COL_MEMQ // MEM_W)),
            pl.BlockSpec((tq, gw), lambda i: (i, COL_GATE // gw)),
            pl.BlockSpec((MEM_W, M), const2),
            pl.BlockSpec((M, MEM_W), const2),
            pl.BlockSpec((1, MEM_HD), const2),
            pl.BlockSpec((N_BRANCH, D_MODEL), const2),
            pl.BlockSpec((N_BRANCH, ATT_W, D_MODEL), lambda i: (0, 0, 0)),
            pl.BlockSpec((D_MODEL, D_MODEL), const2),
        ],
        out_specs=pl.BlockSpec((tq, D_MODEL), lambda i: (i, 0)),
        out_shape=jax.ShapeDtypeStruct((S, D_MODEL), F32),
        compiler_params=_cparams(("parallel",)),
    )(x, y_att, y_rwkv, proj, proj, kmT, vm, mem_q_gain.reshape(1, -1), gate_bias,
      w_up.astype(MXU_DTYPE), w_out.astype(MXU_DTYPE))


def _pop_max(cur, rows):
    m = jnp.max(cur, axis=0, keepdims=True)
    first = jnp.min(jnp.where(cur == m, rows, cur.shape[0]), axis=0, keepdims=True)
    return m, jnp.where(rows == first, -jnp.inf, cur)


def _peer_route_kernel(h_ref, g_ref, wq_ref, k1_ref, k2_ref,
                       xn_ref, s1_ref, s2_ref, p1_ref, p2_ref, tau_ref):
    tq = h_ref.shape[0]
    xn = _lane_rmsnorm(h_ref[...], g_ref[...]).astype(xn_ref.dtype)
    xn_ref[...] = xn
    qp = jnp.dot(xn, wq_ref[...], preferred_element_type=F32)
    nt = (((1,), (1,)), ((), ()))
    half = PEER_QDIM // 2
    rows = lax.broadcasted_iota(I32, (PEER_KEYS, tq), 0)
    pairs = [(a, b) for a in range(PEER_TOPK) for b in range(PEER_TOPK)
             if (a + 1) * (b + 1) <= PEER_TOPK]
    npad = -len(pairs) % 8
    crow = lax.broadcasted_iota(I32, (len(pairs) + npad, tq), 0)
    taus = []
    for h in range(PEER_HEADS):
        q1 = qp[:, h * PEER_QDIM:h * PEER_QDIM + half].astype(MXU_DTYPE)
        q2 = qp[:, h * PEER_QDIM + half:(h + 1) * PEER_QDIM].astype(MXU_DTYPE)
        s1 = lax.dot_general(k1_ref[...], q1, nt, preferred_element_type=F32)
        s2 = lax.dot_general(k2_ref[...], q2, nt, preferred_element_type=F32)
        tops = []
        for s in (s1, s2):
            cur, vals = s, []
            for _ in range(PEER_TOPK):
                m, cur = _pop_max(cur, rows)
                vals.append(m)
            tops.append(vals)
        v1, v2 = tops
        cand = jnp.concatenate([v1[a] + v2[b] for a, b in pairs]
                               + [jnp.full((npad, tq), -jnp.inf, F32)], axis=0)
        top = v1[0] + v2[0]
        z = jnp.zeros((1, tq), F32)
        for _ in range(PEER_TOPK):
            m, cand = _pop_max(cand, crow)
            z = z + jnp.exp(m - top)
        taus.append(m)
        s1_ref[h] = s1
        s2_ref[h] = s2
        p1_ref[h] = jnp.exp(s1 - v1[0]) / z
        p2_ref[h] = jnp.exp(s2 - v2[0])
    tau_ref[...] = jnp.concatenate(taus, axis=0)


def _peer_route(h2, gain, w_q, key1, key2, tq):
    S = h2.shape[0]
    const2 = lambda i: (0, 0)
    half = PEER_QDIM // 2
    kt = jax.ShapeDtypeStruct((PEER_HEADS, PEER_KEYS, S), F32)
    kt_spec = pl.BlockSpec((PEER_HEADS, PEER_KEYS, tq), lambda i: (0, 0, i))
    return pl.pallas_call(
        _peer_route_kernel,
        grid=(S // tq,),
        in_specs=[
            pl.BlockSpec((tq, D_MODEL), lambda i: (i, 0)),
            pl.BlockSpec((1, D_MODEL), const2),
            pl.BlockSpec((D_MODEL, PEER_HEADS * PEER_QDIM), const2),
            pl.BlockSpec((PEER_KEYS, half), const2),
            pl.BlockSpec((PEER_KEYS, half), const2),
        ],
        out_specs=[pl.BlockSpec((tq, D_MODEL), lambda i: (i, 0)), kt_spec, kt_spec, kt_spec,
                   kt_spec, pl.BlockSpec((PEER_HEADS, tq), lambda i: (0, i))],
        out_shape=[jax.ShapeDtypeStruct((S, D_MODEL), MXU_DTYPE), kt, kt, kt, kt,
                   jax.ShapeDtypeStruct((PEER_HEADS, S), F32)],
        compiler_params=_cparams(("parallel",)),
    )(h2, gain.reshape(1, -1), w_q.astype(MXU_DTYPE), key1.astype(MXU_DTYPE),
      key2.astype(MXU_DTYPE))


def _peer_dense_kernel(xn_ref, u_ref, vT_ref, s1_ref, s2_ref, p1_ref, p2_ref, tau_ref, h_ref,
                       o_ref, acc_ref, x_ref, *, te, ne):
    j = pl.program_id(1)

    @pl.when(j == 0)
    def _():
        acc_ref[...] = jnp.zeros(acc_ref.shape, F32)

    nt = (((1,), (1,)), ((), ()))
    hid = lax.dot_general(u_ref[...], xn_ref[...], nt, preferred_element_type=F32)
    act = 0.5 * hid * (1.0 + lax.erf(hid * (2.0 ** -0.5)))
    tau = tau_ref[...]
    for el in range(te // PEER_KEYS):
        e1 = j * (te // PEER_KEYS) + el
        g = None
        for h in range(PEER_HEADS):
            s1r = s1_ref[h, pl.ds(e1, 1), :]
            p1r = p1_ref[h, pl.ds(e1, 1), :]
            keep = (s1r + s2_ref[h]) >= tau[h:h + 1, :]
            t = jnp.where(keep, p2_ref[h], 0.0) * p1r
            g = t if g is None else g + t
        sl = slice(el * PEER_KEYS, (el + 1) * PEER_KEYS)
        x_ref[sl, :] = (g * act[sl, :]).astype(x_ref.dtype)
    acc_ref[...] += jnp.dot(vT_ref[...], x_ref[...], preferred_element_type=F32)

    @pl.when(j == ne - 1)
    def _():
        o_ref[...] = h_ref[...] + acc_ref[...].T


def _peer_dense(h2, xn, u, v, s1, s2, p1, p2, tau, tm, te):
    S = h2.shape[0]
    NE = u.shape[0]
    ne = NE // te
    kt_spec = pl.BlockSpec((PEER_HEADS, PEER_KEYS, tm), lambda i, j: (0, 0, i))
    return pl.pallas_call(
        functools.partial(_peer_dense_kernel, te=te, ne=ne),
        grid=(S // tm, ne),
        in_specs=[
            pl.BlockSpec((tm, D_MODEL), lambda i, j: (i, 0)),
            pl.BlockSpec((te, D_MODEL), lambda i, j: (j, 0)),
            pl.BlockSpec((D_MODEL, te), lambda i, j: (0, j)),
            kt_spec, kt_spec, kt_spec, kt_spec,
            pl.BlockSpec((PEER_HEADS, tm), lambda i, j: (0, i)),
            pl.BlockSpec((tm, D_MODEL), lambda i, j: (i, 0)),
        ],
        out_specs=pl.BlockSpec((tm, D_MODEL), lambda i, j: (i, 0)),
        out_shape=jax.ShapeDtypeStruct((S, D_MODEL), F32),
        scratch_shapes=[pltpu.VMEM((D_MODEL, tm), F32), pltpu.VMEM((te, tm), MXU_DTYPE)],
        compiler_params=_cparams(("parallel", "arbitrary")),
    )(xn, u.astype(MXU_DTYPE), v.T.astype(MXU_DTYPE), s1, s2, p1, p2, tau, h2)


def _peer(h2, gain, w_q, key1, key2, u, v):
    S = h2.shape[0]
    xn, s1, s2, p1, p2, tau = _peer_route(h2, gain, w_q, key1, key2, min(256, S))
    return _peer_dense(h2, xn, u, v, s1, s2, p1, p2, tau, min(512, S), 1024)


def _layer(x, mem, positions, p):
    S = x.shape[0]
    proj = _norm_proj(x, p["mix_norm_gain"], _pack_w_in(p["w_in"]), min(1024, S), 768)
    qT, kh, vT, iqT, ik, iwT = _att_prep(proj, positions, p["att_q_gain"], p["att_k_gain"], 256)
    bias = _dsa_select(iqT, iwT, ik, 256, 512)
    y_att = _dsa_attend(qT, kh, vT, bias, 256, 512)
    y_rwkv = _rwkv_time_mix(proj, p)
    kmT, vm = _mem_kv(mem, p["mem_norm_gain"], p["w_mem_kv"], p["mem_k_gain"])
    h2 = _merge(x, y_att, y_rwkv, proj, kmT, vm, p["mem_q_gain"], p["gate_bias"], p["w_up"],
                p["w_out"], 256)
    return _peer(h2, p["ffn_norm_gain"], p["peer_w_q"], p["peer_key1"], p["peer_key2"],
                 p["peer_u"], p["peer_v"])


_PARAM_NAMES = ("mix_norm_gain", "w_in", "gate_bias", "att_q_gain", "att_k_gain", "mem_norm_gain",
                "w_mem_kv", "mem_q_gain", "mem_k_gain", "rwkv_mu", "rwkv_w0", "rwkv_w2", "rwkv_a0",
                "rwkv_a2", "rwkv_g2", "rwkv_k_k", "rwkv_k_a", "rwkv_r_k", "rwkv_ln_w", "rwkv_ln_b",
                "w_up", "w_out", "ffn_norm_gain", "peer_w_q", "peer_key1", "peer_key2", "peer_u",
                "peer_v")


def kernel(x, mem, positions, mix_norm_gain, w_in, gate_bias, att_q_gain, att_k_gain,
           mem_norm_gain, w_mem_kv, mem_q_gain, mem_k_gain, rwkv_mu, rwkv_w0, rwkv_w2, rwkv_a0,
           rwkv_a2, rwkv_g2, rwkv_k_k, rwkv_k_a, rwkv_r_k, rwkv_ln_w, rwkv_ln_b, w_up, w_out,
           ffn_norm_gain, peer_w_q, peer_key1, peer_key2, peer_u, peer_v):
    params = (mix_norm_gain, w_in, gate_bias, att_q_gain, att_k_gain, mem_norm_gain, w_mem_kv,
              mem_q_gain, mem_k_gain, rwkv_mu, rwkv_w0, rwkv_w2, rwkv_a0, rwkv_a2, rwkv_g2,
              rwkv_k_k, rwkv_k_a, rwkv_r_k, rwkv_ln_w, rwkv_ln_b, w_up, w_out, ffn_norm_gain,
              peer_w_q, peer_key1, peer_key2, peer_u, peer_v)
    assert x.shape[0] == 1 and all(t.shape[0] == 1 for t in params)
    p = {name: t[0] for name, t in zip(_PARAM_NAMES, params)}
    return _layer(x[0], mem[0], positions[0], p)[None]
```

```python
import functools

import numpy as np
import jax
import jax.numpy as jnp
from jax import lax
from jax.experimental import pallas as pl
from jax.experimental.pallas import tpu as pltpu

F32 = jnp.float32
I32 = jnp.int32
MXU_DTYPE = jnp.bfloat16
HI = lax.Precision.HIGHEST

D_MODEL = 1024
ATT_HEADS, ATT_HD = 8, 64
ATT_W = ATT_HEADS * ATT_HD
IDX_HEADS, IDX_HD = 4, 64
TOPK_MAX = 256
RWKV_HEADS, RWKV_HD = 8, 64
RWKV_W = RWKV_HEADS * RWKV_HD
DECAY_LORA, AAA_LORA, GATE_LORA = 64, 64, 128
GN_EPS = 64e-5
MEM_HEADS, MEM_HD = 4, 128
MEM_W = MEM_HEADS * MEM_HD
N_BRANCH = 3
ROPE_THETA = 500000.0
ROPE_ROT = ATT_HD // 4
NORM_EPS = 1e-6
PEER_KEYS = 128
PEER_HEADS = 8
PEER_QDIM = 256
PEER_TOPK = 16

LANES = 128
INT_MIN = -(2 ** 31)
NEG_BIG = -1e30
VMEM_LIMIT = 56 * 1024 * 1024

COL_GATE = 0
COL_Q = 3072
COL_K = 3584
COL_V = 4096
COL_R = 4608
COL_RK = 5120
COL_RV = 5632
COL_MEMQ = 6144
COL_IQ = 6656
COL_LORA = 6912
COL_IKIW = 7168
PROJ_COLS = 7680


def _cparams(sem):
    return pltpu.CompilerParams(dimension_semantics=sem, vmem_limit_bytes=VMEM_LIMIT)


def _mm(a, b):
    return jnp.dot(a.astype(MXU_DTYPE), b.astype(MXU_DTYPE), preferred_element_type=F32)


def _mm_hi(a, b):
    return jnp.dot(a.astype(F32), b.astype(F32), preferred_element_type=F32, precision=HI)


def _normproj_kernel(x_ref, g_ref, w_ref, o_ref, xn_ref):
    @pl.when(pl.program_id(1) == 0)
    def _():
        x = x_ref[...]
        ms = jnp.mean(x * x, axis=-1, keepdims=True)
        xn_ref[...] = (x * lax.rsqrt(ms + NORM_EPS) * g_ref[...]).astype(xn_ref.dtype)

    o_ref[...] = jnp.dot(xn_ref[...], w_ref[...], preferred_element_type=F32)


def _norm_proj(x, gain, w, tm, tn):
    S, D = x.shape
    N = w.shape[1]
    return pl.pallas_call(
        _normproj_kernel,
        grid=(S // tm, N // tn),
        in_specs=[
            pl.BlockSpec((tm, D), lambda i, j: (i, 0)),
            pl.BlockSpec((1, D), lambda i, j: (0, 0)),
            pl.BlockSpec((D, tn), lambda i, j: (0, j)),
        ],
        out_specs=pl.BlockSpec((tm, tn), lambda i, j: (i, j)),
        out_shape=jax.ShapeDtypeStruct((S, N), F32),
        scratch_shapes=[pltpu.VMEM((tm, D), w.dtype)],
        compiler_params=_cparams(("parallel", "arbitrary")),
    )(x, gain.reshape(1, D), w)


def _pack_w_in(w_in):
    D = w_in.shape[0]
    o = 0
    parts = {}
    for name, width in (("q", ATT_W), ("k", ATT_W), ("v", ATT_W), ("iq", IDX_HEADS * IDX_HD),
                        ("ik", IDX_HD), ("iw", IDX_HEADS), ("r", RWKV_W), ("rk", RWKV_W),
                        ("rv", RWKV_W), ("lora", DECAY_LORA + AAA_LORA + GATE_LORA),
                        ("memq", MEM_W), ("gate", N_BRANCH * D_MODEL)):
        parts[name] = w_in[:, o:o + width]
        o += width
    ikiw = jnp.concatenate([parts["ik"], parts["iw"]], axis=1)
    ikiw = jnp.pad(ikiw, ((0, 0), (0, LANES - ikiw.shape[1])))
    packed = jnp.concatenate(
        [parts["gate"], parts["q"], parts["k"], parts["v"], parts["r"], parts["rk"], parts["rv"],
         parts["memq"], parts["iq"], parts["lora"], ikiw], axis=1)
    packed = jnp.pad(packed, ((0, 0), (0, PROJ_COLS - packed.shape[1])))
    return packed.astype(MXU_DTYPE)


def _rope_tables():
    d = np.arange(LANES) % ATT_HD
    inv_freq = 1.0 / (ROPE_THETA ** (np.arange(0, ROPE_ROT, 2, dtype=np.float32) / ROPE_ROT))
    half = ROPE_ROT // 2
    tab = np.zeros((8, LANES), np.float32)
    tab[0] = np.where(d < ROPE_ROT, inv_freq[d % half], 0.0)
    tab[1] = np.where(d < half, -1.0, 0.0)
    tab[2] = np.where((d >= half) & (d < ROPE_ROT), 1.0, 0.0)
    return jnp.asarray(tab)


def _seg_mean_matrix(width, seg):
    g = (np.arange(width)[:, None] // seg) == (np.arange(width)[None, :] // seg)
    return jnp.asarray(g.astype(np.float32) / seg)


def _rope(x, cos, sina, sinb):
    W = x.shape[1]
    reps = W // LANES
    if reps > 1:
        cos = jnp.concatenate([cos] * reps, axis=1)
        sina = jnp.concatenate([sina] * reps, axis=1)
        sinb = jnp.concatenate([sinb] * reps, axis=1)
    half = ROPE_ROT // 2
    up = pltpu.roll(x, W - half, axis=1)
    down = pltpu.roll(x, half, axis=1)
    return x * cos + up * sina + down * sinb


def _att_prep_kernel(q_ref, k_ref, v_ref, iq_ref, ikiw_ref, pos_ref, tab_ref, seg_ref,
                     qg_ref, kg_ref, qT_ref, kh_ref, vT_ref, iqT_ref, ik_ref, iwT_ref):
    tab = tab_ref[...]
    ang = pos_ref[...].astype(F32) * tab[0:1, :]
    cos = jnp.cos(ang)
    sin = jnp.sin(ang)
    sina = sin * tab[1:2, :]
    sinb = sin * tab[2:3, :]
    seg = seg_ref[...]

    def head_norm(x, g):
        ms = _mm_hi(x * x, seg)
        return x * lax.rsqrt(ms + NORM_EPS) * g

    q = _rope(head_norm(q_ref[...], qg_ref[...]), cos, sina, sinb) * (ATT_HD ** -0.5)
    k = _rope(head_norm(k_ref[...], kg_ref[...]), cos, sina, sinb)
    v = v_ref[...]
    iq = _rope(iq_ref[...], cos, sina, sinb)
    ik = _rope(ikiw_ref[...], cos, sina, sinb)
    for h in range(ATT_HEADS):
        kh_ref[h] = k[:, h * ATT_HD:(h + 1) * ATT_HD].astype(kh_ref.dtype)
    qT_ref[...] = q.T.astype(qT_ref.dtype)
    vT_ref[...] = v.T.astype(vT_ref.dtype)
    iqT_ref[...] = iq.T.astype(iqT_ref.dtype)
    ik_ref[...] = ik[:, :IDX_HD].astype(ik_ref.dtype)
    idx_scale = (IDX_HEADS ** -0.5) * (IDX_HD ** -0.5)
    iwT_ref[...] = ikiw_ref[...].T[IDX_HD:IDX_HD + 8, :] * idx_scale


def _att_prep(proj, positions, att_q_gain, att_k_gain, tq):
    S = proj.shape[0]
    qg = jnp.tile(att_q_gain.reshape(1, ATT_HD), (1, ATT_HEADS))
    kg = jnp.tile(att_k_gain.reshape(1, ATT_HD), (1, ATT_HEADS))
    col = lambda off, w: (lambda i: (i, off // w))
    const = lambda i: (0, 0)
    return pl.pallas_call(
        _att_prep_kernel,
        grid=(S // tq,),
        in_specs=[
            pl.BlockSpec((tq, ATT_W), col(COL_Q, ATT_W)),
            pl.BlockSpec((tq, ATT_W), col(COL_K, ATT_W)),
            pl.BlockSpec((tq, ATT_W), col(COL_V, ATT_W)),
            pl.BlockSpec((tq, IDX_HEADS * IDX_HD), col(COL_IQ, IDX_HEADS * IDX_HD)),
            pl.BlockSpec((tq, LANES), col(COL_IKIW, LANES)),
            pl.BlockSpec((tq, 1), lambda i: (i, 0)),
            pl.BlockSpec((8, LANES), const),
            pl.BlockSpec((ATT_W, ATT_W), const),
            pl.BlockSpec((1, ATT_W), const),
            pl.BlockSpec((1, ATT_W), const),
        ],
        out_specs=[
            pl.BlockSpec((ATT_W, tq), lambda i: (0, i)),
            pl.BlockSpec((ATT_HEADS, tq, ATT_HD), lambda i: (0, i, 0)),
            pl.BlockSpec((ATT_W, tq), lambda i: (0, i)),
            pl.BlockSpec((IDX_HEADS * IDX_HD, tq), lambda i: (0, i)),
            pl.BlockSpec((tq, IDX_HD), lambda i: (i, 0)),
            pl.BlockSpec((8, tq), lambda i: (0, i)),
        ],
        out_shape=[
            jax.ShapeDtypeStruct((ATT_W, S), MXU_DTYPE),
            jax.ShapeDtypeStruct((ATT_HEADS, S, ATT_HD), MXU_DTYPE),
            jax.ShapeDtypeStruct((ATT_W, S), MXU_DTYPE),
            jax.ShapeDtypeStruct((IDX_HEADS * IDX_HD, S), MXU_DTYPE),
            jax.ShapeDtypeStruct((S, IDX_HD), MXU_DTYPE),
            jax.ShapeDtypeStruct((8, S), F32),
        ],
        compiler_params=_cparams(("parallel",)),
    )(proj, proj, proj, proj, proj, positions.reshape(S, 1), _rope_tables(),
      _seg_mean_matrix(ATT_W, ATT_HD), qg, kg)


KEY_NEG_INF = -2139095041


def _key_to_float(key):
    f = pltpu.bitcast(key ^ ((key >> 31) & 0x7FFFFFFF), F32)
    return jnp.where(key < KEY_NEG_INF, -jnp.inf, f)


SUB_ACC = 64


def _sel_kernel(iqT_ref, iwT_ref, ik_ref, tri_ref, bias_ref, sc_ref, *, tq, tk, seq, nsel):
    i = pl.program_id(0)
    q0 = i * tq
    nc = (q0 + tq + tk - 1) // tk
    qidx = q0 + lax.broadcasted_iota(I32, (tk, tq), 1)
    krow = lax.broadcasted_iota(I32, (tk, tq), 0)
    iw = iwT_ref[...]

    def score_chunk(c, carry):
        c0 = pl.multiple_of(c * tk, tk)
        ikc = ik_ref[pl.ds(c0, tk), :]
        sc = None
        for h in range(IDX_HEADS):
            logit = jnp.dot(ikc, iqT_ref[h * IDX_HD:(h + 1) * IDX_HD, :],
                            preferred_element_type=F32)
            t = jnp.maximum(logit, 0.0) * iw[h:h + 1, :]
            sc = t if sc is None else sc + t
        sc_ref[pl.ds(c0, tk), :] = jnp.where(c0 + krow <= qidx, sc, -jnp.inf)
        return carry

    lax.fori_loop(0, nc, score_chunk, 0)

    def count(pred, n_out=1):
        def body(c, accs):
            c0 = pl.multiple_of(c * tk, tk)
            ms = pred(sc_ref[pl.ds(c0, tk), :])
            return tuple(a + jnp.sum(m.reshape(tk // SUB_ACC, SUB_ACC, tq), axis=0)
                         for a, m in zip(accs, ms))
        accs = lax.fori_loop(0, nc, body, (jnp.zeros((SUB_ACC, tq), F32),) * n_out)
        return [_col_reduce(a, jnp.add) for a in accs]

    def count_ge(cand):
        return count(lambda s: (jnp.where(s >= cand, 1.0, 0.0),))[0]

    zero = jnp.zeros((1, tq), I32)
    prefix = jnp.where(count_ge(0.0) >= nsel, zero, zero + INT_MIN)

    def bit_body(b, prefix):
        cand = prefix | jnp.left_shift(jnp.int32(1), 30 - b)
        return jnp.where(count_ge(_key_to_float(cand)) >= nsel, cand, prefix)

    tau = _key_to_float(lax.fori_loop(0, 31, bit_body, prefix))

    def is_neg(s):
        return pltpu.bitcast(s, I32) < 0

    def tie_classes(s):
        eq = s == tau
        neg = is_neg(s)
        return (jnp.where(eq, jnp.where(neg, 0.0, 1.0), 0.0),
                jnp.where(eq, jnp.where(neg, 1.0, 0.0), 0.0))

    n_gt, n_eq_pos, n_eq_neg = count(
        lambda s: (jnp.where(s > tau, 1.0, 0.0),) + tie_classes(s), n_out=3)
    need = nsel - n_gt
    mixed = jnp.max(jnp.minimum(n_eq_pos, n_eq_neg)) > 0
    tri = tri_ref[...]

    def write_pass(both_signs):
        def body(c, carry):
            c0 = pl.multiple_of(c * tk, tk)
            s = sc_ref[pl.ds(c0, tk), :]
            if both_signs:
                ind_pos, ind_neg = tie_classes(s)
                cum_pos = carry[0] + _mm(tri, ind_pos)
                cum_neg = carry[1] + _mm(tri, ind_neg)
                rank = jnp.where(is_neg(s), n_eq_pos + cum_neg, cum_pos)
                carry = (cum_pos[tk - 1:tk, :], cum_neg[tk - 1:tk, :])
            else:
                rank = carry[0] + _mm(tri, jnp.where(s == tau, 1.0, 0.0))
                carry = (rank[tk - 1:tk, :], carry[1])
            tie = jnp.where(rank <= need, 0.0, NEG_BIG)
            b = jnp.where(s > tau, 0.0, jnp.where(s == tau, tie, NEG_BIG))
            b = jnp.where(c0 + krow <= qidx, b, NEG_BIG)
            bias_ref[pl.ds(c0, tk), :] = b.astype(bias_ref.dtype)
            return carry
        zf = jnp.zeros((1, tq), F32)
        lax.fori_loop(0, nc, body, (zf, zf))

    lax.cond(mixed, lambda: write_pass(True), lambda: write_pass(False))

    def fill_chunk(c, carry):
        c0 = pl.multiple_of(c * tk, tk)
        bias_ref[pl.ds(c0, tk), :] = jnp.full((tk, tq), NEG_BIG, bias_ref.dtype)
        return carry

    lax.fori_loop(nc, seq // tk, fill_chunk, 0)


def _dsa_select(iqT, iwT, ik, tq, tk):
    S = ik.shape[0]
    nsel = min(TOPK_MAX, S // 4)
    kern = functools.partial(_sel_kernel, tq=tq, tk=tk, seq=S, nsel=nsel)
    return pl.pallas_call(
        kern,
        grid=(S // tq,),
        in_specs=[
            pl.BlockSpec((IDX_HEADS * IDX_HD, tq), lambda i: (0, i)),
            pl.BlockSpec((8, tq), lambda i: (0, i)),
            pl.BlockSpec((S, IDX_HD), lambda i: (0, 0)),
            pl.BlockSpec((tk, tk), lambda i: (0, 0)),
        ],
        out_specs=pl.BlockSpec((S, tq), lambda i: (0, i)),
        out_shape=jax.ShapeDtypeStruct((S, S), jnp.bfloat16),
        scratch_shapes=[pltpu.VMEM((S, tq), F32)],
        compiler_params=_cparams(("parallel",)),
    )(iqT, iwT, ik, jnp.tril(jnp.ones((tk, tk), MXU_DTYPE)))


def _col_reduce(x, op):
    rows = x.shape[0]
    while rows > 8:
        rows //= 2
        x = op(x[:rows], x[rows:])
    red = jnp.max if op is jnp.maximum else jnp.sum
    return red(x, axis=0, keepdims=True)


def _att_kernel(qT_ref, k_ref, vT_ref, bias_ref, o_ref, m_ref, l_ref, acc_ref, *, tq, tk, nk):
    i = pl.program_id(0)
    j = pl.program_id(1)
    last = ((i + 1) * tq - 1) // tk

    @pl.when(j == 0)
    def _():
        m_ref[...] = jnp.full(m_ref.shape, -jnp.inf, F32)
        l_ref[...] = jnp.zeros(l_ref.shape, F32)
        acc_ref[...] = jnp.zeros(acc_ref.shape, F32)

    @pl.when(j <= last)
    def _():
        bias = bias_ref[...].astype(F32)
        hs = lambda h: slice(h * ATT_HD, (h + 1) * ATT_HD)
        m_all = m_ref[...]
        l_all = l_ref[...]
        m_out, l_out = [], []
        for h in range(ATT_HEADS):
            s = jnp.dot(k_ref[h], qT_ref[hs(h), :], preferred_element_type=F32) + bias
            m_prev = m_all[h:h + 1, :]
            m_new = jnp.maximum(m_prev, _col_reduce(s, jnp.maximum))
            alpha = jnp.exp(m_prev - m_new)
            p = jnp.exp(s - m_new)
            l_out.append(alpha * l_all[h:h + 1, :] + _col_reduce(p, jnp.add))
            m_out.append(m_new)
            acc_ref[hs(h), :] = alpha * acc_ref[hs(h), :] + jnp.dot(
                vT_ref[hs(h), :], p.astype(vT_ref.dtype), preferred_element_type=F32)
        m_ref[...] = jnp.concatenate(m_out, axis=0)
        l_ref[...] = jnp.concatenate(l_out, axis=0)

    @pl.when(j == nk - 1)
    def _():
        out = jnp.concatenate(
            [acc_ref[h * ATT_HD:(h + 1) * ATT_HD, :] / l_ref[h:h + 1, :]
             for h in range(ATT_HEADS)], axis=0)
        o_ref[...] = out.T


def _dsa_attend(qT, kh, vT, bias, tq, tk):
    S = qT.shape[1]
    nk = S // tk
    last = lambda i: ((i + 1) * tq - 1) // tk
    kern = functools.partial(_att_kernel, tq=tq, tk=tk, nk=nk)
    return pl.pallas_call(
        kern,
        grid=(S // tq, nk),
        in_specs=[
            pl.BlockSpec((ATT_W, tq), lambda i, j: (0, i)),
            pl.BlockSpec((ATT_HEADS, tk, ATT_HD), lambda i, j: (0, jnp.minimum(j, last(i)), 0)),
            pl.BlockSpec((ATT_W, tk), lambda i, j: (0, jnp.minimum(j, last(i)))),
            pl.BlockSpec((tk, tq), lambda i, j: (jnp.minimum(j, last(i)), i)),
        ],
        out_specs=pl.BlockSpec((tq, ATT_W), lambda i, j: (i, 0)),
        out_shape=jax.ShapeDtypeStruct((S, ATT_W), F32),
        scratch_shapes=[
            pltpu.VMEM((ATT_HEADS, tq), F32),
            pltpu.VMEM((ATT_HEADS, tq), F32),
            pltpu.VMEM((ATT_W, tq), F32),
        ],
        compiler_params=_cparams(("parallel", "arbitrary")),
    )(qT, kh, vT, bias)


CHUNK = 64


def _shift_rows(x, prev8, first):
    prev_row = jnp.where(first, 0.0, prev8[7:8, :])
    row = lax.broadcasted_iota(I32, x.shape, 0)
    return jnp.where(row == 0, prev_row, pltpu.roll(x, 1, axis=0))


def _softplus(z):
    return jnp.maximum(z, 0.0) + jnp.log1p(jnp.exp(-jnp.abs(z)))


def _rwkv_prep_kernel(r_ref, k_ref, v_ref, lo_ref, rp_ref, kp_ref, vp_ref, lop_ref,
                      mur_ref, muk_ref, muv_ref, mulo_ref, w0_ref, w2_ref, a0_ref, a2_ref,
                      g2_ref, kk_ref, ka_ref, rk_ref, seg_ref, tri_ref, ones_ref, end_ref,
                      rt_ref, kt_ref, bt_ref, at_ref, kh_ref, bh_ref, vh_ref, gam_ref,
                      bonus_ref, g_ref):
    first = pl.program_id(0) == 0

    def mix(x_ref, p_ref, mu_ref):
        x = x_ref[...]
        return x + (_shift_rows(x, p_ref[...], first) - x) * mu_ref[...]

    r = mix(r_ref, rp_ref, mur_ref)
    k = mix(k_ref, kp_ref, muk_ref)
    v = mix(v_ref, vp_ref, muv_ref)
    lo = mix(lo_ref, lop_ref, mulo_ref)
    wd = lo[:, :DECAY_LORA]
    ad = lo[:, DECAY_LORA:DECAY_LORA + AAA_LORA]
    gd = lo[:, DECAY_LORA + AAA_LORA:]
    w_log = -_softplus(-(w0_ref[...] + _mm(jnp.tanh(wd), w2_ref[...]))) - 0.5
    lw = -jnp.exp(w_log)
    a = jax.nn.sigmoid(a0_ref[...] + _mm(ad, a2_ref[...]))
    g_ref[...] = _mm(jax.nn.sigmoid(gd), g2_ref[...])
    seg = seg_ref[...]
    kk = k * kk_ref[...]
    kk = kk / jnp.maximum(jnp.sqrt(_mm_hi(kk * kk, seg)), 1e-12)
    k = k * (1.0 + (a - 1.0) * ka_ref[...])
    bonus_ref[...] = _mm_hi(r * k * rk_ref[...], seg) * v
    avec = -kk
    bvec = kk * a
    cs = _mm_hi(tri_ref[...], lw)
    cs_end = _mm_hi(ones_ref[...], lw)
    e_neg = jnp.exp(-cs)
    e_end = jnp.exp(cs_end - cs)
    outs = ((rt_ref, r * jnp.exp(cs)), (kt_ref, k * e_neg), (bt_ref, bvec * e_neg),
            (at_ref, avec * jnp.exp(cs - lw)), (kh_ref, k * e_end), (bh_ref, bvec * e_end),
            (vh_ref, v), (gam_ref, jnp.exp(_mm_hi(end_ref[...], lw))))
    for ref, val in outs:
        for h in range(RWKV_HEADS):
            ref[h] = val[:, h * RWKV_HD:(h + 1) * RWKV_HD]


def _rwkv_prep(proj, p, tq):
    S = proj.shape[0]
    nch = tq // CHUNK
    col = lambda off, w: (lambda i: (i, off // w))
    prev = lambda off, w: (lambda i: (jnp.maximum(i * (tq // 8) - 1, 0), off // w))
    const = lambda i: (0, 0)
    lw_ = DECAY_LORA + AAA_LORA + GATE_LORA
    mu = p["rwkv_mu"]
    row = lambda t: t.reshape(1, -1)
    t_idx = np.arange(tq)
    same = (t_idx[:, None] // CHUNK) == (t_idx[None, :] // CHUNK)
    tri = jnp.asarray((same & (t_idx[None, :] <= t_idx[:, None])).astype(np.float32))
    ones = jnp.asarray(same.astype(np.float32))
    end = jnp.asarray(((t_idx[None, :] // CHUNK) == np.arange(nch)[:, None]).astype(np.float32))
    seg = _seg_mean_matrix(RWKV_W, RWKV_HD) * RWKV_HD
    hm = jax.ShapeDtypeStruct((RWKV_HEADS, S, RWKV_HD), F32)
    hm_spec = pl.BlockSpec((RWKV_HEADS, tq, RWKV_HD), lambda i: (0, i, 0))
    wide = jax.ShapeDtypeStruct((S, RWKV_W), F32)
    wide_spec = pl.BlockSpec((tq, RWKV_W), lambda i: (i, 0))
    vec = lambda w: pl.BlockSpec((1, w), const)
    return pl.pallas_call(
        _rwkv_prep_kernel,
        grid=(S // tq,),
        in_specs=[
            pl.BlockSpec((tq, RWKV_W), col(COL_R, RWKV_W)),
            pl.BlockSpec((tq, RWKV_W), col(COL_RK, RWKV_W)),
            pl.BlockSpec((tq, RWKV_W), col(COL_RV, RWKV_W)),
            pl.BlockSpec((tq, lw_), col(COL_LORA, lw_)),
            pl.BlockSpec((8, RWKV_W), prev(COL_R, RWKV_W)),
            pl.BlockSpec((8, RWKV_W), prev(COL_RK, RWKV_W)),
            pl.BlockSpec((8, RWKV_W), prev(COL_RV, RWKV_W)),
            pl.BlockSpec((8, lw_), prev(COL_LORA, lw_)),
            vec(RWKV_W), vec(RWKV_W), vec(RWKV_W), vec(lw_),
            vec(RWKV_W), pl.BlockSpec((DECAY_LORA, RWKV_W), const),
            vec(RWKV_W), pl.BlockSpec((AAA_LORA, RWKV_W), const),
            pl.BlockSpec((GATE_LORA, RWKV_W), const),
            vec(RWKV_W), vec(RWKV_W), vec(RWKV_W),
            pl.BlockSpec((RWKV_W, RWKV_W), const),
            pl.BlockSpec((tq, tq), const), pl.BlockSpec((tq, tq), const),
            pl.BlockSpec((nch, tq), const),
        ],
        out_specs=[hm_spec] * 7 + [pl.BlockSpec((RWKV_HEADS, nch, RWKV_HD), lambda i: (0, i, 0)),
                                   wide_spec, wide_spec],
        out_shape=[hm] * 7 + [jax.ShapeDtypeStruct((RWKV_HEADS, S // CHUNK, RWKV_HD), F32),
                              wide, wide],
        compiler_params=_cparams(("parallel",)),
    )(proj, proj, proj, proj, proj, proj, proj, proj,
      row(mu[:RWKV_W]), row(mu[RWKV_W:2 * RWKV_W]), row(mu[2 * RWKV_W:3 * RWKV_W]),
      row(mu[3 * RWKV_W:]), row(p["rwkv_w0"]), p["rwkv_w2"], row(p["rwkv_a0"]), p["rwkv_a2"],
      p["rwkv_g2"], row(p["rwkv_k_k"]), row(p["rwkv_k_a"]), row(p["rwkv_r_k"]), seg, tri, ones, end)


def _bmm(a, b, dims):
    return jnp.einsum(dims, a, b, preferred_element_type=F32, precision=HI)


def _bmm1(a, b, dims):
    return jnp.einsum(dims, a.astype(MXU_DTYPE), b.astype(MXU_DTYPE), preferred_element_type=F32)


def _rwkv_chunk_kernel(rt_ref, kt_ref, bt_ref, at_ref, kh_ref, bh_ref, v_ref, gam_ref,
                       p_ref, q_ref, rw_ref, y0_ref, *, nch):
    L, N = CHUNK, RWKV_HD
    ri = lax.broadcasted_iota(I32, (nch, L, L), 1)
    ci = lax.broadcasted_iota(I32, (nch, L, L), 2)
    eye_n = (lax.broadcasted_iota(I32, (nch, N, N), 1)
             == lax.broadcasted_iota(I32, (nch, N, N), 2)).astype(F32)
    for h in range(RWKV_HEADS):
        ld = lambda ref: ref[h].reshape(nch, L, N)
        rt, kt, bt, at, kh, bh, v = (ld(x) for x in (rt_ref, kt_ref, bt_ref, at_ref, kh_ref,
                                                      bh_ref, v_ref))
        gam = gam_ref[h].reshape(nch, 1, N)
        mm = _bmm1
        a_ab = jnp.where(ci < ri, mm(at, bt, "cld,cmd->clm"), 0.0)
        a_ak = jnp.where(ci < ri, mm(at, kt, "cld,cmd->clm"), 0.0)
        m_rk = jnp.where(ci <= ri, mm(rt, kt, "cld,cmd->clm"), 0.0)
        m_rb = jnp.where(ci <= ri, mm(rt, bt, "cld,cmd->clm"), 0.0)
        rhs = jnp.concatenate([at, mm(a_ak, v, "clm,cmd->cld")], axis=2)
        pw = a_ab
        step = 1
        while True:
            rhs = rhs + mm(pw, rhs, "clm,cmd->cld")
            step *= 2
            if step >= L:
                break
            pw = mm(pw, pw, "clm,cmn->cln")
        w, u0 = rhs[:, :, :N], rhs[:, :, N:]
        p_ref[:, h] = eye_n * gam + mm(bh, w, "cld,cle->cde")
        q_ref[:, h] = mm(kh, v, "cld,cle->cde") + mm(bh, u0, "cld,cle->cde")
        rw_ref[h] = (rt + mm(m_rb, w, "clm,cmd->cld")).reshape(nch * L, N)
        y0_ref[h] = (mm(m_rk, v, "clm,cmd->cld") + mm(m_rb, u0, "clm,cmd->cld")).reshape(nch * L, N)


def _rwkv_chunks(rt, kt, bt, at, kh, bh, vh, gam, tt):
    S = rt.shape[1]
    nch = tt // CHUNK
    hm_spec = pl.BlockSpec((RWKV_HEADS, tt, RWKV_HD), lambda i: (0, i, 0))
    hm = jax.ShapeDtypeStruct((RWKV_HEADS, S, RWKV_HD), F32)
    sq_spec = pl.BlockSpec((nch, RWKV_HEADS, RWKV_HD, RWKV_HD), lambda i: (i, 0, 0, 0))
    sq = jax.ShapeDtypeStruct((S // CHUNK, RWKV_HEADS, RWKV_HD, RWKV_HD), F32)
    return pl.pallas_call(
        functools.partial(_rwkv_chunk_kernel, nch=nch),
        grid=(S // tt,),
        in_specs=[hm_spec] * 7 + [pl.BlockSpec((RWKV_HEADS, nch, RWKV_HD), lambda i: (0, i, 0))],
        out_specs=[sq_spec, sq_spec, hm_spec, hm_spec],
        out_shape=[sq, sq, hm, hm],
        compiler_params=_cparams(("parallel",)),
    )(rt, kt, bt, at, kh, bh, vh, gam)


def _rwkv_scan_kernel(p_ref, q_ref, rw_ref, y0_ref, bonus_ref, g_ref, seg_ref, lnw_ref, lnb_ref,
                      o_ref, st_ref, *, nch):
    @pl.when(pl.program_id(0) == 0)
    def _():
        st_ref[...] = jnp.zeros(st_ref.shape, F32)

    L = CHUNK
    st = st_ref[...]
    ys = []
    for c in range(nch):
        rw = rw_ref[:, c * L:(c + 1) * L, :]
        ys.append(_bmm(rw, st, "hld,hde->hle") + y0_ref[:, c * L:(c + 1) * L, :])
        st = _bmm(p_ref[c], st, "hjk,hki->hji") + q_ref[c]
    st_ref[...] = st
    y = jnp.concatenate(ys, axis=1)
    y = jnp.concatenate([y[h] for h in range(RWKV_HEADS)], axis=1)
    seg = seg_ref[...]
    mean = _mm_hi(y, seg)
    d = y - mean
    var = _mm_hi(d * d, seg)
    y = d * lax.rsqrt(var + GN_EPS) * lnw_ref[...] + lnb_ref[...]
    o_ref[...] = (y + bonus_ref[...]) * g_ref[...]


def _rwkv_scan(pm, qm, rw, y0, bonus, g, ln_w, ln_b, tt):
    S = rw.shape[1]
    nch = tt // CHUNK
    hm_spec = pl.BlockSpec((RWKV_HEADS, tt, RWKV_HD), lambda i: (0, i, 0))
    sq_spec = pl.BlockSpec((nch, RWKV_HEADS, RWKV_HD, RWKV_HD), lambda i: (i, 0, 0, 0))
    wide_spec = pl.BlockSpec((tt, RWKV_W), lambda i: (i, 0))
    const = lambda i: (0, 0)
    return pl.pallas_call(
        functools.partial(_rwkv_scan_kernel, nch=nch),
        grid=(S // tt,),
        in_specs=[sq_spec, sq_spec, hm_spec, hm_spec, wide_spec, wide_spec,
                  pl.BlockSpec((RWKV_W, RWKV_W), const), pl.BlockSpec((1, RWKV_W), const),
                  pl.BlockSpec((1, RWKV_W), const)],
        out_specs=wide_spec,
        out_shape=jax.ShapeDtypeStruct((S, RWKV_W), F32),
        scratch_shapes=[pltpu.VMEM((RWKV_HEADS, RWKV_HD, RWKV_HD), F32)],
        compiler_params=_cparams(("arbitrary",)),
    )(pm, qm, rw, y0, bonus, g, _seg_mean_matrix(RWKV_W, RWKV_HD), ln_w.reshape(1, -1),
      ln_b.reshape(1, -1))


def _rwkv_time_mix(proj, p):
    outs = _rwkv_prep(proj, p, 512)
    rt, kt, bt, at, kh, bh, vh, gam, bonus, g = outs
    pm, qm, rw, y0 = _rwkv_chunks(rt, kt, bt, at, kh, bh, vh, gam, 512)
    return _rwkv_scan(pm, qm, rw, y0, bonus, g, p["rwkv_ln_w"], p["rwkv_ln_b"], 512)


def _lane_rmsnorm(x, gain):
    return x * lax.rsqrt(jnp.mean(x * x, axis=-1, keepdims=True) + NORM_EPS) * gain


def _mem_kv_kernel(mem_ref, g_ref, w_ref, kg_ref, kmT_ref, vm_ref):
    m = _lane_rmsnorm(mem_ref[...], g_ref[...])
    kv = _mm(m, w_ref[...])
    km = jnp.concatenate(
        [_lane_rmsnorm(kv[:, h * MEM_HD:(h + 1) * MEM_HD], kg_ref[...]) for h in range(MEM_HEADS)],
        axis=1)
    kmT_ref[...] = km.T.astype(kmT_ref.dtype)
    vm_ref[...] = kv[:, MEM_W:].astype(vm_ref.dtype)


def _mem_kv(mem, gain, w_kv, k_gain):
    M = mem.shape[0]
    return pl.pallas_call(
        _mem_kv_kernel,
        out_shape=[jax.ShapeDtypeStruct((MEM_W, M), MXU_DTYPE),
                   jax.ShapeDtypeStruct((M, MEM_W), MXU_DTYPE)],
        compiler_params=pltpu.CompilerParams(vmem_limit_bytes=VMEM_LIMIT),
    )(mem, gain.reshape(1, -1), w_kv.astype(MXU_DTYPE), k_gain.reshape(1, -1))


def _merge_kernel(x_ref, yatt_ref, yrwkv_ref, memq_ref, gate_ref, kmT_ref, vm_ref, qg_ref,
                  gb_ref, wup_ref, wout_ref, o_ref):
    q = memq_ref[...]
    heads = []
    for h in range(MEM_HEADS):
        sl = slice(h * MEM_HD, (h + 1) * MEM_HD)
        qn = _lane_rmsnorm(q[:, sl], qg_ref[...])
        s = _mm(qn, kmT_ref[sl, :]) * (MEM_HD ** -0.5)
        s = s - jnp.max(s, axis=-1, keepdims=True)
        e = jnp.exp(s)
        p = e / jnp.sum(e, axis=-1, keepdims=True)
        heads.append(_mm(p, vm_ref[:, sl]))
    ymem = jnp.concatenate(heads, axis=1)
    merged = None
    for c, y in enumerate((yatt_ref[...], yrwkv_ref[...], ymem)):
        up = _mm(y, wup_ref[c])
        gate = jax.nn.sigmoid(gate_ref[:, c * D_MODEL:(c + 1) * D_MODEL] + gb_ref[c:c + 1, :])
        merged = gate * up if merged is None else merged + gate * up
    o_ref[...] = x_ref[...] + _mm(merged, wout_ref[...])


def _merge(x, y_att, y_rwkv, proj, kmT, vm, mem_q_gain, gate_bias, w_up, w_out, tq):
    S = x.shape[0]
    M = vm.shape[0]
    gw = N_BRANCH * D_MODEL
    const2 = lambda i: (0, 0)
    return pl.pallas_call(
        _merge_kernel,
        grid=(S // tq,),
        in_specs=[
            pl.BlockSpec((tq, D_MODEL), lambda i: (i, 0)),
            pl.BlockSpec((tq, ATT_W), lambda i: (i, 0)),
            pl.BlockSpec((tq, RWKV_W), lambda i: (i, 0)),
            pl.BlockSpec((tq, MEM_W), lambda i: (i, COL_MEMQ // MEM_W)),
            pl.BlockSpec((tq, gw), lambda i: (i, COL_GATE // gw)),
            pl.BlockSpec((MEM_W, M), const2),
            pl.BlockSpec((M, MEM_W), const2),
            pl.BlockSpec((1, MEM_HD), const2),
            pl.BlockSpec((N_BRANCH, D_MODEL), const2),
            pl.BlockSpec((N_BRANCH, ATT_W, D_MODEL), lambda i: (0, 0, 0)),
            pl.BlockSpec((D_MODEL, D_MODEL), const2),
        ],
        out_specs=pl.BlockSpec((tq, D_MODEL), lambda i: (i, 0)),
        out_shape=jax.ShapeDtypeStruct((S, D_MODEL), F32),
        compiler_params=_cparams(("parallel",)),
    )(x, y_att, y_rwkv, proj, proj, kmT, vm, mem_q_gain.reshape(1, -1), gate_bias,
      w_up.astype(MXU_DTYPE), w_out.astype(MXU_DTYPE))


def _pop_max(cur, rows):
    m = jnp.max(cur, axis=0, keepdims=True)
    first = jnp.min(jnp.where(cur == m, rows, cur.shape[0]), axis=0, keepdims=True)
    return m, jnp.where(rows == first, -jnp.inf, cur)


def _peer_route_kernel(h_ref, g_ref, wq_ref, k1_ref, k2_ref,
                       xn_ref, s1_ref, s2_ref, p1_ref, p2_ref, tau_ref):
    tq = h_ref.shape[0]
    xn = _lane_rmsnorm(h_ref[...], g_ref[...]).astype(xn_ref.dtype)
    xn_ref[...] = xn
    qp = jnp.dot(xn, wq_ref[...], preferred_element_type=F32)
    nt = (((1,), (1,)), ((), ()))
    half = PEER_QDIM // 2
    rows = lax.broadcasted_iota(I32, (PEER_KEYS, tq), 0)
    pairs = [(a, b) for a in range(PEER_TOPK) for b in range(PEER_TOPK)
             if (a + 1) * (b + 1) <= PEER_TOPK]
    npad = -len(pairs) % 8
    crow = lax.broadcasted_iota(I32, (len(pairs) + npad, tq), 0)
    taus = []
    for h in range(PEER_HEADS):
        q1 = qp[:, h * PEER_QDIM:h * PEER_QDIM + half].astype(MXU_DTYPE)
        q2 = qp[:, h * PEER_QDIM + half:(h + 1) * PEER_QDIM].astype(MXU_DTYPE)
        s1 = lax.dot_general(k1_ref[...], q1, nt, preferred_element_type=F32)
        s2 = lax.dot_general(k2_ref[...], q2, nt, preferred_element_type=F32)
        tops = []
        for s in (s1, s2):
            cur, vals = s, []
            for _ in range(PEER_TOPK):
                m, cur = _pop_max(cur, rows)
                vals.append(m)
            tops.append(vals)
        v1, v2 = tops
        cand = jnp.concatenate([v1[a] + v2[b] for a, b in pairs]
                               + [jnp.full((npad, tq), -jnp.inf, F32)], axis=0)
        top = v1[0] + v2[0]
        z = jnp.zeros((1, tq), F32)
        for _ in range(PEER_TOPK):
            m, cand = _pop_max(cand, crow)
            z = z + jnp.exp(m - top)
        taus.append(m)
        s1_ref[h] = s1
        s2_ref[h] = s2
        p1_ref[h] = jnp.exp(s1 - v1[0]) / z
        p2_ref[h] = jnp.exp(s2 - v2[0])
    tau_ref[...] = jnp.concatenate(taus, axis=0)


def _peer_route(h2, gain, w_q, key1, key2, tq):
    S = h2.shape[0]
    const2 = lambda i: (0, 0)
    half = PEER_QDIM // 2
    kt = jax.ShapeDtypeStruct((PEER_HEADS, PEER_KEYS, S), F32)
    kt_spec = pl.BlockSpec((PEER_HEADS, PEER_KEYS, tq), lambda i: (0, 0, i))
    return pl.pallas_call(
        _peer_route_kernel,
        grid=(S // tq,),
        in_specs=[
            pl.BlockSpec((tq, D_MODEL), lambda i: (i, 0)),
            pl.BlockSpec((1, D_MODEL), const2),
            pl.BlockSpec((D_MODEL, PEER_HEADS * PEER_QDIM), const2),
            pl.BlockSpec((PEER_KEYS, half), const2),
            pl.BlockSpec((PEER_KEYS, half), const2),
        ],
        out_specs=[pl.BlockSpec((tq, D_MODEL), lambda i: (i, 0)), kt_spec, kt_spec, kt_spec,
                   kt_spec, pl.BlockSpec((PEER_HEADS, tq), lambda i: (0, i))],
        out_shape=[jax.ShapeDtypeStruct((S, D_MODEL), MXU_DTYPE), kt, kt, kt, kt,
                   jax.ShapeDtypeStruct((PEER_HEADS, S), F32)],
        compiler_params=_cparams(("parallel",)),
    )(h2, gain.reshape(1, -1), w_q.astype(MXU_DTYPE), key1.astype(MXU_DTYPE),
      key2.astype(MXU_DTYPE))


def _peer_dense_kernel(xn_ref, u_ref, vT_ref, s1_ref, s2_ref, p1_ref, p2_ref, tau_ref, h_ref,
                       o_ref, acc_ref, x_ref, *, te, ne):
    j = pl.program_id(1)

    @pl.when(j == 0)
    def _():
        acc_ref[...] = jnp.zeros(acc_ref.shape, F32)

    nt = (((1,), (1,)), ((), ()))
    hid = lax.dot_general(u_ref[...], xn_ref[...], nt, preferred_element_type=F32)
    act = 0.5 * hid * (1.0 + lax.erf(hid * (2.0 ** -0.5)))
    tau = tau_ref[...]
    for el in range(te // PEER_KEYS):
        e1 = j * (te // PEER_KEYS) + el
        g = None
        for h in range(PEER_HEADS):
            s1r = s1_ref[h, pl.ds(e1, 1), :]
            p1r = p1_ref[h, pl.ds(e1, 1), :]
            keep = (s1r + s2_ref[h]) >= tau[h:h + 1, :]
            t = jnp.where(keep, p2_ref[h], 0.0) * p1r
            g = t if g is None else g + t
        sl = slice(el * PEER_KEYS, (el + 1) * PEER_KEYS)
        x_ref[sl, :] = (g * act[sl, :]).astype(x_ref.dtype)
    acc_ref[...] += jnp.dot(vT_ref[...], x_ref[...], preferred_element_type=F32)

    @pl.when(j == ne - 1)
    def _():
        o_ref[...] = h_ref[...] + acc_ref[...].T


def _peer_dense(h2, xn, u, v, s1, s2, p1, p2, tau, tm, te):
    S = h2.shape[0]
    NE = u.shape[0]
    ne = NE // te
    kt_spec = pl.BlockSpec((PEER_HEADS, PEER_KEYS, tm), lambda i, j: (0, 0, i))
    return pl.pallas_call(
        functools.partial(_peer_dense_kernel, te=te, ne=ne),
        grid=(S // tm, ne),
        in_specs=[
            pl.BlockSpec((tm, D_MODEL), lambda i, j: (i, 0)),
            pl.BlockSpec((te, D_MODEL), lambda i, j: (j, 0)),
            pl.BlockSpec((D_MODEL, te), lambda i, j: (0, j)),
            kt_spec, kt_spec, kt_spec, kt_spec,
            pl.BlockSpec((PEER_HEADS, tm), lambda i, j: (0, i)),
            pl.BlockSpec((tm, D_MODEL), lambda i, j: (i, 0)),
        ],
        out_specs=pl.BlockSpec((tm, D_MODEL), lambda i, j: (i, 0)),
        out_shape=jax.ShapeDtypeStruct((S, D_MODEL), F32),
        scratch_shapes=[pltpu.VMEM((D_MODEL, tm), F32), pltpu.VMEM((te, tm), MXU_DTYPE)],
        compiler_params=_cparams(("parallel", "arbitrary")),
    )(xn, u.astype(MXU_DTYPE), v.T.astype(MXU_DTYPE), s1, s2, p1, p2, tau, h2)


def _peer(h2, gain, w_q, key1, key2, u, v):
    S = h2.shape[0]
    xn, s1, s2, p1, p2, tau = _peer_route(h2, gain, w_q, key1, key2, min(256, S))
    return _peer_dense(h2, xn, u, v, s1, s2, p1, p2, tau, min(512, S), 1024)


def _layer(x, mem, positions, p):
    S = x.shape[0]
    proj = _norm_proj(x, p["mix_norm_gain"], _pack_w_in(p["w_in"]), min(1024, S), 768)
    qT, kh, vT, iqT, ik, iwT = _att_prep(proj, positions, p["att_q_gain"], p["att_k_gain"], 256)
    bias = _dsa_select(iqT, iwT, ik, 256, 512)
    y_att = _dsa_attend(qT, kh, vT, bias, 256, 1024)
    y_rwkv = _rwkv_time_mix(proj, p)
    kmT, vm = _mem_kv(mem, p["mem_norm_gain"], p["w_mem_kv"], p["mem_k_gain"])
    h2 = _merge(x, y_att, y_rwkv, proj, kmT, vm, p["mem_q_gain"], p["gate_bias"], p["w_up"],
                p["w_out"], 256)
    return _peer(h2, p["ffn_norm_gain"], p["peer_w_q"], p["peer_key1"], p["peer_key2"],
                 p["peer_u"], p["peer_v"])


_PARAM_NAMES = ("mix_norm_gain", "w_in", "gate_bias", "att_q_gain", "att_k_gain", "mem_norm_gain",
                "w_mem_kv", "mem_q_gain", "mem_k_gain", "rwkv_mu", "rwkv_w0", "rwkv_w2", "rwkv_a0",
                "rwkv_a2", "rwkv_g2", "rwkv_k_k", "rwkv_k_a", "rwkv_r_k", "rwkv_ln_w", "rwkv_ln_b",
                "w_up", "w_out", "ffn_norm_gain", "peer_w_q", "peer_key1", "peer_key2", "peer_u",
                "peer_v")


def kernel(x, mem, positions, mix_norm_gain, w_in, gate_bias, att_q_gain, att_k_gain,
           mem_norm_gain, w_mem_kv, mem_q_gain, mem_k_gain, rwkv_mu, rwkv_w0, rwkv_w2, rwkv_a0,
           rwkv_a2, rwkv_g2, rwkv_k_k, rwkv_k_a, rwkv_r_k, rwkv_ln_w, rwkv_ln_b, w_up, w_out,
           ffn_norm_gain, peer_w_q, peer_key1, peer_key2, peer_u, peer_v):
    params = (mix_norm_gain, w_in, gate_bias, att_q_gain, att_k_gain, mem_norm_gain, w_mem_kv,
              mem_q_gain, mem_k_gain, rwkv_mu, rwkv_w0, rwkv_w2, rwkv_a0, rwkv_a2, rwkv_g2,
              rwkv_k_k, rwkv_k_a, rwkv_r_k, rwkv_ln_w, rwkv_ln_b, w_up, w_out, ffn_norm_gain,
              peer_w_q, peer_key1, peer_key2, peer_u, peer_v)
    assert x.shape[0] == 1 and all(t.shape[0] == 1 for t in params)
    p = {name: t[0] for name, t in zip(_PARAM_NAMES, params)}
    return _layer(x[0], mem[0], positions[0], p)[None]
```

```python
import functools

import numpy as np
import jax
import jax.numpy as jnp
from jax import lax
from jax.experimental import pallas as pl
from jax.experimental.pallas import tpu as pltpu

F32 = jnp.float32
I32 = jnp.int32
MXU_DTYPE = jnp.bfloat16
HI = lax.Precision.HIGHEST

D_MODEL = 1024
ATT_HEADS, ATT_HD = 8, 64
ATT_W = ATT_HEADS * ATT_HD
IDX_HEADS, IDX_HD = 4, 64
TOPK_MAX = 256
RWKV_HEADS, RWKV_HD = 8, 64
RWKV_W = RWKV_HEADS * RWKV_HD
DECAY_LORA, AAA_LORA, GATE_LORA = 64, 64, 128
GN_EPS = 64e-5
MEM_HEADS, MEM_HD = 4, 128
MEM_W = MEM_HEADS * MEM_HD
N_BRANCH = 3
ROPE_THETA = 500000.0
ROPE_ROT = ATT_HD // 4
NORM_EPS = 1e-6
PEER_KEYS = 128
PEER_HEADS = 8
PEER_QDIM = 256
PEER_TOPK = 16

LOG2E = 1.4426950408889634
LANES = 128
INT_MIN = -(2 ** 31)
NEG_BIG = -1e30
VMEM_LIMIT = 56 * 1024 * 1024

COL_GATE = 0
COL_Q = 3072
COL_K = 3584
COL_V = 4096
COL_R = 4608
COL_RK = 5120
COL_RV = 5632
COL_MEMQ = 6144
COL_IQ = 6656
COL_LORA = 6912
COL_IKIW = 7168
PROJ_COLS = 7680


def _cparams(sem):
    return pltpu.CompilerParams(dimension_semantics=sem, vmem_limit_bytes=VMEM_LIMIT)


def _mm(a, b):
    return jnp.dot(a.astype(MXU_DTYPE), b.astype(MXU_DTYPE), preferred_element_type=F32)


def _mm_hi(a, b):
    return jnp.dot(a.astype(F32), b.astype(F32), preferred_element_type=F32, precision=HI)


def _normproj_kernel(x_ref, g_ref, w_ref, o_ref, xn_ref):
    @pl.when(pl.program_id(1) == 0)
    def _():
        x = x_ref[...]
        ms = jnp.mean(x * x, axis=-1, keepdims=True)
        xn_ref[...] = (x * lax.rsqrt(ms + NORM_EPS) * g_ref[...]).astype(xn_ref.dtype)

    o_ref[...] = jnp.dot(xn_ref[...], w_ref[...], preferred_element_type=F32)


def _norm_proj(x, gain, w, tm, tn):
    S, D = x.shape
    N = w.shape[1]
    return pl.pallas_call(
        _normproj_kernel,
        grid=(S // tm, N // tn),
        in_specs=[
            pl.BlockSpec((tm, D), lambda i, j: (i, 0)),
            pl.BlockSpec((1, D), lambda i, j: (0, 0)),
            pl.BlockSpec((D, tn), lambda i, j: (0, j)),
        ],
        out_specs=pl.BlockSpec((tm, tn), lambda i, j: (i, j)),
        out_shape=jax.ShapeDtypeStruct((S, N), F32),
        scratch_shapes=[pltpu.VMEM((tm, D), w.dtype)],
        compiler_params=_cparams(("parallel", "arbitrary")),
    )(x, gain.reshape(1, D), w)


def _pack_w_in(w_in):
    D = w_in.shape[0]
    o = 0
    parts = {}
    for name, width in (("q", ATT_W), ("k", ATT_W), ("v", ATT_W), ("iq", IDX_HEADS * IDX_HD),
                        ("ik", IDX_HD), ("iw", IDX_HEADS), ("r", RWKV_W), ("rk", RWKV_W),
                        ("rv", RWKV_W), ("lora", DECAY_LORA + AAA_LORA + GATE_LORA),
                        ("memq", MEM_W), ("gate", N_BRANCH * D_MODEL)):
        parts[name] = w_in[:, o:o + width]
        o += width
    ikiw = jnp.concatenate([parts["ik"], parts["iw"]], axis=1)
    ikiw = jnp.pad(ikiw, ((0, 0), (0, LANES - ikiw.shape[1])))
    packed = jnp.concatenate(
        [parts["gate"], parts["q"], parts["k"], parts["v"], parts["r"], parts["rk"], parts["rv"],
         parts["memq"], parts["iq"], parts["lora"], ikiw], axis=1)
    packed = jnp.pad(packed, ((0, 0), (0, PROJ_COLS - packed.shape[1])))
    return packed.astype(MXU_DTYPE)


def _rope_tables():
    d = np.arange(LANES) % ATT_HD
    inv_freq = 1.0 / (ROPE_THETA ** (np.arange(0, ROPE_ROT, 2, dtype=np.float32) / ROPE_ROT))
    half = ROPE_ROT // 2
    tab = np.zeros((8, LANES), np.float32)
    tab[0] = np.where(d < ROPE_ROT, inv_freq[d % half], 0.0)
    tab[1] = np.where(d < half, -1.0, 0.0)
    tab[2] = np.where((d >= half) & (d < ROPE_ROT), 1.0, 0.0)
    return jnp.asarray(tab)


def _seg_mean_matrix(width, seg):
    g = (np.arange(width)[:, None] // seg) == (np.arange(width)[None, :] // seg)
    return jnp.asarray(g.astype(np.float32) / seg)


def _rope(x, cos, sina, sinb):
    W = x.shape[1]
    reps = W // LANES
    if reps > 1:
        cos = jnp.concatenate([cos] * reps, axis=1)
        sina = jnp.concatenate([sina] * reps, axis=1)
        sinb = jnp.concatenate([sinb] * reps, axis=1)
    half = ROPE_ROT // 2
    up = pltpu.roll(x, W - half, axis=1)
    down = pltpu.roll(x, half, axis=1)
    return x * cos + up * sina + down * sinb


def _att_prep_kernel(q_ref, k_ref, v_ref, iq_ref, ikiw_ref, pos_ref, tab_ref, seg_ref,
                     qg_ref, kg_ref, qT_ref, kh_ref, vT_ref, iqT_ref, ik_ref, iwT_ref):
    tab = tab_ref[...]
    ang = pos_ref[...].astype(F32) * tab[0:1, :]
    cos = jnp.cos(ang)
    sin = jnp.sin(ang)
    sina = sin * tab[1:2, :]
    sinb = sin * tab[2:3, :]
    seg = seg_ref[...]

    def head_norm(x, g):
        ms = _mm_hi(x * x, seg)
        return x * lax.rsqrt(ms + NORM_EPS) * g

    q = _rope(head_norm(q_ref[...], qg_ref[...]), cos, sina, sinb) * (ATT_HD ** -0.5 * LOG2E)
    k = _rope(head_norm(k_ref[...], kg_ref[...]), cos, sina, sinb)
    v = v_ref[...]
    iq = _rope(iq_ref[...], cos, sina, sinb)
    ik = _rope(ikiw_ref[...], cos, sina, sinb)
    for h in range(ATT_HEADS):
        kh_ref[h] = k[:, h * ATT_HD:(h + 1) * ATT_HD].astype(kh_ref.dtype)
    qT_ref[...] = q.T.astype(qT_ref.dtype)
    vT_ref[...] = v.T.astype(vT_ref.dtype)
    iqT_ref[...] = iq.T.astype(iqT_ref.dtype)
    ik_ref[...] = ik[:, :IDX_HD].astype(ik_ref.dtype)
    idx_scale = (IDX_HEADS ** -0.5) * (IDX_HD ** -0.5)
    iwT_ref[...] = ikiw_ref[...].T[IDX_HD:IDX_HD + 8, :] * idx_scale


def _att_prep(proj, positions, att_q_gain, att_k_gain, tq):
    S = proj.shape[0]
    qg = jnp.tile(att_q_gain.reshape(1, ATT_HD), (1, ATT_HEADS))
    kg = jnp.tile(att_k_gain.reshape(1, ATT_HD), (1, ATT_HEADS))
    col = lambda off, w: (lambda i: (i, off // w))
    const = lambda i: (0, 0)
    return pl.pallas_call(
        _att_prep_kernel,
        grid=(S // tq,),
        in_specs=[
            pl.BlockSpec((tq, ATT_W), col(COL_Q, ATT_W)),
            pl.BlockSpec((tq, ATT_W), col(COL_K, ATT_W)),
            pl.BlockSpec((tq, ATT_W), col(COL_V, ATT_W)),
            pl.BlockSpec((tq, IDX_HEADS * IDX_HD), col(COL_IQ, IDX_HEADS * IDX_HD)),
            pl.BlockSpec((tq, LANES), col(COL_IKIW, LANES)),
            pl.BlockSpec((tq, 1), lambda i: (i, 0)),
            pl.BlockSpec((8, LANES), const),
            pl.BlockSpec((ATT_W, ATT_W), const),
            pl.BlockSpec((1, ATT_W), const),
            pl.BlockSpec((1, ATT_W), const),
        ],
        out_specs=[
            pl.BlockSpec((ATT_W, tq), lambda i: (0, i)),
            pl.BlockSpec((ATT_HEADS, tq, ATT_HD), lambda i: (0, i, 0)),
            pl.BlockSpec((ATT_W, tq), lambda i: (0, i)),
            pl.BlockSpec((IDX_HEADS * IDX_HD, tq), lambda i: (0, i)),
            pl.BlockSpec((tq, IDX_HD), lambda i: (i, 0)),
            pl.BlockSpec((8, tq), lambda i: (0, i)),
        ],
        out_shape=[
            jax.ShapeDtypeStruct((ATT_W, S), MXU_DTYPE),
            jax.ShapeDtypeStruct((ATT_HEADS, S, ATT_HD), MXU_DTYPE),
            jax.ShapeDtypeStruct((ATT_W, S), MXU_DTYPE),
            jax.ShapeDtypeStruct((IDX_HEADS * IDX_HD, S), MXU_DTYPE),
            jax.ShapeDtypeStruct((S, IDX_HD), MXU_DTYPE),
            jax.ShapeDtypeStruct((8, S), F32),
        ],
        compiler_params=_cparams(("parallel",)),
    )(proj, proj, proj, proj, proj, positions.reshape(S, 1), _rope_tables(),
      _seg_mean_matrix(ATT_W, ATT_HD), qg, kg)


KEY_NEG_INF = -2139095041


def _key_to_float(key):
    f = pltpu.bitcast(key ^ ((key >> 31) & 0x7FFFFFFF), F32)
    return jnp.where(key < KEY_NEG_INF, -jnp.inf, f)


SUB_ACC = 64


def _sel_kernel(iqT_ref, iwT_ref, ik_ref, tri_ref, bias_ref, sc_ref, *, tq, tk, seq, nsel):
    i = pl.program_id(0)
    q0 = i * tq
    nc = (q0 + tq + tk - 1) // tk
    qidx = q0 + lax.broadcasted_iota(I32, (tk, tq), 1)
    krow = lax.broadcasted_iota(I32, (tk, tq), 0)
    iw = iwT_ref[...]

    def score_chunk(c, carry):
        c0 = pl.multiple_of(c * tk, tk)
        ikc = ik_ref[pl.ds(c0, tk), :]
        sc = None
        for h in range(IDX_HEADS):
            logit = jnp.dot(ikc, iqT_ref[h * IDX_HD:(h + 1) * IDX_HD, :],
                            preferred_element_type=F32)
            t = jnp.maximum(logit, 0.0) * iw[h:h + 1, :]
            sc = t if sc is None else sc + t
        sc_ref[pl.ds(c0, tk), :] = jnp.where(c0 + krow <= qidx, sc, -jnp.inf)
        return carry

    lax.fori_loop(0, nc, score_chunk, 0)

    def count(pred, n_out=1):
        def body(c, accs):
            c0 = pl.multiple_of(c * tk, tk)
            ms = pred(sc_ref[pl.ds(c0, tk), :])
            return tuple(a + jnp.sum(m.reshape(tk // SUB_ACC, SUB_ACC, tq), axis=0)
                         for a, m in zip(accs, ms))
        accs = lax.fori_loop(0, nc, body, (jnp.zeros((SUB_ACC, tq), F32),) * n_out)
        return [_col_reduce(a, jnp.add) for a in accs]

    def count_ge(cand):
        return count(lambda s: (jnp.where(s >= cand, 1.0, 0.0),))[0]

    zero = jnp.zeros((1, tq), I32)
    n_nonneg = count_ge(0.0)
    n_pos = count(lambda s: (jnp.where(s > 0.0, 1.0, 0.0),))[0]
    nonneg = n_nonneg >= nsel
    prefix = jnp.where(nonneg, zero, zero + INT_MIN)
    n_pref = jnp.where(nonneg, n_nonneg, (nc * tk).astype(F32))
    zero_kth = jnp.where(nonneg, jnp.where(n_pos < nsel, 1.0, 0.0), 0.0)

    def bit_body(b, carry):
        prefix, n_pref = carry
        settled = jnp.where(n_pref == nsel, 1.0, zero_kth)
        cand = prefix | jnp.left_shift(jnp.int32(1), 30 - b)
        cnt = lax.cond(jnp.min(settled) > 0.5,
                       lambda: jnp.zeros((1, tq), F32),
                       lambda: count_ge(_key_to_float(cand)))
        take = jnp.where(cnt >= nsel, 1.0 - zero_kth, 0.0) > 0.5
        return jnp.where(take, cand, prefix), jnp.where(take, cnt, n_pref)

    tau = _key_to_float(lax.fori_loop(0, 31, bit_body, (prefix, n_pref))[0])

    def is_neg(s):
        return pltpu.bitcast(s, I32) < 0

    def tie_classes(s):
        eq = s == tau
        neg = is_neg(s)
        return (jnp.where(eq, jnp.where(neg, 0.0, 1.0), 0.0),
                jnp.where(eq, jnp.where(neg, 1.0, 0.0), 0.0))

    n_gt, n_eq_pos, n_eq_neg = count(
        lambda s: (jnp.where(s > tau, 1.0, 0.0),) + tie_classes(s), n_out=3)
    need = nsel - n_gt
    mixed = jnp.max(jnp.minimum(n_eq_pos, n_eq_neg)) > 0
    tri = tri_ref[...]

    def write_pass(both_signs):
        def body(c, carry):
            c0 = pl.multiple_of(c * tk, tk)
            s = sc_ref[pl.ds(c0, tk), :]
            if both_signs:
                ind_pos, ind_neg = tie_classes(s)
                cum_pos = carry[0] + _mm(tri, ind_pos)
                cum_neg = carry[1] + _mm(tri, ind_neg)
                rank = jnp.where(is_neg(s), n_eq_pos + cum_neg, cum_pos)
                carry = (cum_pos[tk - 1:tk, :], cum_neg[tk - 1:tk, :])
            else:
                rank = carry[0] + _mm(tri, jnp.where(s == tau, 1.0, 0.0))
                carry = (rank[tk - 1:tk, :], carry[1])
            tie = jnp.where(rank <= need, 0.0, NEG_BIG)
            b = jnp.where(s > tau, 0.0, jnp.where(s == tau, tie, NEG_BIG))
            b = jnp.where(c0 + krow <= qidx, b, NEG_BIG)
            bias_ref[pl.ds(c0, tk), :] = b.astype(bias_ref.dtype)
            return carry
        zf = jnp.zeros((1, tq), F32)
        lax.fori_loop(0, nc, body, (zf, zf))

    lax.cond(mixed, lambda: write_pass(True), lambda: write_pass(False))

    def fill_chunk(c, carry):
        c0 = pl.multiple_of(c * tk, tk)
        bias_ref[pl.ds(c0, tk), :] = jnp.full((tk, tq), NEG_BIG, bias_ref.dtype)
        return carry

    lax.fori_loop(nc, seq // tk, fill_chunk, 0)


def _dsa_select(iqT, iwT, ik, tq, tk):
    S = ik.shape[0]
    nsel = min(TOPK_MAX, S // 4)
    kern = functools.partial(_sel_kernel, tq=tq, tk=tk, seq=S, nsel=nsel)
    return pl.pallas_call(
        kern,
        grid=(S // tq,),
        in_specs=[
            pl.BlockSpec((IDX_HEADS * IDX_HD, tq), lambda i: (0, i)),
            pl.BlockSpec((8, tq), lambda i: (0, i)),
            pl.BlockSpec((S, IDX_HD), lambda i: (0, 0)),
            pl.BlockSpec((tk, tk), lambda i: (0, 0)),
        ],
        out_specs=pl.BlockSpec((S, tq), lambda i: (0, i)),
        out_shape=jax.ShapeDtypeStruct((S, S), jnp.bfloat16),
        scratch_shapes=[pltpu.VMEM((S, tq), F32)],
        compiler_params=_cparams(("parallel",)),
    )(iqT, iwT, ik, jnp.tril(jnp.ones((tk, tk), MXU_DTYPE)))


def _col_reduce(x, op):
    rows = x.shape[0]
    while rows > 8:
        rows //= 2
        x = op(x[:rows], x[rows:])
    red = jnp.max if op is jnp.maximum else jnp.sum
    return red(x, axis=0, keepdims=True)


def _att_kernel(qi_ref, kj_ref, qT_ref, k_ref, vT_ref, bias_ref, o_ref, m_ref, l_ref, acc_ref,
                *, tq, tk):
    t = pl.program_id(0)
    i = qi_ref[t]
    j = kj_ref[t]
    last = ((i + 1) * tq - 1) // tk

    @pl.when(j == 0)
    def _():
        m_ref[...] = jnp.full(m_ref.shape, -jnp.inf, F32)
        l_ref[...] = jnp.zeros(l_ref.shape, F32)
        acc_ref[...] = jnp.zeros(acc_ref.shape, F32)

    bias = bias_ref[...].astype(F32)
    hs = lambda h: slice(h * ATT_HD, (h + 1) * ATT_HD)
    m_all = m_ref[...]
    l_all = l_ref[...]
    m_out, l_out = [], []
    halves = (slice(0, tk // 2), slice(tk // 2, tk))
    for h in range(ATT_HEADS):
        qh = qT_ref[hs(h), :]
        ss = [jnp.dot(k_ref[h, ks, :], qh, preferred_element_type=F32) + bias[ks, :]
              for ks in halves]
        m_prev = m_all[h:h + 1, :]
        m_new = jnp.maximum(m_prev, jnp.maximum(_col_reduce(ss[0], jnp.maximum),
                                                _col_reduce(ss[1], jnp.maximum)))
        alpha = jnp.exp2(m_prev - m_new)
        ps = [jnp.exp2(s - m_new) for s in ss]
        l_out.append(alpha * l_all[h:h + 1, :] + _col_reduce(ps[0], jnp.add)
                     + _col_reduce(ps[1], jnp.add))
        m_out.append(m_new)
        pv = [jnp.dot(vT_ref[hs(h), ks], p.astype(vT_ref.dtype), preferred_element_type=F32)
              for ks, p in zip(halves, ps)]
        acc_ref[hs(h), :] = alpha * acc_ref[hs(h), :] + pv[0] + pv[1]
    m_ref[...] = jnp.concatenate(m_out, axis=0)
    l_ref[...] = jnp.concatenate(l_out, axis=0)

    @pl.when(j == last)
    def _():
        out = jnp.concatenate(
            [acc_ref[h * ATT_HD:(h + 1) * ATT_HD, :] / l_ref[h:h + 1, :]
             for h in range(ATT_HEADS)], axis=0)
        o_ref[...] = out.T


def _dsa_attend(qT, kh, vT, bias, tq, tk):
    S = qT.shape[1]
    pairs = [(i, j) for i in range(S // tq) for j in range(((i + 1) * tq - 1) // tk + 1)]
    qi = jnp.asarray([p[0] for p in pairs], I32)
    kj = jnp.asarray([p[1] for p in pairs], I32)
    grid_spec = pltpu.PrefetchScalarGridSpec(
        num_scalar_prefetch=2,
        grid=(len(pairs),),
        in_specs=[
            pl.BlockSpec((ATT_W, tq), lambda t, qi, kj: (0, qi[t])),
            pl.BlockSpec((ATT_HEADS, tk, ATT_HD), lambda t, qi, kj: (0, kj[t], 0)),
            pl.BlockSpec((ATT_W, tk), lambda t, qi, kj: (0, kj[t])),
            pl.BlockSpec((tk, tq), lambda t, qi, kj: (kj[t], qi[t])),
        ],
        out_specs=pl.BlockSpec((tq, ATT_W), lambda t, qi, kj: (qi[t], 0)),
        scratch_shapes=[
            pltpu.VMEM((ATT_HEADS, tq), F32),
            pltpu.VMEM((ATT_HEADS, tq), F32),
            pltpu.VMEM((ATT_W, tq), F32),
        ],
    )
    return pl.pallas_call(
        functools.partial(_att_kernel, tq=tq, tk=tk),
        grid_spec=grid_spec,
        out_shape=jax.ShapeDtypeStruct((S, ATT_W), F32),
        compiler_params=_cparams(("arbitrary",)),
    )(qi, kj, qT, kh, vT, bias)


CHUNK = 64


def _shift_rows(x, prev8, first):
    prev_row = jnp.where(first, 0.0, prev8[7:8, :])
    row = lax.broadcasted_iota(I32, x.shape, 0)
    return jnp.where(row == 0, prev_row, pltpu.roll(x, 1, axis=0))


def _softplus(z):
    return jnp.maximum(z, 0.0) + jnp.log1p(jnp.exp(-jnp.abs(z)))


def _rwkv_prep_kernel(r_ref, k_ref, v_ref, lo_ref, rp_ref, kp_ref, vp_ref, lop_ref,
                      mur_ref, muk_ref, muv_ref, mulo_ref, w0_ref, w2_ref, a0_ref, a2_ref,
                      g2_ref, kk_ref, ka_ref, rk_ref, seg_ref, tri_ref, ones_ref, end_ref,
                      rt_ref, kt_ref, bt_ref, at_ref, kh_ref, bh_ref, vh_ref, gam_ref,
                      bonus_ref, g_ref):
    first = pl.program_id(0) == 0

    def mix(x_ref, p_ref, mu_ref):
        x = x_ref[...]
        return x + (_shift_rows(x, p_ref[...], first) - x) * mu_ref[...]

    r = mix(r_ref, rp_ref, mur_ref)
    k = mix(k_ref, kp_ref, muk_ref)
    v = mix(v_ref, vp_ref, muv_ref)
    lo = mix(lo_ref, lop_ref, mulo_ref)
    wd = lo[:, :DECAY_LORA]
    ad = lo[:, DECAY_LORA:DECAY_LORA + AAA_LORA]
    gd = lo[:, DECAY_LORA + AAA_LORA:]
    w_log = -_softplus(-(w0_ref[...] + _mm(jnp.tanh(wd), w2_ref[...]))) - 0.5
    lw = -jnp.exp(w_log)
    a = jax.nn.sigmoid(a0_ref[...] + _mm(ad, a2_ref[...]))
    g_ref[...] = _mm(jax.nn.sigmoid(gd), g2_ref[...])
    seg = seg_ref[...]
    kk = k * kk_ref[...]
    kk = kk / jnp.maximum(jnp.sqrt(_mm_hi(kk * kk, seg)), 1e-12)
    k = k * (1.0 + (a - 1.0) * ka_ref[...])
    bonus_ref[...] = _mm_hi(r * k * rk_ref[...], seg) * v
    avec = -kk
    bvec = kk * a
    cs = _mm_hi(tri_ref[...], lw)
    cs_end = _mm_hi(ones_ref[...], lw)
    e_neg = jnp.exp(-cs)
    e_end = jnp.exp(cs_end - cs)
    outs = ((rt_ref, r * jnp.exp(cs)), (kt_ref, k * e_neg), (bt_ref, bvec * e_neg),
            (at_ref, avec * jnp.exp(cs - lw)), (kh_ref, k * e_end), (bh_ref, bvec * e_end),
            (vh_ref, v), (gam_ref, jnp.exp(_mm_hi(end_ref[...], lw))))
    for ref, val in outs:
        for h in range(RWKV_HEADS):
            ref[h] = val[:, h * RWKV_HD:(h + 1) * RWKV_HD]


def _rwkv_prep(proj, p, tq):
    S = proj.shape[0]
    nch = tq // CHUNK
    col = lambda off, w: (lambda i: (i, off // w))
    prev = lambda off, w: (lambda i: (jnp.maximum(i * (tq // 8) - 1, 0), off // w))
    const = lambda i: (0, 0)
    lw_ = DECAY_LORA + AAA_LORA + GATE_LORA
    mu = p["rwkv_mu"]
    row = lambda t: t.reshape(1, -1)
    t_idx = np.arange(tq)
    same = (t_idx[:, None] // CHUNK) == (t_idx[None, :] // CHUNK)
    tri = jnp.asarray((same & (t_idx[None, :] <= t_idx[:, None])).astype(np.float32))
    ones = jnp.asarray(same.astype(np.float32))
    end = jnp.asarray(((t_idx[None, :] // CHUNK) == np.arange(nch)[:, None]).astype(np.float32))
    seg = _seg_mean_matrix(RWKV_W, RWKV_HD) * RWKV_HD
    hm = jax.ShapeDtypeStruct((RWKV_HEADS, S, RWKV_HD), F32)
    hm_spec = pl.BlockSpec((RWKV_HEADS, tq, RWKV_HD), lambda i: (0, i, 0))
    wide = jax.ShapeDtypeStruct((S, RWKV_W), F32)
    wide_spec = pl.BlockSpec((tq, RWKV_W), lambda i: (i, 0))
    vec = lambda w: pl.BlockSpec((1, w), const)
    return pl.pallas_call(
        _rwkv_prep_kernel,
        grid=(S // tq,),
        in_specs=[
            pl.BlockSpec((tq, RWKV_W), col(COL_R, RWKV_W)),
            pl.BlockSpec((tq, RWKV_W), col(COL_RK, RWKV_W)),
            pl.BlockSpec((tq, RWKV_W), col(COL_RV, RWKV_W)),
            pl.BlockSpec((tq, lw_), col(COL_LORA, lw_)),
            pl.BlockSpec((8, RWKV_W), prev(COL_R, RWKV_W)),
            pl.BlockSpec((8, RWKV_W), prev(COL_RK, RWKV_W)),
            pl.BlockSpec((8, RWKV_W), prev(COL_RV, RWKV_W)),
            pl.BlockSpec((8, lw_), prev(COL_LORA, lw_)),
            vec(RWKV_W), vec(RWKV_W), vec(RWKV_W), vec(lw_),
            vec(RWKV_W), pl.BlockSpec((DECAY_LORA, RWKV_W), const),
            vec(RWKV_W), pl.BlockSpec((AAA_LORA, RWKV_W), const),
            pl.BlockSpec((GATE_LORA, RWKV_W), const),
            vec(RWKV_W), vec(RWKV_W), vec(RWKV_W),
            pl.BlockSpec((RWKV_W, RWKV_W), const),
            pl.BlockSpec((tq, tq), const), pl.BlockSpec((tq, tq), const),
            pl.BlockSpec((nch, tq), const),
        ],
        out_specs=[hm_spec] * 7 + [pl.BlockSpec((RWKV_HEADS, nch, RWKV_HD), lambda i: (0, i, 0)),
                                   wide_spec, wide_spec],
        out_shape=[hm] * 7 + [jax.ShapeDtypeStruct((RWKV_HEADS, S // CHUNK, RWKV_HD), F32),
                              wide, wide],
        compiler_params=_cparams(("parallel",)),
    )(proj, proj, proj, proj, proj, proj, proj, proj,
      row(mu[:RWKV_W]), row(mu[RWKV_W:2 * RWKV_W]), row(mu[2 * RWKV_W:3 * RWKV_W]),
      row(mu[3 * RWKV_W:]), row(p["rwkv_w0"]), p["rwkv_w2"], row(p["rwkv_a0"]), p["rwkv_a2"],
      p["rwkv_g2"], row(p["rwkv_k_k"]), row(p["rwkv_k_a"]), row(p["rwkv_r_k"]), seg, tri, ones, end)


def _bmm(a, b, dims):
    return jnp.einsum(dims, a, b, preferred_element_type=F32, precision=HI)


def _bmm1(a, b, dims):
    return jnp.einsum(dims, a.astype(MXU_DTYPE), b.astype(MXU_DTYPE), preferred_element_type=F32)


def _rwkv_chunk_kernel(rt_ref, kt_ref, bt_ref, at_ref, kh_ref, bh_ref, v_ref, gam_ref,
                       p_ref, q_ref, rw_ref, y0_ref, *, nch):
    L, N = CHUNK, RWKV_HD
    ri = lax.broadcasted_iota(I32, (nch, L, L), 1)
    ci = lax.broadcasted_iota(I32, (nch, L, L), 2)
    eye_n = (lax.broadcasted_iota(I32, (nch, N, N), 1)
             == lax.broadcasted_iota(I32, (nch, N, N), 2)).astype(F32)
    for h in range(RWKV_HEADS):
        ld = lambda ref: ref[h].reshape(nch, L, N)
        rt, kt, bt, at, kh, bh, v = (ld(x) for x in (rt_ref, kt_ref, bt_ref, at_ref, kh_ref,
                                                      bh_ref, v_ref))
        gam = gam_ref[h].reshape(nch, 1, N)
        mm = _bmm1
        a_ab = jnp.where(ci < ri, mm(at, bt, "cld,cmd->clm"), 0.0)
        a_ak = jnp.where(ci < ri, mm(at, kt, "cld,cmd->clm"), 0.0)
        m_rk = jnp.where(ci <= ri, mm(rt, kt, "cld,cmd->clm"), 0.0)
        m_rb = jnp.where(ci <= ri, mm(rt, bt, "cld,cmd->clm"), 0.0)
        rhs = jnp.concatenate([at, mm(a_ak, v, "clm,cmd->cld")], axis=2)
        pw = a_ab
        step = 1
        while True:
            rhs = rhs + mm(pw, rhs, "clm,cmd->cld")
            step *= 2
            if step >= L:
                break
            pw = mm(pw, pw, "clm,cmn->cln")
        w, u0 = rhs[:, :, :N], rhs[:, :, N:]
        p_ref[:, h] = eye_n * gam + mm(bh, w, "cld,cle->cde")
        q_ref[:, h] = mm(kh, v, "cld,cle->cde") + mm(bh, u0, "cld,cle->cde")
        rw_ref[h] = (rt + mm(m_rb, w, "clm,cmd->cld")).reshape(nch * L, N)
        y0_ref[h] = (mm(m_rk, v, "clm,cmd->cld") + mm(m_rb, u0, "clm,cmd->cld")).reshape(nch * L, N)


def _rwkv_chunks(rt, kt, bt, at, kh, bh, vh, gam, tt):
    S = rt.shape[1]
    nch = tt // CHUNK
    hm_spec = pl.BlockSpec((RWKV_HEADS, tt, RWKV_HD), lambda i: (0, i, 0))
    hm = jax.ShapeDtypeStruct((RWKV_HEADS, S, RWKV_HD), F32)
    sq_spec = pl.BlockSpec((nch, RWKV_HEADS, RWKV_HD, RWKV_HD), lambda i: (i, 0, 0, 0))
    sq = jax.ShapeDtypeStruct((S // CHUNK, RWKV_HEADS, RWKV_HD, RWKV_HD), F32)
    return pl.pallas_call(
        functools.partial(_rwkv_chunk_kernel, nch=nch),
        grid=(S // tt,),
        in_specs=[hm_spec] * 7 + [pl.BlockSpec((RWKV_HEADS, nch, RWKV_HD), lambda i: (0, i, 0))],
        out_specs=[sq_spec, sq_spec, hm_spec, hm_spec],
        out_shape=[sq, sq, hm, hm],
        compiler_params=_cparams(("parallel",)),
    )(rt, kt, bt, at, kh, bh, vh, gam)


def _rwkv_scan_kernel(p_ref, q_ref, rw_ref, y0_ref, bonus_ref, g_ref, seg_ref, lnw_ref, lnb_ref,
                      o_ref, st_ref, *, nch):
    @pl.when(pl.program_id(0) == 0)
    def _():
        st_ref[...] = jnp.zeros(st_ref.shape, F32)

    L = CHUNK
    st = st_ref[...]
    ys = []
    for c in range(nch):
        rw = rw_ref[:, c * L:(c + 1) * L, :]
        ys.append(_bmm(rw, st, "hld,hde->hle") + y0_ref[:, c * L:(c + 1) * L, :])
        st = _bmm(p_ref[c], st, "hjk,hki->hji") + q_ref[c]
    st_ref[...] = st
    y = jnp.concatenate(ys, axis=1)
    y = jnp.concatenate([y[h] for h in range(RWKV_HEADS)], axis=1)
    seg = seg_ref[...]
    mean = _mm_hi(y, seg)
    d = y - mean
    var = _mm_hi(d * d, seg)
    y = d * lax.rsqrt(var + GN_EPS) * lnw_ref[...] + lnb_ref[...]
    o_ref[...] = (y + bonus_ref[...]) * g_ref[...]


def _rwkv_scan(pm, qm, rw, y0, bonus, g, ln_w, ln_b, tt):
    S = rw.shape[1]
    nch = tt // CHUNK
    hm_spec = pl.BlockSpec((RWKV_HEADS, tt, RWKV_HD), lambda i: (0, i, 0))
    sq_spec = pl.BlockSpec((nch, RWKV_HEADS, RWKV_HD, RWKV_HD), lambda i: (i, 0, 0, 0))
    wide_spec = pl.BlockSpec((tt, RWKV_W), lambda i: (i, 0))
    const = lambda i: (0, 0)
    return pl.pallas_call(
        functools.partial(_rwkv_scan_kernel, nch=nch),
        grid=(S // tt,),
        in_specs=[sq_spec, sq_spec, hm_spec, hm_spec, wide_spec, wide_spec,
                  pl.BlockSpec((RWKV_W, RWKV_W), const), pl.BlockSpec((1, RWKV_W), const),
                  pl.BlockSpec((1, RWKV_W), const)],
        out_specs=wide_spec,
        out_shape=jax.ShapeDtypeStruct((S, RWKV_W), F32),
        scratch_shapes=[pltpu.VMEM((RWKV_HEADS, RWKV_HD, RWKV_HD), F32)],
        compiler_params=_cparams(("arbitrary",)),
    )(pm, qm, rw, y0, bonus, g, _seg_mean_matrix(RWKV_W, RWKV_HD), ln_w.reshape(1, -1),
      ln_b.reshape(1, -1))


def _rwkv_time_mix(proj, p):
    outs = _rwkv_prep(proj, p, 512)
    rt, kt, bt, at, kh, bh, vh, gam, bonus, g = outs
    pm, qm, rw, y0 = _rwkv_chunks(rt, kt, bt, at, kh, bh, vh, gam, 512)
    return _rwkv_scan(pm, qm, rw, y0, bonus, g, p["rwkv_ln_w"], p["rwkv_ln_b"], 512)


def _lane_rmsnorm(x, gain):
    return x * lax.rsqrt(jnp.mean(x * x, axis=-1, keepdims=True) + NORM_EPS) * gain


def _mem_kv_kernel(mem_ref, g_ref, w_ref, kg_ref, kmT_ref, vm_ref):
    m = _lane_rmsnorm(mem_ref[...], g_ref[...])
    kv = _mm(m, w_ref[...])
    km = jnp.concatenate(
        [_lane_rmsnorm(kv[:, h * MEM_HD:(h + 1) * MEM_HD], kg_ref[...]) for h in range(MEM_HEADS)],
        axis=1)
    kmT_ref[...] = km.T.astype(kmT_ref.dtype)
    vm_ref[...] = kv[:, MEM_W:].astype(vm_ref.dtype)


def _mem_kv(mem, gain, w_kv, k_gain):
    M = mem.shape[0]
    return pl.pallas_call(
        _mem_kv_kernel,
        out_shape=[jax.ShapeDtypeStruct((MEM_W, M), MXU_DTYPE),
                   jax.ShapeDtypeStruct((M, MEM_W), MXU_DTYPE)],
        compiler_params=pltpu.CompilerParams(vmem_limit_bytes=VMEM_LIMIT),
    )(mem, gain.reshape(1, -1), w_kv.astype(MXU_DTYPE), k_gain.reshape(1, -1))


def _merge_kernel(x_ref, yatt_ref, yrwkv_ref, memq_ref, gate_ref, kmT_ref, vm_ref, qg_ref,
                  gb_ref, wup_ref, wout_ref, o_ref):
    q = memq_ref[...]
    heads = []
    for h in range(MEM_HEADS):
        sl = slice(h * MEM_HD, (h + 1) * MEM_HD)
        qn = _lane_rmsnorm(q[:, sl], qg_ref[...])
        s = _mm(qn, kmT_ref[sl, :]) * (MEM_HD ** -0.5)
        s = s - jnp.max(s, axis=-1, keepdims=True)
        e = jnp.exp(s)
        p = e / jnp.sum(e, axis=-1, keepdims=True)
        heads.append(_mm(p, vm_ref[:, sl]))
    ymem = jnp.concatenate(heads, axis=1)
    merged = None
    for c, y in enumerate((yatt_ref[...], yrwkv_ref[...], ymem)):
        up = _mm(y, wup_ref[c])
        gate = jax.nn.sigmoid(gate_ref[:, c * D_MODEL:(c + 1) * D_MODEL] + gb_ref[c:c + 1, :])
        merged = gate * up if merged is None else merged + gate * up
    o_ref[...] = x_ref[...] + _mm(merged, wout_ref[...])


def _merge(x, y_att, y_rwkv, proj, kmT, vm, mem_q_gain, gate_bias, w_up, w_out, tq):
    S = x.shape[0]
    M = vm.shape[0]
    gw = N_BRANCH * D_MODEL
    const2 = lambda i: (0, 0)
    return pl.pallas_call(
        _merge_kernel,
        grid=(S // tq,),
        in_specs=[
            pl.BlockSpec((tq, D_MODEL), lambda i: (i, 0)),
            pl.BlockSpec((tq, ATT_W), lambda i: (i, 0)),
            pl.BlockSpec((tq, RWKV_W), lambda i: (i, 0)),
            pl.BlockSpec((tq, MEM_W), lambda i: (i, COL_MEMQ // MEM_W)),
            pl.BlockSpec((tq, gw), lambda i: (i, COL_GATE // gw)),
            pl.BlockSpec((MEM_W, M), const2),
            pl.BlockSpec((M, MEM_W), const2),
            pl.BlockSpec((1, MEM_HD), const2),
            pl.BlockSpec((N_BRANCH, D_MODEL), const2),
            pl.BlockSpec((N_BRANCH, ATT_W, D_MODEL), lambda i: (0, 0, 0)),
            pl.BlockSpec((D_MODEL, D_MODEL), const2),
        ],
        out_specs=pl.BlockSpec((tq, D_MODEL), lambda i: (i, 0)),
        out_shape=jax.ShapeDtypeStruct((S, D_MODEL), F32),
        compiler_params=_cparams(("parallel",)),
    )(x, y_att, y_rwkv, proj, proj, kmT, vm, mem_q_gain.reshape(1, -1), gate_bias,
      w_up.astype(MXU_DTYPE), w_out.astype(MXU_DTYPE))


def _pop_max(cur, rows):
    m = jnp.max(cur, axis=0, keepdims=True)
    first = jnp.min(jnp.where(cur == m, rows, cur.shape[0]), axis=0, keepdims=True)
    return m, jnp.where(rows == first, -jnp.inf, cur)


def _peer_route_kernel(h_ref, g_ref, wq_ref, k1_ref, k2_ref,
                       xn_ref, s1_ref, s2_ref, p1_ref, p2_ref, tau_ref):
    tq = h_ref.shape[0]
    xn = _lane_rmsnorm(h_ref[...], g_ref[...]).astype(xn_ref.dtype)
    xn_ref[...] = xn
    qp = jnp.dot(xn, wq_ref[...], preferred_element_type=F32)
    nt = (((1,), (1,)), ((), ()))
    half = PEER_QDIM // 2
    rows = lax.broadcasted_iota(I32, (PEER_KEYS, tq), 0)
    pairs = [(a, b) for a in range(PEER_TOPK) for b in range(PEER_TOPK)
             if (a + 1) * (b + 1) <= PEER_TOPK]
    npad = -len(pairs) % 8
    crow = lax.broadcasted_iota(I32, (len(pairs) + npad, tq), 0)
    taus = []
    for h in range(PEER_HEADS):
        q1 = qp[:, h * PEER_QDIM:h * PEER_QDIM + half].astype(MXU_DTYPE)
        q2 = qp[:, h * PEER_QDIM + half:(h + 1) * PEER_QDIM].astype(MXU_DTYPE)
        s1 = lax.dot_general(k1_ref[...], q1, nt, preferred_element_type=F32)
        s2 = lax.dot_general(k2_ref[...], q2, nt, preferred_element_type=F32)
        tops = []
        for s in (s1, s2):
            cur, vals = s, []
            for _ in range(PEER_TOPK):
                m, cur = _pop_max(cur, rows)
                vals.append(m)
            tops.append(vals)
        v1, v2 = tops
        cand = jnp.concatenate([v1[a] + v2[b] for a, b in pairs]
                               + [jnp.full((npad, tq), -jnp.inf, F32)], axis=0)
        top = v1[0] + v2[0]
        z = jnp.zeros((1, tq), F32)
        for _ in range(PEER_TOPK):
            m, cand = _pop_max(cand, crow)
            z = z + jnp.exp(m - top)
        taus.append(m)
        s1_ref[h] = s1
        s2_ref[h] = s2
        p1_ref[h] = jnp.exp(s1 - v1[0]) / z
        p2_ref[h] = jnp.exp(s2 - v2[0])
    tau_ref[...] = jnp.concatenate(taus, axis=0)


def _peer_route(h2, gain, w_q, key1, key2, tq):
    S = h2.shape[0]
    const2 = lambda i: (0, 0)
    half = PEER_QDIM // 2
    kt = jax.ShapeDtypeStruct((PEER_HEADS, PEER_KEYS, S), F32)
    kt_spec = pl.BlockSpec((PEER_HEADS, PEER_KEYS, tq), lambda i: (0, 0, i))
    return pl.pallas_call(
        _peer_route_kernel,
        grid=(S // tq,),
        in_specs=[
            pl.BlockSpec((tq, D_MODEL), lambda i: (i, 0)),
            pl.BlockSpec((1, D_MODEL), const2),
            pl.BlockSpec((D_MODEL, PEER_HEADS * PEER_QDIM), const2),
            pl.BlockSpec((PEER_KEYS, half), const2),
            pl.BlockSpec((PEER_KEYS, half), const2),
        ],
        out_specs=[pl.BlockSpec((tq, D_MODEL), lambda i: (i, 0)), kt_spec, kt_spec, kt_spec,
                   kt_spec, pl.BlockSpec((PEER_HEADS, tq), lambda i: (0, i))],
        out_shape=[jax.ShapeDtypeStruct((S, D_MODEL), MXU_DTYPE), kt, kt, kt, kt,
                   jax.ShapeDtypeStruct((PEER_HEADS, S), F32)],
        compiler_params=_cparams(("parallel",)),
    )(h2, gain.reshape(1, -1), w_q.astype(MXU_DTYPE), key1.astype(MXU_DTYPE),
      key2.astype(MXU_DTYPE))


def _peer_dense_kernel(xn_ref, u_ref, vT_ref, s1_ref, s2_ref, p1_ref, p2_ref, tau_ref, h_ref,
                       o_ref, acc_ref, x_ref, *, te, ne):
    j = pl.program_id(1)

    @pl.when(j == 0)
    def _():
        acc_ref[...] = jnp.zeros(acc_ref.shape, F32)

    nt = (((1,), (1,)), ((), ()))
    hid = lax.dot_general(u_ref[...], xn_ref[...], nt, preferred_element_type=F32)
    act = 0.5 * hid * (1.0 + lax.erf(hid * (2.0 ** -0.5)))
    tau = tau_ref[...]
    for el in range(te // PEER_KEYS):
        e1 = j * (te // PEER_KEYS) + el
        g = None
        for h in range(PEER_HEADS):
            s1r = s1_ref[h, pl.ds(e1, 1), :]
            p1r = p1_ref[h, pl.ds(e1, 1), :]
            keep = (s1r + s2_ref[h]) >= tau[h:h + 1, :]
            t = jnp.where(keep, p2_ref[h], 0.0) * p1r
            g = t if g is None else g + t
        sl = slice(el * PEER_KEYS, (el + 1) * PEER_KEYS)
        x_ref[sl, :] = (g * act[sl, :]).astype(x_ref.dtype)
    acc_ref[...] += jnp.dot(vT_ref[...], x_ref[...], preferred_element_type=F32)

    @pl.when(j == ne - 1)
    def _():
        o_ref[...] = h_ref[...] + acc_ref[...].T


def _peer_dense(h2, xn, u, v, s1, s2, p1, p2, tau, tm, te):
    S = h2.shape[0]
    NE = u.shape[0]
    ne = NE // te
    kt_spec = pl.BlockSpec((PEER_HEADS, PEER_KEYS, tm), lambda i, j: (0, 0, i))
    return pl.pallas_call(
        functools.partial(_peer_dense_kernel, te=te, ne=ne),
        grid=(S // tm, ne),
        in_specs=[
            pl.BlockSpec((tm, D_MODEL), lambda i, j: (i, 0)),
            pl.BlockSpec((te, D_MODEL), lambda i, j: (j, 0)),
            pl.BlockSpec((D_MODEL, te), lambda i, j: (0, j)),
            kt_spec, kt_spec, kt_spec, kt_spec,
            pl.BlockSpec((PEER_HEADS, tm), lambda i, j: (0, i)),
            pl.BlockSpec((tm, D_MODEL), lambda i, j: (i, 0)),
        ],
        out_specs=pl.BlockSpec((tm, D_MODEL), lambda i, j: (i, 0)),
        out_shape=jax.ShapeDtypeStruct((S, D_MODEL), F32),
        scratch_shapes=[pltpu.VMEM((D_MODEL, tm), F32), pltpu.VMEM((te, tm), MXU_DTYPE)],
        compiler_params=_cparams(("parallel", "arbitrary")),
    )(xn, u.astype(MXU_DTYPE), v.T.astype(MXU_DTYPE), s1, s2, p1, p2, tau, h2)


def _peer(h2, gain, w_q, key1, key2, u, v):
    S = h2.shape[0]
    xn, s1, s2, p1, p2, tau = _peer_route(h2, gain, w_q, key1, key2, min(256, S))
    return _peer_dense(h2, xn, u, v, s1, s2, p1, p2, tau, min(512, S), 1024)


def _layer(x, mem, positions, p):
    S = x.shape[0]
    proj = _norm_proj(x, p["mix_norm_gain"], _pack_w_in(p["w_in"]), min(1024, S), 768)
    qT, kh, vT, iqT, ik, iwT = _att_prep(proj, positions, p["att_q_gain"], p["att_k_gain"], 256)
    bias = _dsa_select(iqT, iwT, ik, 256, 512)
    y_att = _dsa_attend(qT, kh, vT, bias, 256, 1024)
    y_rwkv = _rwkv_time_mix(proj, p)
    kmT, vm = _mem_kv(mem, p["mem_norm_gain"], p["w_mem_kv"], p["mem_k_gain"])
    h2 = _merge(x, y_att, y_rwkv, proj, kmT, vm, p["mem_q_gain"], p["gate_bias"], p["w_up"],
                p["w_out"], 256)
    return _peer(h2, p["ffn_norm_gain"], p["peer_w_q"], p["peer_key1"], p["peer_key2"],
                 p["peer_u"], p["peer_v"])


_PARAM_NAMES = ("mix_norm_gain", "w_in", "gate_bias", "att_q_gain", "att_k_gain", "mem_norm_gain",
                "w_mem_kv", "mem_q_gain", "mem_k_gain", "rwkv_mu", "rwkv_w0", "rwkv_w2", "rwkv_a0",
                "rwkv_a2", "rwkv_g2", "rwkv_k_k", "rwkv_k_a", "rwkv_r_k", "rwkv_ln_w", "rwkv_ln_b",
                "w_up", "w_out", "ffn_norm_gain", "peer_w_q", "peer_key1", "peer_key2", "peer_u",
                "peer_v")


def kernel(x, mem, positions, mix_norm_gain, w_in, gate_bias, att_q_gain, att_k_gain,
           mem_norm_gain, w_mem_kv, mem_q_gain, mem_k_gain, rwkv_mu, rwkv_w0, rwkv_w2, rwkv_a0,
           rwkv_a2, rwkv_g2, rwkv_k_k, rwkv_k_a, rwkv_r_k, rwkv_ln_w, rwkv_ln_b, w_up, w_out,
           ffn_norm_gain, peer_w_q, peer_key1, peer_key2, peer_u, peer_v):
    params = (mix_norm_gain, w_in, gate_bias, att_q_gain, att_k_gain, mem_norm_gain, w_mem_kv,
              mem_q_gain, mem_k_gain, rwkv_mu, rwkv_w0, rwkv_w2, rwkv_a0, rwkv_a2, rwkv_g2,
              rwkv_k_k, rwkv_k_a, rwkv_r_k, rwkv_ln_w, rwkv_ln_b, w_up, w_out, ffn_norm_gain,
              peer_w_q, peer_key1, peer_key2, peer_u, peer_v)
    assert x.shape[0] == 1 and all(t.shape[0] == 1 for t in params)
    p = {name: t[0] for name, t in zip(_PARAM_NAMES, params)}
    return _layer(x[0], mem[0], positions[0], p)[None]
```

```python
import functools

import numpy as np
import jax
import jax.numpy as jnp
from jax import lax
from jax.experimental import pallas as pl
from jax.experimental.pallas import tpu as pltpu

F32 = jnp.float32
I32 = jnp.int32
MXU_DTYPE = jnp.bfloat16
HI = lax.Precision.HIGHEST

D_MODEL = 1024
ATT_HEADS, ATT_HD = 8, 64
ATT_W = ATT_HEADS * ATT_HD
IDX_HEADS, IDX_HD = 4, 64
TOPK_MAX = 256
RWKV_HEADS, RWKV_HD = 8, 64
RWKV_W = RWKV_HEADS * RWKV_HD
DECAY_LORA, AAA_LORA, GATE_LORA = 64, 64, 128
GN_EPS = 64e-5
MEM_HEADS, MEM_HD = 4, 128
MEM_W = MEM_HEADS * MEM_HD
N_BRANCH = 3
ROPE_THETA = 500000.0
ROPE_ROT = ATT_HD // 4
NORM_EPS = 1e-6
PEER_KEYS = 128
PEER_HEADS = 8
PEER_QDIM = 256
PEER_TOPK = 16

LOG2E = 1.4426950408889634
LANES = 128
INT_MIN = -(2 ** 31)
NEG_BIG = -1e30
VMEM_LIMIT = 56 * 1024 * 1024

COL_GATE = 0
COL_Q = 3072
COL_K = 3584
COL_V = 4096
COL_R = 4608
COL_RK = 5120
COL_RV = 5632
COL_MEMQ = 6144
COL_IQ = 6656
COL_LORA = 6912
COL_IKIW = 7168
PROJ_COLS = 7680


def _cparams(sem):
    return pltpu.CompilerParams(dimension_semantics=sem, vmem_limit_bytes=VMEM_LIMIT)


def _mm(a, b):
    return jnp.dot(a.astype(MXU_DTYPE), b.astype(MXU_DTYPE), preferred_element_type=F32)


def _mm_hi(a, b):
    return jnp.dot(a.astype(F32), b.astype(F32), preferred_element_type=F32, precision=HI)


def _normproj_kernel(x_ref, g_ref, w_ref, o_ref, xn_ref):
    @pl.when(pl.program_id(1) == 0)
    def _():
        x = x_ref[...]
        ms = jnp.mean(x * x, axis=-1, keepdims=True)
        xn_ref[...] = (x * lax.rsqrt(ms + NORM_EPS) * g_ref[...]).astype(xn_ref.dtype)

    o_ref[...] = jnp.dot(xn_ref[...], w_ref[...], preferred_element_type=F32)


def _norm_proj(x, gain, w, tm, tn):
    S, D = x.shape
    N = w.shape[1]
    return pl.pallas_call(
        _normproj_kernel,
        grid=(S // tm, N // tn),
        in_specs=[
            pl.BlockSpec((tm, D), lambda i, j: (i, 0)),
            pl.BlockSpec((1, D), lambda i, j: (0, 0)),
            pl.BlockSpec((D, tn), lambda i, j: (0, j)),
        ],
        out_specs=pl.BlockSpec((tm, tn), lambda i, j: (i, j)),
        out_shape=jax.ShapeDtypeStruct((S, N), F32),
        scratch_shapes=[pltpu.VMEM((tm, D), w.dtype)],
        compiler_params=_cparams(("parallel", "arbitrary")),
    )(x, gain.reshape(1, D), w)


def _pack_w_in(w_in):
    D = w_in.shape[0]
    o = 0
    parts = {}
    for name, width in (("q", ATT_W), ("k", ATT_W), ("v", ATT_W), ("iq", IDX_HEADS * IDX_HD),
                        ("ik", IDX_HD), ("iw", IDX_HEADS), ("r", RWKV_W), ("rk", RWKV_W),
                        ("rv", RWKV_W), ("lora", DECAY_LORA + AAA_LORA + GATE_LORA),
                        ("memq", MEM_W), ("gate", N_BRANCH * D_MODEL)):
        parts[name] = w_in[:, o:o + width]
        o += width
    ikiw = jnp.concatenate([parts["ik"], parts["iw"]], axis=1)
    ikiw = jnp.pad(ikiw, ((0, 0), (0, LANES - ikiw.shape[1])))
    packed = jnp.concatenate(
        [parts["gate"], parts["q"], parts["k"], parts["v"], parts["r"], parts["rk"], parts["rv"],
         parts["memq"], parts["iq"], parts["lora"], ikiw], axis=1)
    packed = jnp.pad(packed, ((0, 0), (0, PROJ_COLS - packed.shape[1])))
    return packed.astype(MXU_DTYPE)


def _rope_tables():
    d = np.arange(LANES) % ATT_HD
    inv_freq = 1.0 / (ROPE_THETA ** (np.arange(0, ROPE_ROT, 2, dtype=np.float32) / ROPE_ROT))
    half = ROPE_ROT // 2
    tab = np.zeros((8, LANES), np.float32)
    tab[0] = np.where(d < ROPE_ROT, inv_freq[d % half], 0.0)
    tab[1] = np.where(d < half, -1.0, 0.0)
    tab[2] = np.where((d >= half) & (d < ROPE_ROT), 1.0, 0.0)
    return jnp.asarray(tab)


def _seg_mean_matrix(width, seg):
    g = (np.arange(width)[:, None] // seg) == (np.arange(width)[None, :] // seg)
    return jnp.asarray(g.astype(np.float32) / seg)


def _rope(x, cos, sina, sinb):
    W = x.shape[1]
    reps = W // LANES
    if reps > 1:
        cos = jnp.concatenate([cos] * reps, axis=1)
        sina = jnp.concatenate([sina] * reps, axis=1)
        sinb = jnp.concatenate([sinb] * reps, axis=1)
    half = ROPE_ROT // 2
    up = pltpu.roll(x, W - half, axis=1)
    down = pltpu.roll(x, half, axis=1)
    return x * cos + up * sina + down * sinb


def _att_prep_kernel(q_ref, k_ref, v_ref, iq_ref, ikiw_ref, pos_ref, tab_ref, seg_ref,
                     qg_ref, kg_ref, qT_ref, kh_ref, vT_ref, iqT_ref, ik_ref, iwT_ref):
    tab = tab_ref[...]
    ang = pos_ref[...].astype(F32) * tab[0:1, :]
    cos = jnp.cos(ang)
    sin = jnp.sin(ang)
    sina = sin * tab[1:2, :]
    sinb = sin * tab[2:3, :]
    seg = seg_ref[...]

    def head_norm(x, g):
        ms = _mm_hi(x * x, seg)
        return x * lax.rsqrt(ms + NORM_EPS) * g

    q = _rope(head_norm(q_ref[...], qg_ref[...]), cos, sina, sinb) * (ATT_HD ** -0.5 * LOG2E)
    k = _rope(head_norm(k_ref[...], kg_ref[...]), cos, sina, sinb)
    v = v_ref[...]
    iq = _rope(iq_ref[...], cos, sina, sinb)
    ik = _rope(ikiw_ref[...], cos, sina, sinb)
    for h in range(ATT_HEADS):
        kh_ref[h] = k[:, h * ATT_HD:(h + 1) * ATT_HD].astype(kh_ref.dtype)
    qT_ref[...] = q.T.astype(qT_ref.dtype)
    vT_ref[...] = v.T.astype(vT_ref.dtype)
    iqT_ref[...] = iq.T.astype(iqT_ref.dtype)
    ik_ref[...] = ik[:, :IDX_HD].astype(ik_ref.dtype)
    idx_scale = (IDX_HEADS ** -0.5) * (IDX_HD ** -0.5)
    iwT_ref[...] = ikiw_ref[...].T[IDX_HD:IDX_HD + 8, :] * idx_scale


def _att_prep(proj, positions, att_q_gain, att_k_gain, tq):
    S = proj.shape[0]
    qg = jnp.tile(att_q_gain.reshape(1, ATT_HD), (1, ATT_HEADS))
    kg = jnp.tile(att_k_gain.reshape(1, ATT_HD), (1, ATT_HEADS))
    col = lambda off, w: (lambda i: (i, off // w))
    const = lambda i: (0, 0)
    return pl.pallas_call(
        _att_prep_kernel,
        grid=(S // tq,),
        in_specs=[
            pl.BlockSpec((tq, ATT_W), col(COL_Q, ATT_W)),
            pl.BlockSpec((tq, ATT_W), col(COL_K, ATT_W)),
            pl.BlockSpec((tq, ATT_W), col(COL_V, ATT_W)),
            pl.BlockSpec((tq, IDX_HEADS * IDX_HD), col(COL_IQ, IDX_HEADS * IDX_HD)),
            pl.BlockSpec((tq, LANES), col(COL_IKIW, LANES)),
            pl.BlockSpec((tq, 1), lambda i: (i, 0)),
            pl.BlockSpec((8, LANES), const),
            pl.BlockSpec((ATT_W, ATT_W), const),
            pl.BlockSpec((1, ATT_W), const),
            pl.BlockSpec((1, ATT_W), const),
        ],
        out_specs=[
            pl.BlockSpec((ATT_W, tq), lambda i: (0, i)),
            pl.BlockSpec((ATT_HEADS, tq, ATT_HD), lambda i: (0, i, 0)),
            pl.BlockSpec((ATT_W, tq), lambda i: (0, i)),
            pl.BlockSpec((IDX_HEADS * IDX_HD, tq), lambda i: (0, i)),
            pl.BlockSpec((tq, IDX_HD), lambda i: (i, 0)),
            pl.BlockSpec((8, tq), lambda i: (0, i)),
        ],
        out_shape=[
            jax.ShapeDtypeStruct((ATT_W, S), MXU_DTYPE),
            jax.ShapeDtypeStruct((ATT_HEADS, S, ATT_HD), MXU_DTYPE),
            jax.ShapeDtypeStruct((ATT_W, S), MXU_DTYPE),
            jax.ShapeDtypeStruct((IDX_HEADS * IDX_HD, S), MXU_DTYPE),
            jax.ShapeDtypeStruct((S, IDX_HD), MXU_DTYPE),
            jax.ShapeDtypeStruct((8, S), F32),
        ],
        compiler_params=_cparams(("parallel",)),
    )(proj, proj, proj, proj, proj, positions.reshape(S, 1), _rope_tables(),
      _seg_mean_matrix(ATT_W, ATT_HD), qg, kg)


KEY_NEG_INF = -2139095041


def _key_to_float(key):
    f = pltpu.bitcast(key ^ ((key >> 31) & 0x7FFFFFFF), F32)
    return jnp.where(key < KEY_NEG_INF, -jnp.inf, f)


SUB_ACC = 64


def _sel_kernel(iqT_ref, iwT_ref, ik_ref, tri_ref, bias_ref, sc_ref, *, tq, tk, seq, nsel):
    i = pl.program_id(0)
    q0 = i * tq
    nc = (q0 + tq + tk - 1) // tk
    qidx = q0 + lax.broadcasted_iota(I32, (tk, tq), 1)
    krow = lax.broadcasted_iota(I32, (tk, tq), 0)
    iw = iwT_ref[...]

    def score_chunk(c, carry):
        c0 = pl.multiple_of(c * tk, tk)
        ikc = ik_ref[pl.ds(c0, tk), :]
        sc = None
        for h in range(IDX_HEADS):
            logit = jnp.dot(ikc, iqT_ref[h * IDX_HD:(h + 1) * IDX_HD, :],
                            preferred_element_type=F32)
            t = jnp.maximum(logit, 0.0) * iw[h:h + 1, :]
            sc = t if sc is None else sc + t
        sc_ref[pl.ds(c0, tk), :] = jnp.where(c0 + krow <= qidx, sc, -jnp.inf)
        return carry

    lax.fori_loop(0, nc, score_chunk, 0)

    def count(pred, n_out=1):
        def body(c, accs):
            c0 = pl.multiple_of(c * tk, tk)
            ms = pred(sc_ref[pl.ds(c0, tk), :])
            return tuple(a + jnp.sum(m.reshape(tk // SUB_ACC, SUB_ACC, tq), axis=0)
                         for a, m in zip(accs, ms))
        accs = lax.fori_loop(0, nc, body, (jnp.zeros((SUB_ACC, tq), F32),) * n_out)
        return [_col_reduce(a, jnp.add) for a in accs]

    def count_ge(cand):
        return count(lambda s: (jnp.where(s >= cand, 1.0, 0.0),))[0]

    zero = jnp.zeros((1, tq), I32)
    n_nonneg = count_ge(0.0)
    n_pos = count(lambda s: (jnp.where(s > 0.0, 1.0, 0.0),))[0]
    nonneg = n_nonneg >= nsel
    prefix = jnp.where(nonneg, zero, zero + INT_MIN)
    n_pref = jnp.where(nonneg, n_nonneg, (nc * tk).astype(F32))
    zero_kth = jnp.where(nonneg, jnp.where(n_pos < nsel, 1.0, 0.0), 0.0)

    def bit_body(b, carry):
        prefix, n_pref = carry
        settled = jnp.where(n_pref == nsel, 1.0, zero_kth)
        cand = prefix | jnp.left_shift(jnp.int32(1), 30 - b)
        cnt = lax.cond(jnp.min(settled) > 0.5,
                       lambda: jnp.zeros((1, tq), F32),
                       lambda: count_ge(_key_to_float(cand)))
        take = jnp.where(cnt >= nsel, 1.0 - zero_kth, 0.0) > 0.5
        return jnp.where(take, cand, prefix), jnp.where(take, cnt, n_pref)

    tau = _key_to_float(lax.fori_loop(0, 31, bit_body, (prefix, n_pref))[0])

    def is_neg(s):
        return pltpu.bitcast(s, I32) < 0

    def tie_classes(s):
        eq = s == tau
        neg = is_neg(s)
        return (jnp.where(eq, jnp.where(neg, 0.0, 1.0), 0.0),
                jnp.where(eq, jnp.where(neg, 1.0, 0.0), 0.0))

    n_gt, n_eq_pos, n_eq_neg = count(
        lambda s: (jnp.where(s > tau, 1.0, 0.0),) + tie_classes(s), n_out=3)
    need = nsel - n_gt
    mixed = jnp.max(jnp.minimum(n_eq_pos, n_eq_neg)) > 0
    tri = tri_ref[...]

    def write_pass(both_signs):
        def body(c, carry):
            c0 = pl.multiple_of(c * tk, tk)
            s = sc_ref[pl.ds(c0, tk), :]
            if both_signs:
                ind_pos, ind_neg = tie_classes(s)
                cum_pos = carry[0] + _mm(tri, ind_pos)
                cum_neg = carry[1] + _mm(tri, ind_neg)
                rank = jnp.where(is_neg(s), n_eq_pos + cum_neg, cum_pos)
                carry = (cum_pos[tk - 1:tk, :], cum_neg[tk - 1:tk, :])
            else:
                rank = carry[0] + _mm(tri, jnp.where(s == tau, 1.0, 0.0))
                carry = (rank[tk - 1:tk, :], carry[1])
            tie = jnp.where(rank <= need, 0.0, NEG_BIG)
            b = jnp.where(s > tau, 0.0, jnp.where(s == tau, tie, NEG_BIG))
            b = jnp.where(c0 + krow <= qidx, b, NEG_BIG)
            bias_ref[pl.ds(c0, tk), :] = b.astype(bias_ref.dtype)
            return carry
        zf = jnp.zeros((1, tq), F32)
        lax.fori_loop(0, nc, body, (zf, zf))

    lax.cond(mixed, lambda: write_pass(True), lambda: write_pass(False))

    def fill_chunk(c, carry):
        c0 = pl.multiple_of(c * tk, tk)
        bias_ref[pl.ds(c0, tk), :] = jnp.full((tk, tq), NEG_BIG, bias_ref.dtype)
        return carry

    lax.fori_loop(nc, seq // tk, fill_chunk, 0)


def _dsa_select(iqT, iwT, ik, tq, tk):
    S = ik.shape[0]
    nsel = min(TOPK_MAX, S // 4)
    kern = functools.partial(_sel_kernel, tq=tq, tk=tk, seq=S, nsel=nsel)
    return pl.pallas_call(
        kern,
        grid=(S // tq,),
        in_specs=[
            pl.BlockSpec((IDX_HEADS * IDX_HD, tq), lambda i: (0, i)),
            pl.BlockSpec((8, tq), lambda i: (0, i)),
            pl.BlockSpec((S, IDX_HD), lambda i: (0, 0)),
            pl.BlockSpec((tk, tk), lambda i: (0, 0)),
        ],
        out_specs=pl.BlockSpec((S, tq), lambda i: (0, i)),
        out_shape=jax.ShapeDtypeStruct((S, S), jnp.bfloat16),
        scratch_shapes=[pltpu.VMEM((S, tq), F32)],
        compiler_params=_cparams(("parallel",)),
    )(iqT, iwT, ik, jnp.tril(jnp.ones((tk, tk), MXU_DTYPE)))


def _col_reduce(x, op):
    rows = x.shape[0]
    while rows > 8:
        rows //= 2
        x = op(x[:rows], x[rows:])
    red = jnp.max if op is jnp.maximum else jnp.sum
    return red(x, axis=0, keepdims=True)


def _att_kernel(qi_ref, kj_ref, qT_ref, k_ref, vT_ref, bias_ref, o_ref, m_ref, l_ref, acc_ref,
                *, tq, tk):
    t = pl.program_id(0)
    i = qi_ref[t]
    j = kj_ref[t]
    last = ((i + 1) * tq - 1) // tk

    @pl.when(j == 0)
    def _():
        m_ref[...] = jnp.full(m_ref.shape, -jnp.inf, F32)
        l_ref[...] = jnp.zeros(l_ref.shape, F32)
        acc_ref[...] = jnp.zeros(acc_ref.shape, F32)

    bias = bias_ref[...].astype(F32)
    hs = lambda h: slice(h * ATT_HD, (h + 1) * ATT_HD)
    m_all = m_ref[...]
    l_all = l_ref[...]
    m_out, l_out = [], []
    halves = (slice(0, tk // 2), slice(tk // 2, tk))
    for h in range(ATT_HEADS):
        qh = qT_ref[hs(h), :]
        ss = [jnp.dot(k_ref[h, ks, :], qh, preferred_element_type=F32) + bias[ks, :]
              for ks in halves]
        m_prev = m_all[h:h + 1, :]
        m_new = jnp.maximum(m_prev, jnp.maximum(_col_reduce(ss[0], jnp.maximum),
                                                _col_reduce(ss[1], jnp.maximum)))
        alpha = jnp.exp2(m_prev - m_new)
        ps = [jnp.exp2(s - m_new) for s in ss]
        l_out.append(alpha * l_all[h:h + 1, :] + _col_reduce(ps[0], jnp.add)
                     + _col_reduce(ps[1], jnp.add))
        m_out.append(m_new)
        pv = [jnp.dot(vT_ref[hs(h), ks], p.astype(vT_ref.dtype), preferred_element_type=F32)
              for ks, p in zip(halves, ps)]
        acc_ref[hs(h), :] = alpha * acc_ref[hs(h), :] + pv[0] + pv[1]
    m_ref[...] = jnp.concatenate(m_out, axis=0)
    l_ref[...] = jnp.concatenate(l_out, axis=0)

    @pl.when(j == last)
    def _():
        out = jnp.concatenate(
            [acc_ref[h * ATT_HD:(h + 1) * ATT_HD, :] / l_ref[h:h + 1, :]
             for h in range(ATT_HEADS)], axis=0)
        o_ref[...] = out.T


def _dsa_attend(qT, kh, vT, bias, tq, tk):
    S = qT.shape[1]
    pairs = [(i, j) for i in range(S // tq) for j in range(((i + 1) * tq - 1) // tk + 1)]
    qi = jnp.asarray([p[0] for p in pairs], I32)
    kj = jnp.asarray([p[1] for p in pairs], I32)
    grid_spec = pltpu.PrefetchScalarGridSpec(
        num_scalar_prefetch=2,
        grid=(len(pairs),),
        in_specs=[
            pl.BlockSpec((ATT_W, tq), lambda t, qi, kj: (0, qi[t])),
            pl.BlockSpec((ATT_HEADS, tk, ATT_HD), lambda t, qi, kj: (0, kj[t], 0)),
            pl.BlockSpec((ATT_W, tk), lambda t, qi, kj: (0, kj[t])),
            pl.BlockSpec((tk, tq), lambda t, qi, kj: (kj[t], qi[t])),
        ],
        out_specs=pl.BlockSpec((tq, ATT_W), lambda t, qi, kj: (qi[t], 0)),
        scratch_shapes=[
            pltpu.VMEM((ATT_HEADS, tq), F32),
            pltpu.VMEM((ATT_HEADS, tq), F32),
            pltpu.VMEM((ATT_W, tq), F32),
        ],
    )
    return pl.pallas_call(
        functools.partial(_att_kernel, tq=tq, tk=tk),
        grid_spec=grid_spec,
        out_shape=jax.ShapeDtypeStruct((S, ATT_W), F32),
        compiler_params=_cparams(("arbitrary",)),
    )(qi, kj, qT, kh, vT, bias)


CHUNK = 64


def _shift_rows(x, prev8, first):
    prev_row = jnp.where(first, 0.0, prev8[7:8, :])
    row = lax.broadcasted_iota(I32, x.shape, 0)
    return jnp.where(row == 0, prev_row, pltpu.roll(x, 1, axis=0))


def _softplus(z):
    return jnp.maximum(z, 0.0) + jnp.log1p(jnp.exp(-jnp.abs(z)))


def _rwkv_prep_kernel(r_ref, k_ref, v_ref, lo_ref, rp_ref, kp_ref, vp_ref, lop_ref,
                      mur_ref, muk_ref, muv_ref, mulo_ref, w0_ref, w2_ref, a0_ref, a2_ref,
                      g2_ref, kk_ref, ka_ref, rk_ref, seg_ref, tri_ref, ones_ref, end_ref,
                      rt_ref, kt_ref, bt_ref, at_ref, kh_ref, bh_ref, vh_ref, gam_ref,
                      bonus_ref, g_ref):
    first = pl.program_id(0) == 0

    def mix(x_ref, p_ref, mu_ref):
        x = x_ref[...]
        return x + (_shift_rows(x, p_ref[...], first) - x) * mu_ref[...]

    r = mix(r_ref, rp_ref, mur_ref)
    k = mix(k_ref, kp_ref, muk_ref)
    v = mix(v_ref, vp_ref, muv_ref)
    lo = mix(lo_ref, lop_ref, mulo_ref)
    wd = lo[:, :DECAY_LORA]
    ad = lo[:, DECAY_LORA:DECAY_LORA + AAA_LORA]
    gd = lo[:, DECAY_LORA + AAA_LORA:]
    w_log = -_softplus(-(w0_ref[...] + _mm(jnp.tanh(wd), w2_ref[...]))) - 0.5
    lw = -jnp.exp(w_log)
    a = jax.nn.sigmoid(a0_ref[...] + _mm(ad, a2_ref[...]))
    g_ref[...] = _mm(jax.nn.sigmoid(gd), g2_ref[...])
    seg = seg_ref[...]
    kk = k * kk_ref[...]
    kk = kk / jnp.maximum(jnp.sqrt(_mm_hi(kk * kk, seg)), 1e-12)
    k = k * (1.0 + (a - 1.0) * ka_ref[...])
    bonus_ref[...] = _mm_hi(r * k * rk_ref[...], seg) * v
    avec = -kk
    bvec = kk * a
    cs = _mm_hi(tri_ref[...], lw)
    cs_end = _mm_hi(ones_ref[...], lw)
    e_neg = jnp.exp(-cs)
    e_end = jnp.exp(cs_end - cs)
    outs = ((rt_ref, r * jnp.exp(cs)), (kt_ref, k * e_neg), (bt_ref, bvec * e_neg),
            (at_ref, avec * jnp.exp(cs - lw)), (kh_ref, k * e_end), (bh_ref, bvec * e_end),
            (vh_ref, v), (gam_ref, jnp.exp(_mm_hi(end_ref[...], lw))))
    for ref, val in outs:
        for h in range(RWKV_HEADS):
            ref[h] = val[:, h * RWKV_HD:(h + 1) * RWKV_HD]


def _rwkv_prep(proj, p, tq):
    S = proj.shape[0]
    nch = tq // CHUNK
    col = lambda off, w: (lambda i: (i, off // w))
    prev = lambda off, w: (lambda i: (jnp.maximum(i * (tq // 8) - 1, 0), off // w))
    const = lambda i: (0, 0)
    lw_ = DECAY_LORA + AAA_LORA + GATE_LORA
    mu = p["rwkv_mu"]
    row = lambda t: t.reshape(1, -1)
    t_idx = np.arange(tq)
    same = (t_idx[:, None] // CHUNK) == (t_idx[None, :] // CHUNK)
    tri = jnp.asarray((same & (t_idx[None, :] <= t_idx[:, None])).astype(np.float32))
    ones = jnp.asarray(same.astype(np.float32))
    end = jnp.asarray(((t_idx[None, :] // CHUNK) == np.arange(nch)[:, None]).astype(np.float32))
    seg = _seg_mean_matrix(RWKV_W, RWKV_HD) * RWKV_HD
    hm = jax.ShapeDtypeStruct((RWKV_HEADS, S, RWKV_HD), F32)
    hm_spec = pl.BlockSpec((RWKV_HEADS, tq, RWKV_HD), lambda i: (0, i, 0))
    wide = jax.ShapeDtypeStruct((S, RWKV_W), F32)
    wide_spec = pl.BlockSpec((tq, RWKV_W), lambda i: (i, 0))
    vec = lambda w: pl.BlockSpec((1, w), const)
    return pl.pallas_call(
        _rwkv_prep_kernel,
        grid=(S // tq,),
        in_specs=[
            pl.BlockSpec((tq, RWKV_W), col(COL_R, RWKV_W)),
            pl.BlockSpec((tq, RWKV_W), col(COL_RK, RWKV_W)),
            pl.BlockSpec((tq, RWKV_W), col(COL_RV, RWKV_W)),
            pl.BlockSpec((tq, lw_), col(COL_LORA, lw_)),
            pl.BlockSpec((8, RWKV_W), prev(COL_R, RWKV_W)),
            pl.BlockSpec((8, RWKV_W), prev(COL_RK, RWKV_W)),
            pl.BlockSpec((8, RWKV_W), prev(COL_RV, RWKV_W)),
            pl.BlockSpec((8, lw_), prev(COL_LORA, lw_)),
            vec(RWKV_W), vec(RWKV_W), vec(RWKV_W), vec(lw_),
            vec(RWKV_W), pl.BlockSpec((DECAY_LORA, RWKV_W), const),
            vec(RWKV_W), pl.BlockSpec((AAA_LORA, RWKV_W), const),
            pl.BlockSpec((GATE_LORA, RWKV_W), const),
            vec(RWKV_W), vec(RWKV_W), vec(RWKV_W),
            pl.BlockSpec((RWKV_W, RWKV_W), const),
            pl.BlockSpec((tq, tq), const), pl.BlockSpec((tq, tq), const),
            pl.BlockSpec((nch, tq), const),
        ],
        out_specs=[hm_spec] * 7 + [pl.BlockSpec((RWKV_HEADS, nch, RWKV_HD), lambda i: (0, i, 0)),
                                   wide_spec, wide_spec],
        out_shape=[hm] * 7 + [jax.ShapeDtypeStruct((RWKV_HEADS, S // CHUNK, RWKV_HD), F32),
                              wide, wide],
        compiler_params=_cparams(("parallel",)),
    )(proj, proj, proj, proj, proj, proj, proj, proj,
      row(mu[:RWKV_W]), row(mu[RWKV_W:2 * RWKV_W]), row(mu[2 * RWKV_W:3 * RWKV_W]),
      row(mu[3 * RWKV_W:]), row(p["rwkv_w0"]), p["rwkv_w2"], row(p["rwkv_a0"]), p["rwkv_a2"],
      p["rwkv_g2"], row(p["rwkv_k_k"]), row(p["rwkv_k_a"]), row(p["rwkv_r_k"]), seg, tri, ones, end)


def _bmm(a, b, dims):
    return jnp.einsum(dims, a, b, preferred_element_type=F32, precision=HI)


def _bmm1(a, b, dims):
    return jnp.einsum(dims, a.astype(MXU_DTYPE), b.astype(MXU_DTYPE), preferred_element_type=F32)


def _rwkv_chunk_kernel(rt_ref, kt_ref, bt_ref, at_ref, kh_ref, bh_ref, v_ref, gam_ref,
                       p_ref, q_ref, rw_ref, y0_ref, *, nch):
    L, N = CHUNK, RWKV_HD
    ri = lax.broadcasted_iota(I32, (nch, L, L), 1)
    ci = lax.broadcasted_iota(I32, (nch, L, L), 2)
    eye_n = (lax.broadcasted_iota(I32, (nch, N, N), 1)
             == lax.broadcasted_iota(I32, (nch, N, N), 2)).astype(F32)
    for h in range(RWKV_HEADS):
        ld = lambda ref: ref[h].reshape(nch, L, N)
        rt, kt, bt, at, kh, bh, v = (ld(x) for x in (rt_ref, kt_ref, bt_ref, at_ref, kh_ref,
                                                      bh_ref, v_ref))
        gam = gam_ref[h].reshape(nch, 1, N)
        mm = _bmm1
        a_ab = jnp.where(ci < ri, mm(at, bt, "cld,cmd->clm"), 0.0)
        a_ak = jnp.where(ci < ri, mm(at, kt, "cld,cmd->clm"), 0.0)
        m_rk = jnp.where(ci <= ri, mm(rt, kt, "cld,cmd->clm"), 0.0)
        m_rb = jnp.where(ci <= ri, mm(rt, bt, "cld,cmd->clm"), 0.0)
        rhs = jnp.concatenate([at, mm(a_ak, v, "clm,cmd->cld")], axis=2)
        pw = a_ab
        step = 1
        while True:
            rhs = rhs + mm(pw, rhs, "clm,cmd->cld")
            step *= 2
            if step >= L:
                break
            pw = mm(pw, pw, "clm,cmn->cln")
        w, u0 = rhs[:, :, :N], rhs[:, :, N:]
        p_ref[:, h] = eye_n * gam + mm(bh, w, "cld,cle->cde")
        q_ref[:, h] = mm(kh, v, "cld,cle->cde") + mm(bh, u0, "cld,cle->cde")
        rw_ref[h] = (rt + mm(m_rb, w, "clm,cmd->cld")).reshape(nch * L, N)
        y0_ref[h] = (mm(m_rk, v, "clm,cmd->cld") + mm(m_rb, u0, "clm,cmd->cld")).reshape(nch * L, N)


def _rwkv_chunks(rt, kt, bt, at, kh, bh, vh, gam, tt):
    S = rt.shape[1]
    nch = tt // CHUNK
    hm_spec = pl.BlockSpec((RWKV_HEADS, tt, RWKV_HD), lambda i: (0, i, 0))
    hm = jax.ShapeDtypeStruct((RWKV_HEADS, S, RWKV_HD), F32)
    sq_spec = pl.BlockSpec((nch, RWKV_HEADS, RWKV_HD, RWKV_HD), lambda i: (i, 0, 0, 0))
    sq = jax.ShapeDtypeStruct((S // CHUNK, RWKV_HEADS, RWKV_HD, RWKV_HD), F32)
    return pl.pallas_call(
        functools.partial(_rwkv_chunk_kernel, nch=nch),
        grid=(S // tt,),
        in_specs=[hm_spec] * 7 + [pl.BlockSpec((RWKV_HEADS, nch, RWKV_HD), lambda i: (0, i, 0))],
        out_specs=[sq_spec, sq_spec, hm_spec, hm_spec],
        out_shape=[sq, sq, hm, hm],
        compiler_params=_cparams(("parallel",)),
    )(rt, kt, bt, at, kh, bh, vh, gam)


def _rwkv_scan_kernel(p_ref, q_ref, rw_ref, y0_ref, bonus_ref, g_ref, seg_ref, lnw_ref, lnb_ref,
                      o_ref, st_ref, *, nch):
    @pl.when(pl.program_id(0) == 0)
    def _():
        st_ref[...] = jnp.zeros(st_ref.shape, F32)

    L = CHUNK
    st = st_ref[...]
    ys = []
    for c in range(nch):
        rw = rw_ref[:, c * L:(c + 1) * L, :]
        ys.append(_bmm(rw, st, "hld,hde->hle") + y0_ref[:, c * L:(c + 1) * L, :])
        st = _bmm(p_ref[c], st, "hjk,hki->hji") + q_ref[c]
    st_ref[...] = st
    y = jnp.concatenate(ys, axis=1)
    y = jnp.concatenate([y[h] for h in range(RWKV_HEADS)], axis=1)
    seg = seg_ref[...]
    mean = _mm_hi(y, seg)
    d = y - mean
    var = _mm_hi(d * d, seg)
    y = d * lax.rsqrt(var + GN_EPS) * lnw_ref[...] + lnb_ref[...]
    o_ref[...] = (y + bonus_ref[...]) * g_ref[...]


def _rwkv_scan(pm, qm, rw, y0, bonus, g, ln_w, ln_b, tt):
    S = rw.shape[1]
    nch = tt // CHUNK
    hm_spec = pl.BlockSpec((RWKV_HEADS, tt, RWKV_HD), lambda i: (0, i, 0))
    sq_spec = pl.BlockSpec((nch, RWKV_HEADS, RWKV_HD, RWKV_HD), lambda i: (i, 0, 0, 0))
    wide_spec = pl.BlockSpec((tt, RWKV_W), lambda i: (i, 0))
    const = lambda i: (0, 0)
    return pl.pallas_call(
        functools.partial(_rwkv_scan_kernel, nch=nch),
        grid=(S // tt,),
        in_specs=[sq_spec, sq_spec, hm_spec, hm_spec, wide_spec, wide_spec,
                  pl.BlockSpec((RWKV_W, RWKV_W), const), pl.BlockSpec((1, RWKV_W), const),
                  pl.BlockSpec((1, RWKV_W), const)],
        out_specs=wide_spec,
        out_shape=jax.ShapeDtypeStruct((S, RWKV_W), F32),
        scratch_shapes=[pltpu.VMEM((RWKV_HEADS, RWKV_HD, RWKV_HD), F32)],
        compiler_params=_cparams(("arbitrary",)),
    )(pm, qm, rw, y0, bonus, g, _seg_mean_matrix(RWKV_W, RWKV_HD), ln_w.reshape(1, -1),
      ln_b.reshape(1, -1))


def _rwkv_time_mix(proj, p):
    outs = _rwkv_prep(proj, p, 512)
    rt, kt, bt, at, kh, bh, vh, gam, bonus, g = outs
    pm, qm, rw, y0 = _rwkv_chunks(rt, kt, bt, at, kh, bh, vh, gam, 512)
    return _rwkv_scan(pm, qm, rw, y0, bonus, g, p["rwkv_ln_w"], p["rwkv_ln_b"], 512)


def _lane_rmsnorm(x, gain):
    return x * lax.rsqrt(jnp.mean(x * x, axis=-1, keepdims=True) + NORM_EPS) * gain


def _mem_kv_kernel(mem_ref, g_ref, w_ref, kg_ref, kmT_ref, vm_ref):
    m = _lane_rmsnorm(mem_ref[...], g_ref[...])
    kv = _mm(m, w_ref[...])
    km = jnp.concatenate(
        [_lane_rmsnorm(kv[:, h * MEM_HD:(h + 1) * MEM_HD], kg_ref[...]) for h in range(MEM_HEADS)],
        axis=1)
    kmT_ref[...] = km.T.astype(kmT_ref.dtype)
    vm_ref[...] = kv[:, MEM_W:].astype(vm_ref.dtype)


def _mem_kv(mem, gain, w_kv, k_gain):
    M = mem.shape[0]
    return pl.pallas_call(
        _mem_kv_kernel,
        out_shape=[jax.ShapeDtypeStruct((MEM_W, M), MXU_DTYPE),
                   jax.ShapeDtypeStruct((M, MEM_W), MXU_DTYPE)],
        compiler_params=pltpu.CompilerParams(vmem_limit_bytes=VMEM_LIMIT),
    )(mem, gain.reshape(1, -1), w_kv.astype(MXU_DTYPE), k_gain.reshape(1, -1))


def _merge_kernel(x_ref, yatt_ref, yrwkv_ref, memq_ref, gate_ref, kmT_ref, vm_ref, qg_ref,
                  gb_ref, wup_ref, wout_ref, o_ref):
    q = memq_ref[...]
    heads = []
    for h in range(MEM_HEADS):
        sl = slice(h * MEM_HD, (h + 1) * MEM_HD)
        qn = _lane_rmsnorm(q[:, sl], qg_ref[...])
        s = _mm(qn, kmT_ref[sl, :]) * (MEM_HD ** -0.5)
        s = s - jnp.max(s, axis=-1, keepdims=True)
        e = jnp.exp(s)
        p = e / jnp.sum(e, axis=-1, keepdims=True)
        heads.append(_mm(p, vm_ref[:, sl]))
    ymem = jnp.concatenate(heads, axis=1)
    merged = None
    for c, y in enumerate((yatt_ref[...], yrwkv_ref[...], ymem)):
        up = _mm(y, wup_ref[c])
        gate = jax.nn.sigmoid(gate_ref[:, c * D_MODEL:(c + 1) * D_MODEL] + gb_ref[c:c + 1, :])
        merged = gate * up if merged is None else merged + gate * up
    o_ref[...] = x_ref[...] + _mm(merged, wout_ref[...])


def _merge(x, y_att, y_rwkv, proj, kmT, vm, mem_q_gain, gate_bias, w_up, w_out, tq):
    S = x.shape[0]
    M = vm.shape[0]
    gw = N_BRANCH * D_MODEL
    const2 = lambda i: (0, 0)
    return pl.pallas_call(
        _merge_kernel,
        grid=(S // tq,),
        in_specs=[
            pl.BlockSpec((tq, D_MODEL), lambda i: (i, 0)),
            pl.BlockSpec((tq, ATT_W), lambda i: (i, 0)),
            pl.BlockSpec((tq, RWKV_W), lambda i: (i, 0)),
            pl.BlockSpec((tq, MEM_W), lambda i: (i, COL_MEMQ // MEM_W)),
            pl.BlockSpec((tq, gw), lambda i: (i, COL_GATE // gw)),
            pl.BlockSpec((MEM_W, M), const2),
            pl.BlockSpec((M, MEM_W), const2),
            pl.BlockSpec((1, MEM_HD), const2),
            pl.BlockSpec((N_BRANCH, D_MODEL), const2),
            pl.BlockSpec((N_BRANCH, ATT_W, D_MODEL), lambda i: (0, 0, 0)),
            pl.BlockSpec((D_MODEL, D_MODEL), const2),
        ],
        out_specs=pl.BlockSpec((tq, D_MODEL), lambda i: (i, 0)),
        out_shape=jax.ShapeDtypeStruct((S, D_MODEL), F32),
        compiler_params=_cparams(("parallel",)),
    )(x, y_att, y_rwkv, proj, proj, kmT, vm, mem_q_gain.reshape(1, -1), gate_bias,
      w_up.astype(MXU_DTYPE), w_out.astype(MXU_DTYPE))


NO_RANK = 255.0


def _pop_max(cur, rows):
    m = jnp.max(cur, axis=0, keepdims=True)
    first = jnp.min(jnp.where(cur == m, rows, cur.shape[0]), axis=0, keepdims=True)
    hit = rows == first
    return m, hit, jnp.where(hit, -jnp.inf, cur)


def _peer_route_kernel(h_ref, g_ref, wq_ref, k1_ref, k2_ref,
                       xn_ref, bsel_ref, r2_ref, p1_ref, p2_ref):
    tq = h_ref.shape[0]
    xn = _lane_rmsnorm(h_ref[...], g_ref[...]).astype(xn_ref.dtype)
    xn_ref[...] = xn
    qp = jnp.dot(xn, wq_ref[...], preferred_element_type=F32)
    nt = (((1,), (1,)), ((), ()))
    half = PEER_QDIM // 2
    rows = lax.broadcasted_iota(I32, (PEER_KEYS, tq), 0)
    pairs = [(a, b) for a in range(PEER_TOPK) for b in range(PEER_TOPK)
             if (a + 1) * (b + 1) <= PEER_TOPK]
    npad = -len(pairs) % 8
    crow = lax.broadcasted_iota(I32, (len(pairs) + npad, tq), 0)
    for h in range(PEER_HEADS):
        q1 = qp[:, h * PEER_QDIM:h * PEER_QDIM + half].astype(MXU_DTYPE)
        q2 = qp[:, h * PEER_QDIM + half:(h + 1) * PEER_QDIM].astype(MXU_DTYPE)
        s1 = lax.dot_general(k1_ref[...], q1, nt, preferred_element_type=F32)
        s2 = lax.dot_general(k2_ref[...], q2, nt, preferred_element_type=F32)
        tops, ranks = [], []
        for s in (s1, s2):
            cur, vals = s, []
            rank = jnp.full((PEER_KEYS, tq), NO_RANK, F32)
            for r in range(PEER_TOPK):
                m, hit, cur = _pop_max(cur, rows)
                rank = jnp.where(hit, float(r), rank)
                vals.append(m)
            tops.append(vals)
            ranks.append(rank)
        v1, v2 = tops
        rank1, rank2 = ranks
        cand = jnp.concatenate([v1[a] + v2[b] for a, b in pairs]
                               + [jnp.full((npad, tq), -jnp.inf, F32)], axis=0)
        top = v1[0] + v2[0]
        z = jnp.zeros((1, tq), F32)
        for _ in range(PEER_TOPK):
            tau, _, cand = _pop_max(cand, crow)
            z = z + jnp.exp(tau - top)
        bsel = jnp.zeros((PEER_KEYS, tq), F32)
        for a in range(PEER_TOPK):
            n_b = jnp.zeros((1, tq), F32)
            for b in [pb for pa, pb in pairs if pa == a]:
                n_b = n_b + jnp.where(v1[a] + v2[b] >= tau, 1.0, 0.0)
            bsel = jnp.where(rank1 == float(a), n_b, bsel)
        bsel_ref[h] = bsel
        r2_ref[h] = rank2.astype(r2_ref.dtype)
        p1_ref[h] = jnp.exp(s1 - v1[0]) / z
        p2_ref[h] = jnp.exp(s2 - v2[0]).astype(p2_ref.dtype)


def _peer_route(h2, gain, w_q, key1, key2, tq):
    S = h2.shape[0]
    const2 = lambda i: (0, 0)
    half = PEER_QDIM // 2
    kt = jax.ShapeDtypeStruct((PEER_HEADS, PEER_KEYS, S), F32)
    ktn = jax.ShapeDtypeStruct((PEER_HEADS, PEER_KEYS, S), MXU_DTYPE)
    kt_spec = pl.BlockSpec((PEER_HEADS, PEER_KEYS, tq), lambda i: (0, 0, i))
    return pl.pallas_call(
        _peer_route_kernel,
        grid=(S // tq,),
        in_specs=[
            pl.BlockSpec((tq, D_MODEL), lambda i: (i, 0)),
            pl.BlockSpec((1, D_MODEL), const2),
            pl.BlockSpec((D_MODEL, PEER_HEADS * PEER_QDIM), const2),
            pl.BlockSpec((PEER_KEYS, half), const2),
            pl.BlockSpec((PEER_KEYS, half), const2),
        ],
        out_specs=[pl.BlockSpec((tq, D_MODEL), lambda i: (i, 0)), kt_spec, kt_spec, kt_spec,
                   kt_spec],
        out_shape=[jax.ShapeDtypeStruct((S, D_MODEL), MXU_DTYPE), kt, ktn, kt, ktn],
        compiler_params=_cparams(("parallel",)),
    )(h2, gain.reshape(1, -1), w_q.astype(MXU_DTYPE), key1.astype(MXU_DTYPE),
      key2.astype(MXU_DTYPE))


def _peer_dense_kernel(xn_ref, u_ref, vT_ref, bsel_ref, r2_ref, p1_ref, p2_ref, h_ref,
                       o_ref, acc_ref, x_ref, *, te, ne):
    j = pl.program_id(1)

    @pl.when(j == 0)
    def _():
        acc_ref[...] = jnp.zeros(acc_ref.shape, F32)

    nt = (((1,), (1,)), ((), ()))
    hid = lax.dot_general(u_ref[...], xn_ref[...], nt, preferred_element_type=F32)
    act = (0.5 * hid * (1.0 + lax.erf(hid * (2.0 ** -0.5)))).astype(x_ref.dtype)
    for el in range(te // PEER_KEYS):
        e1 = j * (te // PEER_KEYS) + el
        g = None
        for h in range(PEER_HEADS):
            n_b = bsel_ref[h, pl.ds(e1, 1), :].astype(x_ref.dtype)
            p1r = p1_ref[h, pl.ds(e1, 1), :].astype(x_ref.dtype)
            t = jnp.where(r2_ref[h] < n_b, p2_ref[h], 0.0) * p1r
            g = t if g is None else g + t
        sl = slice(el * PEER_KEYS, (el + 1) * PEER_KEYS)
        x_ref[sl, :] = g * act[sl, :]
    acc_ref[...] += jnp.dot(vT_ref[...], x_ref[...], preferred_element_type=F32)

    @pl.when(j == ne - 1)
    def _():
        o_ref[...] = h_ref[...] + acc_ref[...].T


def _peer_dense(h2, xn, u, v, bsel, r2, p1, p2, tm, te):
    S = h2.shape[0]
    NE = u.shape[0]
    ne = NE // te
    kt_spec = pl.BlockSpec((PEER_HEADS, PEER_KEYS, tm), lambda i, j: (0, 0, i))
    return pl.pallas_call(
        functools.partial(_peer_dense_kernel, te=te, ne=ne),
        grid=(S // tm, ne),
        in_specs=[
            pl.BlockSpec((tm, D_MODEL), lambda i, j: (i, 0)),
            pl.BlockSpec((te, D_MODEL), lambda i, j: (j, 0)),
            pl.BlockSpec((D_MODEL, te), lambda i, j: (0, j)),
            kt_spec, kt_spec, kt_spec, kt_spec,
            pl.BlockSpec((tm, D_MODEL), lambda i, j: (i, 0)),
        ],
        out_specs=pl.BlockSpec((tm, D_MODEL), lambda i, j: (i, 0)),
        out_shape=jax.ShapeDtypeStruct((S, D_MODEL), F32),
        scratch_shapes=[pltpu.VMEM((D_MODEL, tm), F32), pltpu.VMEM((te, tm), MXU_DTYPE)],
        compiler_params=_cparams(("parallel", "arbitrary")),
    )(xn, u.astype(MXU_DTYPE), v.T.astype(MXU_DTYPE), bsel, r2, p1, p2, h2)


def _peer(h2, gain, w_q, key1, key2, u, v):
    S = h2.shape[0]
    xn, bsel, r2, p1, p2 = _peer_route(h2, gain, w_q, key1, key2, min(256, S))
    return _peer_dense(h2, xn, u, v, bsel, r2, p1, p2, min(512, S), 1024)


def _layer(x, mem, positions, p):
    S = x.shape[0]
    proj = _norm_proj(x, p["mix_norm_gain"], _pack_w_in(p["w_in"]), min(1024, S), 768)
    qT, kh, vT, iqT, ik, iwT = _att_prep(proj, positions, p["att_q_gain"], p["att_k_gain"], 256)
    bias = _dsa_select(iqT, iwT, ik, 256, 512)
    y_att = _dsa_attend(qT, kh, vT, bias, 256, 1024)
    y_rwkv = _rwkv_time_mix(proj, p)
    kmT, vm = _mem_kv(mem, p["mem_norm_gain"], p["w_mem_kv"], p["mem_k_gain"])
    h2 = _merge(x, y_att, y_rwkv, proj, kmT, vm, p["mem_q_gain"], p["gate_bias"], p["w_up"],
                p["w_out"], 256)
    return _peer(h2, p["ffn_norm_gain"], p["peer_w_q"], p["peer_key1"], p["peer_key2"],
                 p["peer_u"], p["peer_v"])


_PARAM_NAMES = ("mix_norm_gain", "w_in", "gate_bias", "att_q_gain", "att_k_gain", "mem_norm_gain",
                "w_mem_kv", "mem_q_gain", "mem_k_gain", "rwkv_mu", "rwkv_w0", "rwkv_w2", "rwkv_a0",
                "rwkv_a2", "rwkv_g2", "rwkv_k_k", "rwkv_k_a", "rwkv_r_k", "rwkv_ln_w", "rwkv_ln_b",
                "w_up", "w_out", "ffn_norm_gain", "peer_w_q", "peer_key1", "peer_key2", "peer_u",
                "peer_v")


def kernel(x, mem, positions, mix_norm_gain, w_in, gate_bias, att_q_gain, att_k_gain,
           mem_norm_gain, w_mem_kv, mem_q_gain, mem_k_gain, rwkv_mu, rwkv_w0, rwkv_w2, rwkv_a0,
           rwkv_a2, rwkv_g2, rwkv_k_k, rwkv_k_a, rwkv_r_k, rwkv_ln_w, rwkv_ln_b, w_up, w_out,
           ffn_norm_gain, peer_w_q, peer_key1, peer_key2, peer_u, peer_v):
    params = (mix_norm_gain, w_in, gate_bias, att_q_gain, att_k_gain, mem_norm_gain, w_mem_kv,
              mem_q_gain, mem_k_gain, rwkv_mu, rwkv_w0, rwkv_w2, rwkv_a0, rwkv_a2, rwkv_g2,
              rwkv_k_k, rwkv_k_a, rwkv_r_k, rwkv_ln_w, rwkv_ln_b, w_up, w_out, ffn_norm_gain,
              peer_w_q, peer_key1, peer_key2, peer_u, peer_v)
    assert x.shape[0] == 1 and all(t.shape[0] == 1 for t in params)
    p = {name: t[0] for name, t in zip(_PARAM_NAMES, params)}
    return _layer(x[0], mem[0], positions[0], p)[None]
```

```python
import functools

import numpy as np
import jax
import jax.numpy as jnp
from jax import lax
from jax.experimental import pallas as pl
from jax.experimental.pallas import tpu as pltpu

F32 = jnp.float32
I32 = jnp.int32
MXU_DTYPE = jnp.bfloat16
HI = lax.Precision.HIGHEST

D_MODEL = 1024
ATT_HEADS, ATT_HD = 8, 64
ATT_W = ATT_HEADS * ATT_HD
IDX_HEADS, IDX_HD = 4, 64
TOPK_MAX = 256
RWKV_HEADS, RWKV_HD = 8, 64
RWKV_W = RWKV_HEADS * RWKV_HD
DECAY_LORA, AAA_LORA, GATE_LORA = 64, 64, 128
GN_EPS = 64e-5
MEM_HEADS, MEM_HD = 4, 128
MEM_W = MEM_HEADS * MEM_HD
N_BRANCH = 3
ROPE_THETA = 500000.0
ROPE_ROT = ATT_HD // 4
NORM_EPS = 1e-6
PEER_KEYS = 128
PEER_HEADS = 8
PEER_QDIM = 256
PEER_TOPK = 16

LOG2E = 1.4426950408889634
V_ROWS = 80
LANES = 128
INT_MIN = -(2 ** 31)
NEG_BIG = -1e30
VMEM_LIMIT = 56 * 1024 * 1024

COL_GATE = 0
COL_Q = 3072
COL_K = 3584
COL_V = 4096
COL_R = 4608
COL_RK = 5120
COL_RV = 5632
COL_MEMQ = 6144
COL_IQ = 6656
COL_LORA = 6912
COL_IKIW = 7168
PROJ_COLS = 7680


def _cparams(sem):
    return pltpu.CompilerParams(dimension_semantics=sem, vmem_limit_bytes=VMEM_LIMIT)


def _mm(a, b):
    return jnp.dot(a.astype(MXU_DTYPE), b.astype(MXU_DTYPE), preferred_element_type=F32)


def _mm_hi(a, b):
    return jnp.dot(a.astype(F32), b.astype(F32), preferred_element_type=F32, precision=HI)


def _normproj_kernel(x_ref, g_ref, w_ref, o_ref, xn_ref):
    @pl.when(pl.program_id(1) == 0)
    def _():
        x = x_ref[...]
        ms = jnp.mean(x * x, axis=-1, keepdims=True)
        xn_ref[...] = (x * lax.rsqrt(ms + NORM_EPS) * g_ref[...]).astype(xn_ref.dtype)

    o_ref[...] = jnp.dot(xn_ref[...], w_ref[...], preferred_element_type=F32)


def _norm_proj(x, gain, w, tm, tn):
    S, D = x.shape
    N = w.shape[1]
    return pl.pallas_call(
        _normproj_kernel,
        grid=(S // tm, N // tn),
        in_specs=[
            pl.BlockSpec((tm, D), lambda i, j: (i, 0)),
            pl.BlockSpec((1, D), lambda i, j: (0, 0)),
            pl.BlockSpec((D, tn), lambda i, j: (0, j)),
        ],
        out_specs=pl.BlockSpec((tm, tn), lambda i, j: (i, j)),
        out_shape=jax.ShapeDtypeStruct((S, N), F32),
        scratch_shapes=[pltpu.VMEM((tm, D), w.dtype)],
        compiler_params=_cparams(("parallel", "arbitrary")),
    )(x, gain.reshape(1, D), w)


def _pack_w_in(w_in):
    D = w_in.shape[0]
    o = 0
    parts = {}
    for name, width in (("q", ATT_W), ("k", ATT_W), ("v", ATT_W), ("iq", IDX_HEADS * IDX_HD),
                        ("ik", IDX_HD), ("iw", IDX_HEADS), ("r", RWKV_W), ("rk", RWKV_W),
                        ("rv", RWKV_W), ("lora", DECAY_LORA + AAA_LORA + GATE_LORA),
                        ("memq", MEM_W), ("gate", N_BRANCH * D_MODEL)):
        parts[name] = w_in[:, o:o + width]
        o += width
    ikiw = jnp.concatenate([parts["ik"], parts["iw"]], axis=1)
    ikiw = jnp.pad(ikiw, ((0, 0), (0, LANES - ikiw.shape[1])))
    packed = jnp.concatenate(
        [parts["gate"], parts["q"], parts["k"], parts["v"], parts["r"], parts["rk"], parts["rv"],
         parts["memq"], parts["iq"], parts["lora"], ikiw], axis=1)
    packed = jnp.pad(packed, ((0, 0), (0, PROJ_COLS - packed.shape[1])))
    return packed.astype(MXU_DTYPE)


def _rope_tables():
    d = np.arange(LANES) % ATT_HD
    inv_freq = 1.0 / (ROPE_THETA ** (np.arange(0, ROPE_ROT, 2, dtype=np.float32) / ROPE_ROT))
    half = ROPE_ROT // 2
    tab = np.zeros((8, LANES), np.float32)
    tab[0] = np.where(d < ROPE_ROT, inv_freq[d % half], 0.0)
    tab[1] = np.where(d < half, -1.0, 0.0)
    tab[2] = np.where((d >= half) & (d < ROPE_ROT), 1.0, 0.0)
    return jnp.asarray(tab)


def _seg_mean_matrix(width, seg):
    g = (np.arange(width)[:, None] // seg) == (np.arange(width)[None, :] // seg)
    return jnp.asarray(g.astype(np.float32) / seg)


def _rope(x, cos, sina, sinb):
    W = x.shape[1]
    reps = W // LANES
    if reps > 1:
        cos = jnp.concatenate([cos] * reps, axis=1)
        sina = jnp.concatenate([sina] * reps, axis=1)
        sinb = jnp.concatenate([sinb] * reps, axis=1)
    half = ROPE_ROT // 2
    up = pltpu.roll(x, W - half, axis=1)
    down = pltpu.roll(x, half, axis=1)
    return x * cos + up * sina + down * sinb


def _att_prep_kernel(q_ref, k_ref, v_ref, iq_ref, ikiw_ref, pos_ref, tab_ref, seg_ref,
                     qg_ref, kg_ref, qT_ref, kh_ref, vT_ref, iqT_ref, ik_ref, iwT_ref):
    tab = tab_ref[...]
    ang = pos_ref[...].astype(F32) * tab[0:1, :]
    cos = jnp.cos(ang)
    sin = jnp.sin(ang)
    sina = sin * tab[1:2, :]
    sinb = sin * tab[2:3, :]
    seg = seg_ref[...]

    def head_norm(x, g):
        ms = _mm_hi(x * x, seg)
        return x * lax.rsqrt(ms + NORM_EPS) * g

    q = _rope(head_norm(q_ref[...], qg_ref[...]), cos, sina, sinb) * (ATT_HD ** -0.5 * LOG2E)
    k = _rope(head_norm(k_ref[...], kg_ref[...]), cos, sina, sinb)
    v = v_ref[...]
    iq = _rope(iq_ref[...], cos, sina, sinb)
    ik = _rope(ikiw_ref[...], cos, sina, sinb)
    for h in range(ATT_HEADS):
        kh_ref[h] = k[:, h * ATT_HD:(h + 1) * ATT_HD].astype(kh_ref.dtype)
    qT_ref[...] = q.T.astype(qT_ref.dtype)
    vT = v.T
    tq = vT.shape[1]
    pad = jnp.concatenate([jnp.ones((1, tq), F32), jnp.zeros((V_ROWS - ATT_HD - 1, tq), F32)], axis=0)
    for h in range(ATT_HEADS):
        vT_ref[h] = jnp.concatenate([vT[h * ATT_HD:(h + 1) * ATT_HD, :], pad],
                                    axis=0).astype(vT_ref.dtype)
    iqT_ref[...] = iq.T.astype(iqT_ref.dtype)
    ik_ref[...] = ik[:, :IDX_HD].astype(ik_ref.dtype)
    idx_scale = (IDX_HEADS ** -0.5) * (IDX_HD ** -0.5)
    iwT_ref[...] = ikiw_ref[...].T[IDX_HD:IDX_HD + 8, :] * idx_scale


def _att_prep(proj, positions, att_q_gain, att_k_gain, tq):
    S = proj.shape[0]
    qg = jnp.tile(att_q_gain.reshape(1, ATT_HD), (1, ATT_HEADS))
    kg = jnp.tile(att_k_gain.reshape(1, ATT_HD), (1, ATT_HEADS))
    col = lambda off, w: (lambda i: (i, off // w))
    const = lambda i: (0, 0)
    return pl.pallas_call(
        _att_prep_kernel,
        grid=(S // tq,),
        in_specs=[
            pl.BlockSpec((tq, ATT_W), col(COL_Q, ATT_W)),
            pl.BlockSpec((tq, ATT_W), col(COL_K, ATT_W)),
            pl.BlockSpec((tq, ATT_W), col(COL_V, ATT_W)),
            pl.BlockSpec((tq, IDX_HEADS * IDX_HD), col(COL_IQ, IDX_HEADS * IDX_HD)),
            pl.BlockSpec((tq, LANES), col(COL_IKIW, LANES)),
            pl.BlockSpec((tq, 1), lambda i: (i, 0)),
            pl.BlockSpec((8, LANES), const),
            pl.BlockSpec((ATT_W, ATT_W), const),
            pl.BlockSpec((1, ATT_W), const),
            pl.BlockSpec((1, ATT_W), const),
        ],
        out_specs=[
            pl.BlockSpec((ATT_W, tq), lambda i: (0, i)),
            pl.BlockSpec((ATT_HEADS, tq, ATT_HD), lambda i: (0, i, 0)),
            pl.BlockSpec((ATT_HEADS, V_ROWS, tq), lambda i: (0, 0, i)),
            pl.BlockSpec((IDX_HEADS * IDX_HD, tq), lambda i: (0, i)),
            pl.BlockSpec((tq, IDX_HD), lambda i: (i, 0)),
            pl.BlockSpec((8, tq), lambda i: (0, i)),
        ],
        out_shape=[
            jax.ShapeDtypeStruct((ATT_W, S), MXU_DTYPE),
            jax.ShapeDtypeStruct((ATT_HEADS, S, ATT_HD), MXU_DTYPE),
            jax.ShapeDtypeStruct((ATT_HEADS, V_ROWS, S), MXU_DTYPE),
            jax.ShapeDtypeStruct((IDX_HEADS * IDX_HD, S), MXU_DTYPE),
            jax.ShapeDtypeStruct((S, IDX_HD), MXU_DTYPE),
            jax.ShapeDtypeStruct((8, S), F32),
        ],
        compiler_params=_cparams(("parallel",)),
    )(proj, proj, proj, proj, proj, positions.reshape(S, 1), _rope_tables(),
      _seg_mean_matrix(ATT_W, ATT_HD), qg, kg)


KEY_NEG_INF = -2139095041


def _key_to_float(key):
    f = pltpu.bitcast(key ^ ((key >> 31) & 0x7FFFFFFF), F32)
    return jnp.where(key < KEY_NEG_INF, -jnp.inf, f)


SUB_ACC = 64


LIST_ROWS = 256
LIST_DEPTH = 10


def _sel_kernel(iqT_ref, iwT_ref, ik_ref, tri_ref, bias_ref, sc_ref, cand_ref, *, tq, tk, seq,
                nsel):
    i = pl.program_id(0)
    q0 = i * tq
    nc = (q0 + tq + tk - 1) // tk
    qidx = q0 + lax.broadcasted_iota(I32, (tk, tq), 1)
    krow = lax.broadcasted_iota(I32, (tk, tq), 0)
    iw = iwT_ref[...]

    def score_chunk(c, carry):
        c0 = pl.multiple_of(c * tk, tk)
        ikc = ik_ref[pl.ds(c0, tk), :]
        sc = None
        for h in range(IDX_HEADS):
            logit = jnp.dot(ikc, iqT_ref[h * IDX_HD:(h + 1) * IDX_HD, :],
                            preferred_element_type=F32)
            t = jnp.maximum(logit, 0.0) * iw[h:h + 1, :]
            sc = t if sc is None else sc + t
        sc = jnp.where(c0 + krow <= qidx, sc, -jnp.inf)
        sc_ref[pl.ds(c0, tk), :] = sc
        for part in range(tk // LIST_ROWS):
            x = sc[part * LIST_ROWS:(part + 1) * LIST_ROWS, :]
            for d in range(LIST_DEPTH):
                rows = slice(d * LIST_ROWS, (d + 1) * LIST_ROWS)
                kept = cand_ref[rows, :]
                cand_ref[rows, :] = jnp.maximum(kept, x)
                x = jnp.minimum(kept, x)
        return carry

    cand_ref[...] = jnp.full(cand_ref.shape, -jnp.inf, F32)
    lax.fori_loop(0, nc, score_chunk, 0)

    def count(pred, n_out=1, src_ref=sc_ref, n_chunks=nc):
        sub = SUB_ACC if n_out == 1 else SUB_ACC // 4

        def body(c, accs):
            c0 = pl.multiple_of(c * tk, tk)
            ms = pred(src_ref[pl.ds(c0, tk), :])
            return tuple(a + jnp.sum(m.reshape(tk // sub, sub, tq), axis=0)
                         for a, m in zip(accs, ms))
        accs = lax.fori_loop(0, n_chunks, body, (jnp.zeros((sub, tq), F32),) * n_out)
        return [_col_reduce(a, jnp.add) for a in accs]

    zero = jnp.zeros((1, tq), I32)

    def kth_largest(src_ref, n_chunks):
        def count_ge(cand):
            return count(lambda s: (jnp.where(s >= cand, 1.0, 0.0),), 1, src_ref, n_chunks)[0]

        prefix = jnp.where(count_ge(0.0) >= nsel, zero, zero + INT_MIN)

        def bit_body(b, prefix):
            cand = prefix | jnp.left_shift(jnp.int32(1), 30 - b)
            return jnp.where(count_ge(_key_to_float(cand)) >= nsel, cand, prefix)

        return _key_to_float(lax.fori_loop(0, 31, bit_body, prefix))

    n_cand_chunks = LIST_DEPTH * LIST_ROWS // tk
    tau_cand = kth_largest(cand_ref, n_cand_chunks)
    last_kept = _col_reduce(cand_ref[(LIST_DEPTH - 1) * LIST_ROWS:, :], jnp.maximum)
    lists_ok = jnp.min(jnp.where(last_kept <= tau_cand, 1.0, 0.0)) > 0.5
    tau = lax.cond(lists_ok, lambda: tau_cand, lambda: kth_largest(sc_ref, nc))

    def is_neg(s):
        return pltpu.bitcast(s, I32) < 0

    def tie_classes(s):
        eq = s == tau
        neg = is_neg(s)
        return (jnp.where(eq, jnp.where(neg, 0.0, 1.0), 0.0),
                jnp.where(eq, jnp.where(neg, 1.0, 0.0), 0.0))

    n_gt, n_eq_pos, n_eq_neg = count(
        lambda s: (jnp.where(s > tau, 1.0, 0.0),) + tie_classes(s), n_out=3)
    need = nsel - n_gt
    mixed = jnp.max(jnp.minimum(n_eq_pos, n_eq_neg)) > 0
    tri = tri_ref[...]

    def write_pass(both_signs):
        def body(c, carry):
            c0 = pl.multiple_of(c * tk, tk)
            s = sc_ref[pl.ds(c0, tk), :]
            if both_signs:
                ind_pos, ind_neg = tie_classes(s)
                cum_pos = carry[0] + _mm(tri, ind_pos)
                cum_neg = carry[1] + _mm(tri, ind_neg)
                rank = jnp.where(is_neg(s), n_eq_pos + cum_neg, cum_pos)
                carry = (cum_pos[tk - 1:tk, :], cum_neg[tk - 1:tk, :])
            else:
                rank = carry[0] + _mm(tri, jnp.where(s == tau, 1.0, 0.0))
                carry = (rank[tk - 1:tk, :], carry[1])
            tie = jnp.where(rank <= need, 0.0, NEG_BIG)
            b = jnp.where(s > tau, 0.0, jnp.where(s == tau, tie, NEG_BIG))
            b = jnp.where(c0 + krow <= qidx, b, NEG_BIG)
            bias_ref[pl.ds(c0, tk), :] = b.astype(bias_ref.dtype)
            return carry
        zf = jnp.zeros((1, tq), F32)
        lax.fori_loop(0, nc, body, (zf, zf))

    lax.cond(mixed, lambda: write_pass(True), lambda: write_pass(False))

    def fill_chunk(c, carry):
        c0 = pl.multiple_of(c * tk, tk)
        bias_ref[pl.ds(c0, tk), :] = jnp.full((tk, tq), NEG_BIG, bias_ref.dtype)
        return carry

    lax.fori_loop(nc, seq // tk, fill_chunk, 0)


def _dsa_select(iqT, iwT, ik, tq, tk):
    S = ik.shape[0]
    nsel = min(TOPK_MAX, S // 4)
    kern = functools.partial(_sel_kernel, tq=tq, tk=tk, seq=S, nsel=nsel)
    return pl.pallas_call(
        kern,
        grid=(S // tq,),
        in_specs=[
            pl.BlockSpec((IDX_HEADS * IDX_HD, tq), lambda i: (0, i)),
            pl.BlockSpec((8, tq), lambda i: (0, i)),
            pl.BlockSpec((S, IDX_HD), lambda i: (0, 0)),
            pl.BlockSpec((tk, tk), lambda i: (0, 0)),
        ],
        out_specs=pl.BlockSpec((S, tq), lambda i: (0, i)),
        out_shape=jax.ShapeDtypeStruct((S, S), jnp.bfloat16),
        scratch_shapes=[pltpu.VMEM((S, tq), F32),
                        pltpu.VMEM((LIST_DEPTH * LIST_ROWS, tq), F32)],
        compiler_params=_cparams(("parallel",)),
    )(iqT, iwT, ik, jnp.tril(jnp.ones((tk, tk), MXU_DTYPE)))


def _col_reduce(x, op):
    rows = x.shape[0]
    while rows > 8:
        rows //= 2
        x = op(x[:rows], x[rows:])
    red = jnp.max if op is jnp.maximum else jnp.sum
    return red(x, axis=0, keepdims=True)


def _att_kernel(qi_ref, kj_ref, qT_ref, k_ref, vT_ref, bias_ref, o_ref, m_ref, acc_ref,
                *, tq, tk):
    t = pl.program_id(0)
    i = qi_ref[t]
    j = kj_ref[t]
    last = ((i + 1) * tq - 1) // tk

    @pl.when(j == 0)
    def _():
        m_ref[...] = jnp.full(m_ref.shape, -jnp.inf, F32)
        acc_ref[...] = jnp.zeros(acc_ref.shape, F32)

    bias = bias_ref[...].astype(F32)
    hs = lambda h: slice(h * ATT_HD, (h + 1) * ATT_HD)
    m_all = m_ref[...]
    m_out = []
    halves = (slice(0, tk // 2), slice(tk // 2, tk))
    for h in range(ATT_HEADS):
        qh = qT_ref[hs(h), :]
        ss = [jnp.dot(k_ref[h, ks, :], qh, preferred_element_type=F32) + bias[ks, :]
              for ks in halves]
        m_prev = m_all[h:h + 1, :]
        m_new = jnp.maximum(m_prev, jnp.maximum(_col_reduce(ss[0], jnp.maximum),
                                                _col_reduce(ss[1], jnp.maximum)))
        alpha = jnp.exp2(m_prev - m_new)
        m_out.append(m_new)
        pv = [jnp.dot(vT_ref[h, :, ks], jnp.exp2((s - m_new).astype(vT_ref.dtype)),
                      preferred_element_type=F32) for ks, s in zip(halves, ss)]
        acc_ref[h] = alpha * acc_ref[h] + pv[0] + pv[1]
    m_ref[...] = jnp.concatenate(m_out, axis=0)

    @pl.when(j == last)
    def _():
        out = jnp.concatenate(
            [acc_ref[h, :ATT_HD, :] / acc_ref[h, ATT_HD:ATT_HD + 1, :]
             for h in range(ATT_HEADS)], axis=0)
        o_ref[...] = out.T


def _dsa_attend(qT, kh, vT, bias, tq, tk):
    S = qT.shape[1]
    pairs = [(i, j) for i in range(S // tq) for j in range(((i + 1) * tq - 1) // tk + 1)]
    qi = jnp.asarray([p[0] for p in pairs], I32)
    kj = jnp.asarray([p[1] for p in pairs], I32)
    grid_spec = pltpu.PrefetchScalarGridSpec(
        num_scalar_prefetch=2,
        grid=(len(pairs),),
        in_specs=[
            pl.BlockSpec((ATT_W, tq), lambda t, qi, kj: (0, qi[t])),
            pl.BlockSpec((ATT_HEADS, tk, ATT_HD), lambda t, qi, kj: (0, kj[t], 0)),
            pl.BlockSpec((ATT_HEADS, V_ROWS, tk), lambda t, qi, kj: (0, 0, kj[t])),
            pl.BlockSpec((tk, tq), lambda t, qi, kj: (kj[t], qi[t])),
        ],
        out_specs=pl.BlockSpec((tq, ATT_W), lambda t, qi, kj: (qi[t], 0)),
        scratch_shapes=[
            pltpu.VMEM((ATT_HEADS, tq), F32),
            pltpu.VMEM((ATT_HEADS, V_ROWS, tq), F32),
        ],
    )
    return pl.pallas_call(
        functools.partial(_att_kernel, tq=tq, tk=tk),
        grid_spec=grid_spec,
        out_shape=jax.ShapeDtypeStruct((S, ATT_W), F32),
        compiler_params=_cparams(("arbitrary",)),
    )(qi, kj, qT, kh, vT, bias)


CHUNK = 64


def _shift_rows(x, prev8, first):
    prev_row = jnp.where(first, 0.0, prev8[7:8, :])
    row = lax.broadcasted_iota(I32, x.shape, 0)
    return jnp.where(row == 0, prev_row, pltpu.roll(x, 1, axis=0))


def _softplus(z):
    return jnp.maximum(z, 0.0) + jnp.log1p(jnp.exp(-jnp.abs(z)))


def _rwkv_prep_kernel(r_ref, k_ref, v_ref, lo_ref, rp_ref, kp_ref, vp_ref, lop_ref,
                      mur_ref, muk_ref, muv_ref, mulo_ref, w0_ref, w2_ref, a0_ref, a2_ref,
                      g2_ref, kk_ref, ka_ref, rk_ref, seg_ref, tri_ref, ones_ref, end_ref,
                      rt_ref, kt_ref, bt_ref, at_ref, kh_ref, bh_ref, vh_ref, gam_ref,
                      bonus_ref, g_ref):
    first = pl.program_id(0) == 0

    def mix(x_ref, p_ref, mu_ref):
        x = x_ref[...]
        return x + (_shift_rows(x, p_ref[...], first) - x) * mu_ref[...]

    r = mix(r_ref, rp_ref, mur_ref)
    k = mix(k_ref, kp_ref, muk_ref)
    v = mix(v_ref, vp_ref, muv_ref)
    lo = mix(lo_ref, lop_ref, mulo_ref)
    wd = lo[:, :DECAY_LORA]
    ad = lo[:, DECAY_LORA:DECAY_LORA + AAA_LORA]
    gd = lo[:, DECAY_LORA + AAA_LORA:]
    w_log = -_softplus(-(w0_ref[...] + _mm(jnp.tanh(wd), w2_ref[...]))) - 0.5
    lw = -jnp.exp(w_log)
    a = jax.nn.sigmoid(a0_ref[...] + _mm(ad, a2_ref[...]))
    g_ref[...] = _mm(jax.nn.sigmoid(gd), g2_ref[...])
    seg = seg_ref[...]
    kk = k * kk_ref[...]
    kk = kk / jnp.maximum(jnp.sqrt(_mm_hi(kk * kk, seg)), 1e-12)
    k = k * (1.0 + (a - 1.0) * ka_ref[...])
    bonus_ref[...] = _mm_hi(r * k * rk_ref[...], seg) * v
    avec = -kk
    bvec = kk * a
    cs = _mm_hi(tri_ref[...], lw)
    cs_end = _mm_hi(ones_ref[...], lw)
    e_neg = jnp.exp(-cs)
    e_end = jnp.exp(cs_end - cs)
    outs = ((rt_ref, r * jnp.exp(cs)), (kt_ref, k * e_neg), (bt_ref, bvec * e_neg),
            (at_ref, avec * jnp.exp(cs - lw)), (kh_ref, k * e_end), (bh_ref, bvec * e_end),
            (vh_ref, v), (gam_ref, jnp.exp(_mm_hi(end_ref[...], lw))))
    for ref, val in outs:
        for h in range(RWKV_HEADS):
            ref[h] = val[:, h * RWKV_HD:(h + 1) * RWKV_HD]


def _rwkv_prep(proj, p, tq):
    S = proj.shape[0]
    nch = tq // CHUNK
    col = lambda off, w: (lambda i: (i, off // w))
    prev = lambda off, w: (lambda i: (jnp.maximum(i * (tq // 8) - 1, 0), off // w))
    const = lambda i: (0, 0)
    lw_ = DECAY_LORA + AAA_LORA + GATE_LORA
    mu = p["rwkv_mu"]
    row = lambda t: t.reshape(1, -1)
    t_idx = np.arange(tq)
    same = (t_idx[:, None] // CHUNK) == (t_idx[None, :] // CHUNK)
    tri = jnp.asarray((same & (t_idx[None, :] <= t_idx[:, None])).astype(np.float32))
    ones = jnp.asarray(same.astype(np.float32))
    end = jnp.asarray(((t_idx[None, :] // CHUNK) == np.arange(nch)[:, None]).astype(np.float32))
    seg = _seg_mean_matrix(RWKV_W, RWKV_HD) * RWKV_HD
    hm = jax.ShapeDtypeStruct((RWKV_HEADS, S, RWKV_HD), F32)
    hm_spec = pl.BlockSpec((RWKV_HEADS, tq, RWKV_HD), lambda i: (0, i, 0))
    wide = jax.ShapeDtypeStruct((S, RWKV_W), F32)
    wide_spec = pl.BlockSpec((tq, RWKV_W), lambda i: (i, 0))
    vec = lambda w: pl.BlockSpec((1, w), const)
    return pl.pallas_call(
        _rwkv_prep_kernel,
        grid=(S // tq,),
        in_specs=[
            pl.BlockSpec((tq, RWKV_W), col(COL_R, RWKV_W)),
            pl.BlockSpec((tq, RWKV_W), col(COL_RK, RWKV_W)),
            pl.BlockSpec((tq, RWKV_W), col(COL_RV, RWKV_W)),
            pl.BlockSpec((tq, lw_), col(COL_LORA, lw_)),
            pl.BlockSpec((8, RWKV_W), prev(COL_R, RWKV_W)),
            pl.BlockSpec((8, RWKV_W), prev(COL_RK, RWKV_W)),
            pl.BlockSpec((8, RWKV_W), prev(COL_RV, RWKV_W)),
            pl.BlockSpec((8, lw_), prev(COL_LORA, lw_)),
            vec(RWKV_W), vec(RWKV_W), vec(RWKV_W), vec(lw_),
            vec(RWKV_W), pl.BlockSpec((DECAY_LORA, RWKV_W), const),
            vec(RWKV_W), pl.BlockSpec((AAA_LORA, RWKV_W), const),
            pl.BlockSpec((GATE_LORA, RWKV_W), const),
            vec(RWKV_W), vec(RWKV_W), vec(RWKV_W),
            pl.BlockSpec((RWKV_W, RWKV_W), const),
            pl.BlockSpec((tq, tq), const), pl.BlockSpec((tq, tq), const),
            pl.BlockSpec((nch, tq), const),
        ],
        out_specs=[hm_spec] * 7 + [pl.BlockSpec((RWKV_HEADS, nch, RWKV_HD), lambda i: (0, i, 0)),
                                   wide_spec, wide_spec],
        out_shape=[hm] * 7 + [jax.ShapeDtypeStruct((RWKV_HEADS, S // CHUNK, RWKV_HD), F32),
                              wide, wide],
        compiler_params=_cparams(("parallel",)),
    )(proj, proj, proj, proj, proj, proj, proj, proj,
      row(mu[:RWKV_W]), row(mu[RWKV_W:2 * RWKV_W]), row(mu[2 * RWKV_W:3 * RWKV_W]),
      row(mu[3 * RWKV_W:]), row(p["rwkv_w0"]), p["rwkv_w2"], row(p["rwkv_a0"]), p["rwkv_a2"],
      p["rwkv_g2"], row(p["rwkv_k_k"]), row(p["rwkv_k_a"]), row(p["rwkv_r_k"]), seg, tri, ones, end)


def _bmm(a, b, dims):
    return jnp.einsum(dims, a, b, preferred_element_type=F32, precision=HI)


def _bmm1(a, b, dims):
    return jnp.einsum(dims, a.astype(MXU_DTYPE), b.astype(MXU_DTYPE), preferred_element_type=F32)


def _rwkv_chunk_kernel(rt_ref, kt_ref, bt_ref, at_ref, kh_ref, bh_ref, v_ref, gam_ref,
                       p_ref, q_ref, rw_ref, y0_ref, *, nch):
    L, N = CHUNK, RWKV_HD
    ri = lax.broadcasted_iota(I32, (nch, L, L), 1)
    ci = lax.broadcasted_iota(I32, (nch, L, L), 2)
    eye_n = (lax.broadcasted_iota(I32, (nch, N, N), 1)
             == lax.broadcasted_iota(I32, (nch, N, N), 2)).astype(F32)
    for h in range(RWKV_HEADS):
        ld = lambda ref: ref[h].reshape(nch, L, N)
        rt, kt, bt, at, kh, bh, v = (ld(x) for x in (rt_ref, kt_ref, bt_ref, at_ref, kh_ref,
                                                      bh_ref, v_ref))
        gam = gam_ref[h].reshape(nch, 1, N)
        mm = _bmm1
        a_ab = jnp.where(ci < ri, mm(at, bt, "cld,cmd->clm"), 0.0)
        a_ak = jnp.where(ci < ri, mm(at, kt, "cld,cmd->clm"), 0.0)
        m_rk = jnp.where(ci <= ri, mm(rt, kt, "cld,cmd->clm"), 0.0)
        m_rb = jnp.where(ci <= ri, mm(rt, bt, "cld,cmd->clm"), 0.0)
        rhs = jnp.concatenate([at, mm(a_ak, v, "clm,cmd->cld")], axis=2)
        pw = a_ab
        step = 1
        while True:
            rhs = rhs + mm(pw, rhs, "clm,cmd->cld")
            step *= 2
            if step >= L:
                break
            pw = mm(pw, pw, "clm,cmn->cln")
        w, u0 = rhs[:, :, :N], rhs[:, :, N:]
        p_ref[:, h] = eye_n * gam + mm(bh, w, "cld,cle->cde")
        q_ref[:, h] = mm(kh, v, "cld,cle->cde") + mm(bh, u0, "cld,cle->cde")
        rw_ref[h] = (rt + mm(m_rb, w, "clm,cmd->cld")).reshape(nch * L, N)
        y0_ref[h] = (mm(m_rk, v, "clm,cmd->cld") + mm(m_rb, u0, "clm,cmd->cld")).reshape(nch * L, N)


def _rwkv_chunks(rt, kt, bt, at, kh, bh, vh, gam, tt):
    S = rt.shape[1]
    nch = tt // CHUNK
    hm_spec = pl.BlockSpec((RWKV_HEADS, tt, RWKV_HD), lambda i: (0, i, 0))
    hm = jax.ShapeDtypeStruct((RWKV_HEADS, S, RWKV_HD), F32)
    sq_spec = pl.BlockSpec((nch, RWKV_HEADS, RWKV_HD, RWKV_HD), lambda i: (i, 0, 0, 0))
    sq = jax.ShapeDtypeStruct((S // CHUNK, RWKV_HEADS, RWKV_HD, RWKV_HD), F32)
    return pl.pallas_call(
        functools.partial(_rwkv_chunk_kernel, nch=nch),
        grid=(S // tt,),
        in_specs=[hm_spec] * 7 + [pl.BlockSpec((RWKV_HEADS, nch, RWKV_HD), lambda i: (0, i, 0))],
        out_specs=[sq_spec, sq_spec, hm_spec, hm_spec],
        out_shape=[sq, sq, hm, hm],
        compiler_params=_cparams(("parallel",)),
    )(rt, kt, bt, at, kh, bh, vh, gam)


def _rwkv_scan_kernel(p_ref, q_ref, rw_ref, y0_ref, bonus_ref, g_ref, seg_ref, lnw_ref, lnb_ref,
                      o_ref, st_ref, *, nch):
    @pl.when(pl.program_id(0) == 0)
    def _():
        st_ref[...] = jnp.zeros(st_ref.shape, F32)

    L = CHUNK
    st = st_ref[...]
    ys = []
    for c in range(nch):
        rw = rw_ref[:, c * L:(c + 1) * L, :]
        ys.append(_bmm(rw, st, "hld,hde->hle") + y0_ref[:, c * L:(c + 1) * L, :])
        st = _bmm(p_ref[c], st, "hjk,hki->hji") + q_ref[c]
    st_ref[...] = st
    y = jnp.concatenate(ys, axis=1)
    y = jnp.concatenate([y[h] for h in range(RWKV_HEADS)], axis=1)
    seg = seg_ref[...]
    mean = _mm_hi(y, seg)
    d = y - mean
    var = _mm_hi(d * d, seg)
    y = d * lax.rsqrt(var + GN_EPS) * lnw_ref[...] + lnb_ref[...]
    o_ref[...] = (y + bonus_ref[...]) * g_ref[...]


def _rwkv_scan(pm, qm, rw, y0, bonus, g, ln_w, ln_b, tt):
    S = rw.shape[1]
    nch = tt // CHUNK
    hm_spec = pl.BlockSpec((RWKV_HEADS, tt, RWKV_HD), lambda i: (0, i, 0))
    sq_spec = pl.BlockSpec((nch, RWKV_HEADS, RWKV_HD, RWKV_HD), lambda i: (i, 0, 0, 0))
    wide_spec = pl.BlockSpec((tt, RWKV_W), lambda i: (i, 0))
    const = lambda i: (0, 0)
    return pl.pallas_call(
        functools.partial(_rwkv_scan_kernel, nch=nch),
        grid=(S // tt,),
        in_specs=[sq_spec, sq_spec, hm_spec, hm_spec, wide_spec, wide_spec,
                  pl.BlockSpec((RWKV_W, RWKV_W), const), pl.BlockSpec((1, RWKV_W), const),
                  pl.BlockSpec((1, RWKV_W), const)],
        out_specs=wide_spec,
        out_shape=jax.ShapeDtypeStruct((S, RWKV_W), F32),
        scratch_shapes=[pltpu.VMEM((RWKV_HEADS, RWKV_HD, RWKV_HD), F32)],
        compiler_params=_cparams(("arbitrary",)),
    )(pm, qm, rw, y0, bonus, g, _seg_mean_matrix(RWKV_W, RWKV_HD), ln_w.reshape(1, -1),
      ln_b.reshape(1, -1))


def _rwkv_time_mix(proj, p):
    outs = _rwkv_prep(proj, p, 512)
    rt, kt, bt, at, kh, bh, vh, gam, bonus, g = outs
    pm, qm, rw, y0 = _rwkv_chunks(rt, kt, bt, at, kh, bh, vh, gam, 512)
    return _rwkv_scan(pm, qm, rw, y0, bonus, g, p["rwkv_ln_w"], p["rwkv_ln_b"], 512)


def _lane_rmsnorm(x, gain):
    return x * lax.rsqrt(jnp.mean(x * x, axis=-1, keepdims=True) + NORM_EPS) * gain


def _mem_kv_kernel(mem_ref, g_ref, w_ref, kg_ref, kmT_ref, vm_ref):
    m = _lane_rmsnorm(mem_ref[...], g_ref[...])
    kv = _mm(m, w_ref[...])
    km = jnp.concatenate(
        [_lane_rmsnorm(kv[:, h * MEM_HD:(h + 1) * MEM_HD], kg_ref[...]) for h in range(MEM_HEADS)],
        axis=1)
    kmT_ref[...] = km.T.astype(kmT_ref.dtype)
    vm_ref[...] = kv[:, MEM_W:].astype(vm_ref.dtype)


def _mem_kv(mem, gain, w_kv, k_gain):
    M = mem.shape[0]
    return pl.pallas_call(
        _mem_kv_kernel,
        out_shape=[jax.ShapeDtypeStruct((MEM_W, M), MXU_DTYPE),
                   jax.ShapeDtypeStruct((M, MEM_W), MXU_DTYPE)],
        compiler_params=pltpu.CompilerParams(vmem_limit_bytes=VMEM_LIMIT),
    )(mem, gain.reshape(1, -1), w_kv.astype(MXU_DTYPE), k_gain.reshape(1, -1))


def _merge_kernel(x_ref, yatt_ref, yrwkv_ref, memq_ref, gate_ref, kmT_ref, vm_ref, qg_ref,
                  gb_ref, wup_ref, wout_ref, o_ref):
    q = memq_ref[...]
    heads = []
    for h in range(MEM_HEADS):
        sl = slice(h * MEM_HD, (h + 1) * MEM_HD)
        qn = _lane_rmsnorm(q[:, sl], qg_ref[...])
        s = _mm(qn, kmT_ref[sl, :]) * (MEM_HD ** -0.5)
        s = s - jnp.max(s, axis=-1, keepdims=True)
        e = jnp.exp(s)
        p = e / jnp.sum(e, axis=-1, keepdims=True)
        heads.append(_mm(p, vm_ref[:, sl]))
    ymem = jnp.concatenate(heads, axis=1)
    merged = None
    for c, y in enumerate((yatt_ref[...], yrwkv_ref[...], ymem)):
        up = _mm(y, wup_ref[c])
        gate = jax.nn.sigmoid(gate_ref[:, c * D_MODEL:(c + 1) * D_MODEL] + gb_ref[c:c + 1, :])
        merged = gate * up if merged is None else merged + gate * up
    o_ref[...] = x_ref[...] + _mm(merged, wout_ref[...])


def _merge(x, y_att, y_rwkv, proj, kmT, vm, mem_q_gain, gate_bias, w_up, w_out, tq):
    S = x.shape[0]
    M = vm.shape[0]
    gw = N_BRANCH * D_MODEL
    const2 = lambda i: (0, 0)
    return pl.pallas_call(
        _merge_kernel,
        grid=(S // tq,),
        in_specs=[
            pl.BlockSpec((tq, D_MODEL), lambda i: (i, 0)),
            pl.BlockSpec((tq, ATT_W), lambda i: (i, 0)),
            pl.BlockSpec((tq, RWKV_W), lambda i: (i, 0)),
            pl.BlockSpec((tq, MEM_W), lambda i: (i, COL_MEMQ // MEM_W)),
            pl.BlockSpec((tq, gw), lambda i: (i, COL_GATE // gw)),
            pl.BlockSpec((MEM_W, M), const2),
            pl.BlockSpec((M, MEM_W), const2),
            pl.BlockSpec((1, MEM_HD), const2),
            pl.BlockSpec((N_BRANCH, D_MODEL), const2),
            pl.BlockSpec((N_BRANCH, ATT_W, D_MODEL), lambda i: (0, 0, 0)),
            pl.BlockSpec((D_MODEL, D_MODEL), const2),
        ],
        out_specs=pl.BlockSpec((tq, D_MODEL), lambda i: (i, 0)),
        out_shape=jax.ShapeDtypeStruct((S, D_MODEL), F32),
        compiler_params=_cparams(("parallel",)),
    )(x, y_att, y_rwkv, proj, proj, kmT, vm, mem_q_gain.reshape(1, -1), gate_bias,
      w_up.astype(MXU_DTYPE), w_out.astype(MXU_DTYPE))


NO_RANK = 255.0


def _pop_max(cur, rows):
    m = jnp.max(cur, axis=0, keepdims=True)
    first = jnp.min(jnp.where(cur == m, rows, cur.shape[0]), axis=0, keepdims=True)
    hit = rows == first
    return m, hit, jnp.where(hit, -jnp.inf, cur)


def _peer_route_kernel(h_ref, g_ref, wq_ref, k1_ref, k2_ref,
                       xn_ref, bsel_ref, r2_ref, p1_ref, p2_ref):
    tq = h_ref.shape[0]
    xn = _lane_rmsnorm(h_ref[...], g_ref[...]).astype(xn_ref.dtype)
    xn_ref[...] = xn
    qp = jnp.dot(xn, wq_ref[...], preferred_element_type=F32)
    nt = (((1,), (1,)), ((), ()))
    half = PEER_QDIM // 2
    rows = lax.broadcasted_iota(I32, (PEER_KEYS, tq), 0)
    pairs = [(a, b) for a in range(PEER_TOPK) for b in range(PEER_TOPK)
             if (a + 1) * (b + 1) <= PEER_TOPK]
    npad = -len(pairs) % 8
    crow = lax.broadcasted_iota(I32, (len(pairs) + npad, tq), 0)
    for h in range(PEER_HEADS):
        q1 = qp[:, h * PEER_QDIM:h * PEER_QDIM + half].astype(MXU_DTYPE)
        q2 = qp[:, h * PEER_QDIM + half:(h + 1) * PEER_QDIM].astype(MXU_DTYPE)
        s1 = lax.dot_general(k1_ref[...], q1, nt, preferred_element_type=F32)
        s2 = lax.dot_general(k2_ref[...], q2, nt, preferred_element_type=F32)
        tops, ranks = [], []
        for s in (s1, s2):
            cur, vals = s, []
            rank = jnp.full((PEER_KEYS, tq), NO_RANK, F32)
            for r in range(PEER_TOPK):
                m, hit, cur = _pop_max(cur, rows)
                rank = jnp.where(hit, float(r), rank)
                vals.append(m)
            tops.append(vals)
            ranks.append(rank)
        v1, v2 = tops
        rank1, rank2 = ranks
        cand = jnp.concatenate([v1[a] + v2[b] for a, b in pairs]
                               + [jnp.full((npad, tq), -jnp.inf, F32)], axis=0)
        top = v1[0] + v2[0]
        z = jnp.zeros((1, tq), F32)
        for _ in range(PEER_TOPK):
            tau, _, cand = _pop_max(cand, crow)
            z = z + jnp.exp(tau - top)
        bsel = jnp.zeros((PEER_KEYS, tq), F32)
        for a in range(PEER_TOPK):
            n_b = jnp.zeros((1, tq), F32)
            for b in [pb for pa, pb in pairs if pa == a]:
                n_b = n_b + jnp.where(v1[a] + v2[b] >= tau, 1.0, 0.0)
            bsel = jnp.where(rank1 == float(a), n_b, bsel)
        bsel_ref[h] = bsel
        r2_ref[h] = rank2.astype(r2_ref.dtype)
        p1_ref[h] = jnp.exp(s1 - v1[0]) / z
        p2_ref[h] = jnp.exp(s2 - v2[0]).astype(p2_ref.dtype)


def _peer_route(h2, gain, w_q, key1, key2, tq):
    S = h2.shape[0]
    const2 = lambda i: (0, 0)
    half = PEER_QDIM // 2
    kt = jax.ShapeDtypeStruct((PEER_HEADS, PEER_KEYS, S), F32)
    ktn = jax.ShapeDtypeStruct((PEER_HEADS, PEER_KEYS, S), MXU_DTYPE)
    kt_spec = pl.BlockSpec((PEER_HEADS, PEER_KEYS, tq), lambda i: (0, 0, i))
    return pl.pallas_call(
        _peer_route_kernel,
        grid=(S // tq,),
        in_specs=[
            pl.BlockSpec((tq, D_MODEL), lambda i: (i, 0)),
            pl.BlockSpec((1, D_MODEL), const2),
            pl.BlockSpec((D_MODEL, PEER_HEADS * PEER_QDIM), const2),
            pl.BlockSpec((PEER_KEYS, half), const2),
            pl.BlockSpec((PEER_KEYS, half), const2),
        ],
        out_specs=[pl.BlockSpec((tq, D_MODEL), lambda i: (i, 0)), kt_spec, kt_spec, kt_spec,
                   kt_spec],
        out_shape=[jax.ShapeDtypeStruct((S, D_MODEL), MXU_DTYPE), kt, ktn, kt, ktn],
        compiler_params=_cparams(("parallel",)),
    )(h2, gain.reshape(1, -1), w_q.astype(MXU_DTYPE), key1.astype(MXU_DTYPE),
      key2.astype(MXU_DTYPE))


def _peer_dense_kernel(xn_ref, u_ref, vT_ref, bsel_ref, r2_ref, p1_ref, p2_ref, h_ref,
                       o_ref, acc_ref, x_ref, *, te, ne):
    j = pl.program_id(1)

    @pl.when(j == 0)
    def _():
        acc_ref[...] = jnp.zeros(acc_ref.shape, F32)

    nt = (((1,), (1,)), ((), ()))
    hid = lax.dot_general(u_ref[...], xn_ref[...], nt, preferred_element_type=F32)
    act = (0.5 * hid * (1.0 + lax.erf(hid * (2.0 ** -0.5)))).astype(x_ref.dtype)
    for el in range(te // PEER_KEYS):
        e1 = j * (te // PEER_KEYS) + el
        g = None
        for h in range(PEER_HEADS):
            n_b = bsel_ref[h, pl.ds(e1, 1), :].astype(x_ref.dtype)
            p1r = p1_ref[h, pl.ds(e1, 1), :].astype(x_ref.dtype)
            t = jnp.where(r2_ref[h] < n_b, p2_ref[h], 0.0) * p1r
            g = t if g is None else g + t
        sl = slice(el * PEER_KEYS, (el + 1) * PEER_KEYS)
        x_ref[sl, :] = g * act[sl, :]
    acc_ref[...] += jnp.dot(vT_ref[...], x_ref[...], preferred_element_type=F32)

    @pl.when(j == ne - 1)
    def _():
        o_ref[...] = h_ref[...] + acc_ref[...].T


def _peer_dense(h2, xn, u, v, bsel, r2, p1, p2, tm, te):
    S = h2.shape[0]
    NE = u.shape[0]
    ne = NE // te
    kt_spec = pl.BlockSpec((PEER_HEADS, PEER_KEYS, tm), lambda i, j: (0, 0, i))
    return pl.pallas_call(
        functools.partial(_peer_dense_kernel, te=te, ne=ne),
        grid=(S // tm, ne),
        in_specs=[
            pl.BlockSpec((tm, D_MODEL), lambda i, j: (i, 0)),
            pl.BlockSpec((te, D_MODEL), lambda i, j: (j, 0)),
            pl.BlockSpec((D_MODEL, te), lambda i, j: (0, j)),
            kt_spec, kt_spec, kt_spec, kt_spec,
            pl.BlockSpec((tm, D_MODEL), lambda i, j: (i, 0)),
        ],
        out_specs=pl.BlockSpec((tm, D_MODEL), lambda i, j: (i, 0)),
        out_shape=jax.ShapeDtypeStruct((S, D_MODEL), F32),
        scratch_shapes=[pltpu.VMEM((D_MODEL, tm), F32), pltpu.VMEM((te, tm), MXU_DTYPE)],
        compiler_params=_cparams(("parallel", "arbitrary")),
    )(xn, u.astype(MXU_DTYPE), v.T.astype(MXU_DTYPE), bsel, r2, p1, p2, h2)


def _peer(h2, gain, w_q, key1, key2, u, v):
    S = h2.shape[0]
    xn, bsel, r2, p1, p2 = _peer_route(h2, gain, w_q, key1, key2, min(256, S))
    return _peer_dense(h2, xn, u, v, bsel, r2, p1, p2, min(512, S), 1024)


def _layer(x, mem, positions, p):
    S = x.shape[0]
    proj = _norm_proj(x, p["mix_norm_gain"], _pack_w_in(p["w_in"]), min(1024, S), 768)
    qT, kh, vT, iqT, ik, iwT = _att_prep(proj, positions, p["att_q_gain"], p["att_k_gain"], 256)
    bias = _dsa_select(iqT, iwT, ik, 256, 512)
    y_att = _dsa_attend(qT, kh, vT, bias, 512, 1024)
    y_rwkv = _rwkv_time_mix(proj, p)
    kmT, vm = _mem_kv(mem, p["mem_norm_gain"], p["w_mem_kv"], p["mem_k_gain"])
    h2 = _merge(x, y_att, y_rwkv, proj, kmT, vm, p["mem_q_gain"], p["gate_bias"], p["w_up"],
                p["w_out"], 256)
    return _peer(h2, p["ffn_norm_gain"], p["peer_w_q"], p["peer_key1"], p["peer_key2"],
                 p["peer_u"], p["peer_v"])


_PARAM_NAMES = ("mix_norm_gain", "w_in", "gate_bias", "att_q_gain", "att_k_gain", "mem_norm_gain",
                "w_mem_kv", "mem_q_gain", "mem_k_gain", "rwkv_mu", "rwkv_w0", "rwkv_w2", "rwkv_a0",
                "rwkv_a2", "rwkv_g2", "rwkv_k_k", "rwkv_k_a", "rwkv_r_k", "rwkv_ln_w", "rwkv_ln_b",
                "w_up", "w_out", "ffn_norm_gain", "peer_w_q", "peer_key1", "peer_key2", "peer_u",
                "peer_v")


def kernel(x, mem, positions, mix_norm_gain, w_in, gate_bias, att_q_gain, att_k_gain,
           mem_norm_gain, w_mem_kv, mem_q_gain, mem_k_gain, rwkv_mu, rwkv_w0, rwkv_w2, rwkv_a0,
           rwkv_a2, rwkv_g2, rwkv_k_k, rwkv_k_a, rwkv_r_k, rwkv_ln_w, rwkv_ln_b, w_up, w_out,
           ffn_norm_gain, peer_w_q, peer_key1, peer_key2, peer_u, peer_v):
    params = (mix_norm_gain, w_in, gate_bias, att_q_gain, att_k_gain, mem_norm_gain, w_mem_kv,
              mem_q_gain, mem_k_gain, rwkv_mu, rwkv_w0, rwkv_w2, rwkv_a0, rwkv_a2, rwkv_g2,
              rwkv_k_k, rwkv_k_a, rwkv_r_k, rwkv_ln_w, rwkv_ln_b, w_up, w_out, ffn_norm_gain,
              peer_w_q, peer_key1, peer_key2, peer_u, peer_v)
    assert x.shape[0] == 1 and all(t.shape[0] == 1 for t in params)
    p = {name: t[0] for name, t in zip(_PARAM_NAMES, params)}
    return _layer(x[0], mem[0], positions[0], p)[None]
```

```python
import functools

import numpy as np
import jax
import jax.numpy as jnp
from jax import lax
from jax.experimental import pallas as pl
from jax.experimental.pallas import tpu as pltpu

F32 = jnp.float32
I32 = jnp.int32
MXU_DTYPE = jnp.bfloat16
HI = lax.Precision.HIGHEST

D_MODEL = 1024
ATT_HEADS, ATT_HD = 8, 64
ATT_W = ATT_HEADS * ATT_HD
IDX_HEADS, IDX_HD = 4, 64
TOPK_MAX = 256
RWKV_HEADS, RWKV_HD = 8, 64
RWKV_W = RWKV_HEADS * RWKV_HD
DECAY_LORA, AAA_LORA, GATE_LORA = 64, 64, 128
GN_EPS = 64e-5
MEM_HEADS, MEM_HD = 4, 128
MEM_W = MEM_HEADS * MEM_HD
N_BRANCH = 3
ROPE_THETA = 500000.0
ROPE_ROT = ATT_HD // 4
NORM_EPS = 1e-6
PEER_KEYS = 128
PEER_HEADS = 8
PEER_QDIM = 256
PEER_TOPK = 16

LOG2E = 1.4426950408889634
V_ROWS = 80
LANES = 128
INT_MIN = -(2 ** 31)
NEG_BIG = -1e30
VMEM_LIMIT = 56 * 1024 * 1024

COL_GATE = 0
COL_Q = 3072
COL_K = 3584
COL_V = 4096
COL_R = 4608
COL_RK = 5120
COL_RV = 5632
COL_MEMQ = 6144
COL_IQ = 6656
COL_LORA = 6912
COL_IKIW = 7168
PROJ_COLS = 7680


def _cparams(sem):
    return pltpu.CompilerParams(dimension_semantics=sem, vmem_limit_bytes=VMEM_LIMIT)


def _mm(a, b):
    return jnp.dot(a.astype(MXU_DTYPE), b.astype(MXU_DTYPE), preferred_element_type=F32)


def _mm_hi(a, b):
    return jnp.dot(a.astype(F32), b.astype(F32), preferred_element_type=F32, precision=HI)


def _normproj_kernel(x_ref, g_ref, w_ref, o_ref, xn_ref):
    @pl.when(pl.program_id(1) == 0)
    def _():
        x = x_ref[...]
        ms = jnp.mean(x * x, axis=-1, keepdims=True)
        xn_ref[...] = (x * lax.rsqrt(ms + NORM_EPS) * g_ref[...]).astype(xn_ref.dtype)

    o_ref[...] = jnp.dot(xn_ref[...], w_ref[...], preferred_element_type=F32)


def _norm_proj(x, gain, w, tm, tn):
    S, D = x.shape
    N = w.shape[1]
    return pl.pallas_call(
        _normproj_kernel,
        grid=(S // tm, N // tn),
        in_specs=[
            pl.BlockSpec((tm, D), lambda i, j: (i, 0)),
            pl.BlockSpec((1, D), lambda i, j: (0, 0)),
            pl.BlockSpec((D, tn), lambda i, j: (0, j)),
        ],
        out_specs=pl.BlockSpec((tm, tn), lambda i, j: (i, j)),
        out_shape=jax.ShapeDtypeStruct((S, N), F32),
        scratch_shapes=[pltpu.VMEM((tm, D), w.dtype)],
        compiler_params=_cparams(("parallel", "arbitrary")),
    )(x, gain.reshape(1, D), w)


def _pack_w_in(w_in):
    D = w_in.shape[0]
    o = 0
    parts = {}
    for name, width in (("q", ATT_W), ("k", ATT_W), ("v", ATT_W), ("iq", IDX_HEADS * IDX_HD),
                        ("ik", IDX_HD), ("iw", IDX_HEADS), ("r", RWKV_W), ("rk", RWKV_W),
                        ("rv", RWKV_W), ("lora", DECAY_LORA + AAA_LORA + GATE_LORA),
                        ("memq", MEM_W), ("gate", N_BRANCH * D_MODEL)):
        parts[name] = w_in[:, o:o + width]
        o += width
    ikiw = jnp.concatenate([parts["ik"], parts["iw"]], axis=1)
    ikiw = jnp.pad(ikiw, ((0, 0), (0, LANES - ikiw.shape[1])))
    packed = jnp.concatenate(
        [parts["gate"], parts["q"], parts["k"], parts["v"], parts["r"], parts["rk"], parts["rv"],
         parts["memq"], parts["iq"], parts["lora"], ikiw], axis=1)
    packed = jnp.pad(packed, ((0, 0), (0, PROJ_COLS - packed.shape[1])))
    return packed.astype(MXU_DTYPE)


def _rope_tables():
    d = np.arange(LANES) % ATT_HD
    inv_freq = 1.0 / (ROPE_THETA ** (np.arange(0, ROPE_ROT, 2, dtype=np.float32) / ROPE_ROT))
    half = ROPE_ROT // 2
    tab = np.zeros((8, LANES), np.float32)
    tab[0] = np.where(d < ROPE_ROT, inv_freq[d % half], 0.0)
    tab[1] = np.where(d < half, -1.0, 0.0)
    tab[2] = np.where((d >= half) & (d < ROPE_ROT), 1.0, 0.0)
    return jnp.asarray(tab)


def _seg_mean_matrix(width, seg):
    g = (np.arange(width)[:, None] // seg) == (np.arange(width)[None, :] // seg)
    return jnp.asarray(g.astype(np.float32) / seg)


def _rope(x, cos, sina, sinb):
    W = x.shape[1]
    reps = W // LANES
    if reps > 1:
        cos = jnp.concatenate([cos] * reps, axis=1)
        sina = jnp.concatenate([sina] * reps, axis=1)
        sinb = jnp.concatenate([sinb] * reps, axis=1)
    half = ROPE_ROT // 2
    up = pltpu.roll(x, W - half, axis=1)
    down = pltpu.roll(x, half, axis=1)
    return x * cos + up * sina + down * sinb


def _att_prep_kernel(q_ref, k_ref, v_ref, iq_ref, ikiw_ref, pos_ref, tab_ref, seg_ref,
                     qg_ref, kg_ref, qT_ref, kh_ref, vT_ref, iqT_ref, ik_ref, iwT_ref):
    tab = tab_ref[...]
    ang = pos_ref[...].astype(F32) * tab[0:1, :]
    cos = jnp.cos(ang)
    sin = jnp.sin(ang)
    sina = sin * tab[1:2, :]
    sinb = sin * tab[2:3, :]
    seg = seg_ref[...]

    def head_norm(x, g):
        ms = _mm_hi(x * x, seg)
        return x * lax.rsqrt(ms + NORM_EPS) * g

    q = _rope(head_norm(q_ref[...], qg_ref[...]), cos, sina, sinb) * (ATT_HD ** -0.5 * LOG2E)
    k = _rope(head_norm(k_ref[...], kg_ref[...]), cos, sina, sinb)
    v = v_ref[...]
    iq = _rope(iq_ref[...], cos, sina, sinb)
    ik = _rope(ikiw_ref[...], cos, sina, sinb)
    for h in range(ATT_HEADS):
        kh_ref[h] = k[:, h * ATT_HD:(h + 1) * ATT_HD].astype(kh_ref.dtype)
    qT_ref[...] = q.T.astype(qT_ref.dtype)
    vT = v.T
    tq = vT.shape[1]
    pad = jnp.concatenate([jnp.ones((1, tq), F32), jnp.zeros((V_ROWS - ATT_HD - 1, tq), F32)], axis=0)
    for h in range(ATT_HEADS):
        vT_ref[h] = jnp.concatenate([vT[h * ATT_HD:(h + 1) * ATT_HD, :], pad],
                                    axis=0).astype(vT_ref.dtype)
    iqT_ref[...] = iq.T.astype(iqT_ref.dtype)
    ik_ref[...] = ik[:, :IDX_HD].astype(ik_ref.dtype)
    idx_scale = (IDX_HEADS ** -0.5) * (IDX_HD ** -0.5)
    iwT_ref[...] = ikiw_ref[...].T[IDX_HD:IDX_HD + 8, :] * idx_scale


def _att_prep(proj, positions, att_q_gain, att_k_gain, tq):
    S = proj.shape[0]
    qg = jnp.tile(att_q_gain.reshape(1, ATT_HD), (1, ATT_HEADS))
    kg = jnp.tile(att_k_gain.reshape(1, ATT_HD), (1, ATT_HEADS))
    col = lambda off, w: (lambda i: (i, off // w))
    const = lambda i: (0, 0)
    return pl.pallas_call(
        _att_prep_kernel,
        grid=(S // tq,),
        in_specs=[
            pl.BlockSpec((tq, ATT_W), col(COL_Q, ATT_W)),
            pl.BlockSpec((tq, ATT_W), col(COL_K, ATT_W)),
            pl.BlockSpec((tq, ATT_W), col(COL_V, ATT_W)),
            pl.BlockSpec((tq, IDX_HEADS * IDX_HD), col(COL_IQ, IDX_HEADS * IDX_HD)),
            pl.BlockSpec((tq, LANES), col(COL_IKIW, LANES)),
            pl.BlockSpec((tq, 1), lambda i: (i, 0)),
            pl.BlockSpec((8, LANES), const),
            pl.BlockSpec((ATT_W, ATT_W), const),
            pl.BlockSpec((1, ATT_W), const),
            pl.BlockSpec((1, ATT_W), const),
        ],
        out_specs=[
            pl.BlockSpec((ATT_W, tq), lambda i: (0, i)),
            pl.BlockSpec((ATT_HEADS, tq, ATT_HD), lambda i: (0, i, 0)),
            pl.BlockSpec((ATT_HEADS, V_ROWS, tq), lambda i: (0, 0, i)),
            pl.BlockSpec((IDX_HEADS * IDX_HD, tq), lambda i: (0, i)),
            pl.BlockSpec((tq, IDX_HD), lambda i: (i, 0)),
            pl.BlockSpec((8, tq), lambda i: (0, i)),
        ],
        out_shape=[
            jax.ShapeDtypeStruct((ATT_W, S), MXU_DTYPE),
            jax.ShapeDtypeStruct((ATT_HEADS, S, ATT_HD), MXU_DTYPE),
            jax.ShapeDtypeStruct((ATT_HEADS, V_ROWS, S), MXU_DTYPE),
            jax.ShapeDtypeStruct((IDX_HEADS * IDX_HD, S), MXU_DTYPE),
            jax.ShapeDtypeStruct((S, IDX_HD), MXU_DTYPE),
            jax.ShapeDtypeStruct((8, S), F32),
        ],
        compiler_params=_cparams(("parallel",)),
    )(proj, proj, proj, proj, proj, positions.reshape(S, 1), _rope_tables(),
      _seg_mean_matrix(ATT_W, ATT_HD), qg, kg)


KEY_NEG_INF = -2139095041


def _key_to_float(key):
    f = pltpu.bitcast(key ^ ((key >> 31) & 0x7FFFFFFF), F32)
    return jnp.where(key < KEY_NEG_INF, -jnp.inf, f)


SUB_ACC = 64


LIST_ROWS = 256
LIST_DEPTH = 10


def _sel_kernel(iqT_ref, iwT_ref, ik_ref, tri_ref, bias_ref, sc_ref, cand_ref, *, tq, tk, seq,
                nsel):
    i = pl.program_id(0)
    q0 = i * tq
    nc = (q0 + tq + tk - 1) // tk
    qidx = q0 + lax.broadcasted_iota(I32, (tk, tq), 1)
    krow = lax.broadcasted_iota(I32, (tk, tq), 0)
    iw = iwT_ref[...]

    def score_chunk(c, carry):
        c0 = pl.multiple_of(c * tk, tk)
        ikc = ik_ref[pl.ds(c0, tk), :]
        sc = None
        for h in range(IDX_HEADS):
            logit = jnp.dot(ikc, iqT_ref[h * IDX_HD:(h + 1) * IDX_HD, :],
                            preferred_element_type=F32)
            t = jnp.maximum(logit, 0.0) * iw[h:h + 1, :]
            sc = t if sc is None else sc + t
        sc = jnp.where(c0 + krow <= qidx, sc, -jnp.inf)
        sc_ref[pl.ds(c0, tk), :] = sc
        for part in range(tk // LIST_ROWS):
            x = sc[part * LIST_ROWS:(part + 1) * LIST_ROWS, :]
            for d in range(LIST_DEPTH):
                rows = slice(d * LIST_ROWS, (d + 1) * LIST_ROWS)
                kept = cand_ref[rows, :]
                cand_ref[rows, :] = jnp.maximum(kept, x)
                x = jnp.minimum(kept, x)
        return carry

    cand_ref[...] = jnp.full(cand_ref.shape, -jnp.inf, F32)
    lax.fori_loop(0, nc, score_chunk, 0)

    def count(pred, n_out=1, src_ref=sc_ref, n_chunks=nc):
        sub = SUB_ACC if n_out == 1 else SUB_ACC // 4

        def body(c, accs):
            c0 = pl.multiple_of(c * tk, tk)
            ms = pred(src_ref[pl.ds(c0, tk), :])
            return tuple(a + jnp.sum(m.reshape(tk // sub, sub, tq), axis=0)
                         for a, m in zip(accs, ms))
        accs = lax.fori_loop(0, n_chunks, body, (jnp.zeros((sub, tq), F32),) * n_out)
        return [_col_reduce(a, jnp.add) for a in accs]

    zero = jnp.zeros((1, tq), I32)

    def kth_largest(src_ref, n_chunks):
        def count_ge(cand):
            return count(lambda s: (jnp.where(s >= cand, 1.0, 0.0),), 1, src_ref, n_chunks)[0]

        prefix = jnp.where(count_ge(0.0) >= nsel, zero, zero + INT_MIN)

        def bit_body(b, prefix):
            cand = prefix | jnp.left_shift(jnp.int32(1), 30 - b)
            return jnp.where(count_ge(_key_to_float(cand)) >= nsel, cand, prefix)

        return _key_to_float(lax.fori_loop(0, 31, bit_body, prefix))

    n_cand_chunks = LIST_DEPTH * LIST_ROWS // tk
    tau_cand = kth_largest(cand_ref, n_cand_chunks)
    last_kept = _col_reduce(cand_ref[(LIST_DEPTH - 1) * LIST_ROWS:, :], jnp.maximum)
    lists_ok = jnp.min(jnp.where(last_kept <= tau_cand, 1.0, 0.0)) > 0.5
    tau = lax.cond(lists_ok, lambda: tau_cand, lambda: kth_largest(sc_ref, nc))

    def is_neg(s):
        return pltpu.bitcast(s, I32) < 0

    def tie_classes(s):
        eq = s == tau
        neg = is_neg(s)
        return (jnp.where(eq, jnp.where(neg, 0.0, 1.0), 0.0),
                jnp.where(eq, jnp.where(neg, 1.0, 0.0), 0.0))

    n_gt, n_eq_pos, n_eq_neg = count(
        lambda s: (jnp.where(s > tau, 1.0, 0.0),) + tie_classes(s), n_out=3)
    need = nsel - n_gt
    mixed = jnp.max(jnp.minimum(n_eq_pos, n_eq_neg)) > 0
    tri = tri_ref[...]

    def write_pass(both_signs):
        def body(c, carry):
            c0 = pl.multiple_of(c * tk, tk)
            s = sc_ref[pl.ds(c0, tk), :]
            if both_signs:
                ind_pos, ind_neg = tie_classes(s)
                cum_pos = carry[0] + _mm(tri, ind_pos)
                cum_neg = carry[1] + _mm(tri, ind_neg)
                rank = jnp.where(is_neg(s), n_eq_pos + cum_neg, cum_pos)
                carry = (cum_pos[tk - 1:tk, :], cum_neg[tk - 1:tk, :])
            else:
                rank = carry[0] + _mm(tri, jnp.where(s == tau, 1.0, 0.0))
                carry = (rank[tk - 1:tk, :], carry[1])
            tie = jnp.where(rank <= need, 0.0, NEG_BIG)
            b = jnp.where(s > tau, 0.0, jnp.where(s == tau, tie, NEG_BIG))
            b = jnp.where(c0 + krow <= qidx, b, NEG_BIG)
            bias_ref[pl.ds(c0, tk), :] = b.astype(bias_ref.dtype)
            return carry
        zf = jnp.zeros((1, tq), F32)
        lax.fori_loop(0, nc, body, (zf, zf))

    lax.cond(mixed, lambda: write_pass(True), lambda: write_pass(False))

    def fill_chunk(c, carry):
        c0 = pl.multiple_of(c * tk, tk)
        bias_ref[pl.ds(c0, tk), :] = jnp.full((tk, tq), NEG_BIG, bias_ref.dtype)
        return carry

    lax.fori_loop(nc, seq // tk, fill_chunk, 0)


def _dsa_select(iqT, iwT, ik, tq, tk):
    S = ik.shape[0]
    nsel = min(TOPK_MAX, S // 4)
    kern = functools.partial(_sel_kernel, tq=tq, tk=tk, seq=S, nsel=nsel)
    return pl.pallas_call(
        kern,
        grid=(S // tq,),
        in_specs=[
            pl.BlockSpec((IDX_HEADS * IDX_HD, tq), lambda i: (0, i)),
            pl.BlockSpec((8, tq), lambda i: (0, i)),
            pl.BlockSpec((S, IDX_HD), lambda i: (0, 0)),
            pl.BlockSpec((tk, tk), lambda i: (0, 0)),
        ],
        out_specs=pl.BlockSpec((S, tq), lambda i: (0, i)),
        out_shape=jax.ShapeDtypeStruct((S, S), jnp.bfloat16),
        scratch_shapes=[pltpu.VMEM((S, tq), F32),
                        pltpu.VMEM((LIST_DEPTH * LIST_ROWS, tq), F32)],
        compiler_params=_cparams(("parallel",)),
    )(iqT, iwT, ik, jnp.tril(jnp.ones((tk, tk), MXU_DTYPE)))


def _col_reduce(x, op):
    rows = x.shape[0]
    while rows > 8:
        rows //= 2
        x = op(x[:rows], x[rows:])
    red = jnp.max if op is jnp.maximum else jnp.sum
    return red(x, axis=0, keepdims=True)


def _att_kernel(qi_ref, kj_ref, qT_ref, k_ref, vT_ref, bias_ref, o_ref, m_ref, acc_ref,
                *, tq, tk):
    t = pl.program_id(0)
    i = qi_ref[t]
    j = kj_ref[t]
    last = ((i + 1) * tq - 1) // tk

    @pl.when(j == 0)
    def _():
        m_ref[...] = jnp.full(m_ref.shape, -jnp.inf, F32)
        acc_ref[...] = jnp.zeros(acc_ref.shape, F32)

    bias = bias_ref[...].astype(F32)
    hs = lambda h: slice(h * ATT_HD, (h + 1) * ATT_HD)
    m_all = m_ref[...]
    m_out = []
    halves = (slice(0, tk // 2), slice(tk // 2, tk))
    for h in range(ATT_HEADS):
        qh = qT_ref[hs(h), :]
        ss = [jnp.dot(k_ref[h, ks, :], qh, preferred_element_type=F32) + bias[ks, :]
              for ks in halves]
        m_prev = m_all[h:h + 1, :]
        m_new = jnp.maximum(m_prev, jnp.maximum(_col_reduce(ss[0], jnp.maximum),
                                                _col_reduce(ss[1], jnp.maximum)))
        alpha = jnp.exp2(m_prev - m_new)
        m_out.append(m_new)
        pv = [jnp.dot(vT_ref[h, :, ks], jnp.exp2((s - m_new).astype(vT_ref.dtype)),
                      preferred_element_type=F32) for ks, s in zip(halves, ss)]
        acc_ref[h] = alpha * acc_ref[h] + pv[0] + pv[1]
    m_ref[...] = jnp.concatenate(m_out, axis=0)

    @pl.when(j == last)
    def _():
        out = jnp.concatenate(
            [acc_ref[h, :ATT_HD, :] / acc_ref[h, ATT_HD:ATT_HD + 1, :]
             for h in range(ATT_HEADS)], axis=0)
        o_ref[...] = out.T


def _dsa_attend(qT, kh, vT, bias, tq, tk):
    S = qT.shape[1]
    pairs = [(i, j) for i in range(S // tq) for j in range(((i + 1) * tq - 1) // tk + 1)]
    qi = jnp.asarray([p[0] for p in pairs], I32)
    kj = jnp.asarray([p[1] for p in pairs], I32)
    grid_spec = pltpu.PrefetchScalarGridSpec(
        num_scalar_prefetch=2,
        grid=(len(pairs),),
        in_specs=[
            pl.BlockSpec((ATT_W, tq), lambda t, qi, kj: (0, qi[t])),
            pl.BlockSpec((ATT_HEADS, tk, ATT_HD), lambda t, qi, kj: (0, kj[t], 0)),
            pl.BlockSpec((ATT_HEADS, V_ROWS, tk), lambda t, qi, kj: (0, 0, kj[t])),
            pl.BlockSpec((tk, tq), lambda t, qi, kj: (kj[t], qi[t])),
        ],
        out_specs=pl.BlockSpec((tq, ATT_W), lambda t, qi, kj: (qi[t], 0)),
        scratch_shapes=[
            pltpu.VMEM((ATT_HEADS, tq), F32),
            pltpu.VMEM((ATT_HEADS, V_ROWS, tq), F32),
        ],
    )
    return pl.pallas_call(
        functools.partial(_att_kernel, tq=tq, tk=tk),
        grid_spec=grid_spec,
        out_shape=jax.ShapeDtypeStruct((S, ATT_W), F32),
        compiler_params=_cparams(("arbitrary",)),
    )(qi, kj, qT, kh, vT, bias)


CHUNK = 64


def _shift_rows(x, prev8, first):
    prev_row = jnp.where(first, 0.0, prev8[7:8, :])
    row = lax.broadcasted_iota(I32, x.shape, 0)
    return jnp.where(row == 0, prev_row, pltpu.roll(x, 1, axis=0))


def _softplus(z):
    return jnp.maximum(z, 0.0) + jnp.log1p(jnp.exp(-jnp.abs(z)))


def _rwkv_prep_kernel(r_ref, k_ref, v_ref, lo_ref, rp_ref, kp_ref, vp_ref, lop_ref,
                      mur_ref, muk_ref, muv_ref, mulo_ref, w0_ref, w2_ref, a0_ref, a2_ref,
                      g2_ref, kk_ref, ka_ref, rk_ref, seg_ref, tri_ref, ones_ref, end_ref,
                      rt_ref, kt_ref, bt_ref, at_ref, kh_ref, bh_ref, vh_ref, gam_ref,
                      bonus_ref, g_ref):
    first = pl.program_id(0) == 0

    def mix(x_ref, p_ref, mu_ref):
        x = x_ref[...]
        return x + (_shift_rows(x, p_ref[...], first) - x) * mu_ref[...]

    r = mix(r_ref, rp_ref, mur_ref)
    k = mix(k_ref, kp_ref, muk_ref)
    v = mix(v_ref, vp_ref, muv_ref)
    lo = mix(lo_ref, lop_ref, mulo_ref)
    wd = lo[:, :DECAY_LORA]
    ad = lo[:, DECAY_LORA:DECAY_LORA + AAA_LORA]
    gd = lo[:, DECAY_LORA + AAA_LORA:]
    w_log = -_softplus(-(w0_ref[...] + _mm(jnp.tanh(wd), w2_ref[...]))) - 0.5
    lw = -jnp.exp(w_log)
    a = jax.nn.sigmoid(a0_ref[...] + _mm(ad, a2_ref[...]))
    g_ref[...] = _mm(jax.nn.sigmoid(gd), g2_ref[...])
    seg = seg_ref[...]
    kk = k * kk_ref[...]
    kk = kk / jnp.maximum(jnp.sqrt(_mm_hi(kk * kk, seg)), 1e-12)
    k = k * (1.0 + (a - 1.0) * ka_ref[...])
    bonus_ref[...] = _mm_hi(r * k * rk_ref[...], seg) * v
    avec = -kk
    bvec = kk * a
    cs = _mm_hi(tri_ref[...], lw)
    cs_end = _mm_hi(ones_ref[...], lw)
    e_neg = jnp.exp(-cs)
    e_end = jnp.exp(cs_end - cs)
    outs = ((rt_ref, r * jnp.exp(cs)), (kt_ref, k * e_neg), (bt_ref, bvec * e_neg),
            (at_ref, avec * jnp.exp(cs - lw)), (kh_ref, k * e_end), (bh_ref, bvec * e_end),
            (vh_ref, v), (gam_ref, jnp.exp(_mm_hi(end_ref[...], lw))))
    for ref, val in outs:
        for h in range(RWKV_HEADS):
            ref[h] = val[:, h * RWKV_HD:(h + 1) * RWKV_HD]


def _rwkv_prep(proj, p, tq):
    S = proj.shape[0]
    nch = tq // CHUNK
    col = lambda off, w: (lambda i: (i, off // w))
    prev = lambda off, w: (lambda i: (jnp.maximum(i * (tq // 8) - 1, 0), off // w))
    const = lambda i: (0, 0)
    lw_ = DECAY_LORA + AAA_LORA + GATE_LORA
    mu = p["rwkv_mu"]
    row = lambda t: t.reshape(1, -1)
    t_idx = np.arange(tq)
    same = (t_idx[:, None] // CHUNK) == (t_idx[None, :] // CHUNK)
    tri = jnp.asarray((same & (t_idx[None, :] <= t_idx[:, None])).astype(np.float32))
    ones = jnp.asarray(same.astype(np.float32))
    end = jnp.asarray(((t_idx[None, :] // CHUNK) == np.arange(nch)[:, None]).astype(np.float32))
    seg = _seg_mean_matrix(RWKV_W, RWKV_HD) * RWKV_HD
    hm = jax.ShapeDtypeStruct((RWKV_HEADS, S, RWKV_HD), F32)
    hm_spec = pl.BlockSpec((RWKV_HEADS, tq, RWKV_HD), lambda i: (0, i, 0))
    wide = jax.ShapeDtypeStruct((S, RWKV_W), F32)
    wide_spec = pl.BlockSpec((tq, RWKV_W), lambda i: (i, 0))
    vec = lambda w: pl.BlockSpec((1, w), const)
    return pl.pallas_call(
        _rwkv_prep_kernel,
        grid=(S // tq,),
        in_specs=[
            pl.BlockSpec((tq, RWKV_W), col(COL_R, RWKV_W)),
            pl.BlockSpec((tq, RWKV_W), col(COL_RK, RWKV_W)),
            pl.BlockSpec((tq, RWKV_W), col(COL_RV, RWKV_W)),
            pl.BlockSpec((tq, lw_), col(COL_LORA, lw_)),
            pl.BlockSpec((8, RWKV_W), prev(COL_R, RWKV_W)),
            pl.BlockSpec((8, RWKV_W), prev(COL_RK, RWKV_W)),
            pl.BlockSpec((8, RWKV_W), prev(COL_RV, RWKV_W)),
            pl.BlockSpec((8, lw_), prev(COL_LORA, lw_)),
            vec(RWKV_W), vec(RWKV_W), vec(RWKV_W), vec(lw_),
            vec(RWKV_W), pl.BlockSpec((DECAY_LORA, RWKV_W), const),
            vec(RWKV_W), pl.BlockSpec((AAA_LORA, RWKV_W), const),
            pl.BlockSpec((GATE_LORA, RWKV_W), const),
            vec(RWKV_W), vec(RWKV_W), vec(RWKV_W),
            pl.BlockSpec((RWKV_W, RWKV_W), const),
            pl.BlockSpec((tq, tq), const), pl.BlockSpec((tq, tq), const),
            pl.BlockSpec((nch, tq), const),
        ],
        out_specs=[hm_spec] * 7 + [pl.BlockSpec((RWKV_HEADS, nch, RWKV_HD), lambda i: (0, i, 0)),
                                   wide_spec, wide_spec],
        out_shape=[hm] * 7 + [jax.ShapeDtypeStruct((RWKV_HEADS, S // CHUNK, RWKV_HD), F32),
                              wide, wide],
        compiler_params=_cparams(("parallel",)),
    )(proj, proj, proj, proj, proj, proj, proj, proj,
      row(mu[:RWKV_W]), row(mu[RWKV_W:2 * RWKV_W]), row(mu[2 * RWKV_W:3 * RWKV_W]),
      row(mu[3 * RWKV_W:]), row(p["rwkv_w0"]), p["rwkv_w2"], row(p["rwkv_a0"]), p["rwkv_a2"],
      p["rwkv_g2"], row(p["rwkv_k_k"]), row(p["rwkv_k_a"]), row(p["rwkv_r_k"]), seg, tri, ones, end)


def _bmm(a, b, dims):
    return jnp.einsum(dims, a, b, preferred_element_type=F32, precision=HI)


def _bmm1(a, b, dims):
    return jnp.einsum(dims, a.astype(MXU_DTYPE), b.astype(MXU_DTYPE), preferred_element_type=F32)


def _rwkv_chunk_kernel(rt_ref, kt_ref, bt_ref, at_ref, kh_ref, bh_ref, v_ref, gam_ref,
                       p_ref, q_ref, rw_ref, y0_ref, *, nch):
    L, N = CHUNK, RWKV_HD
    ri = lax.broadcasted_iota(I32, (nch, L, L), 1)
    ci = lax.broadcasted_iota(I32, (nch, L, L), 2)
    eye_n = (lax.broadcasted_iota(I32, (nch, N, N), 1)
             == lax.broadcasted_iota(I32, (nch, N, N), 2)).astype(F32)
    for h in range(RWKV_HEADS):
        ld = lambda ref: ref[h].reshape(nch, L, N)
        rt, kt, bt, at, kh, bh, v = (ld(x) for x in (rt_ref, kt_ref, bt_ref, at_ref, kh_ref,
                                                      bh_ref, v_ref))
        gam = gam_ref[h].reshape(nch, 1, N)
        mm = _bmm1
        a_ab = jnp.where(ci < ri, mm(at, bt, "cld,cmd->clm"), 0.0)
        a_ak = jnp.where(ci < ri, mm(at, kt, "cld,cmd->clm"), 0.0)
        m_rk = jnp.where(ci <= ri, mm(rt, kt, "cld,cmd->clm"), 0.0)
        m_rb = jnp.where(ci <= ri, mm(rt, bt, "cld,cmd->clm"), 0.0)
        rhs = jnp.concatenate([at, mm(a_ak, v, "clm,cmd->cld")], axis=2)
        pw = a_ab
        step = 1
        while True:
            rhs = rhs + mm(pw, rhs, "clm,cmd->cld")
            step *= 2
            if step >= L:
                break
            pw = mm(pw, pw, "clm,cmn->cln")
        w, u0 = rhs[:, :, :N], rhs[:, :, N:]
        p_ref[:, h] = eye_n * gam + mm(bh, w, "cld,cle->cde")
        q_ref[:, h] = mm(kh, v, "cld,cle->cde") + mm(bh, u0, "cld,cle->cde")
        rw_ref[h] = (rt + mm(m_rb, w, "clm,cmd->cld")).reshape(nch * L, N)
        y0_ref[h] = (mm(m_rk, v, "clm,cmd->cld") + mm(m_rb, u0, "clm,cmd->cld")).reshape(nch * L, N)


def _rwkv_chunks(rt, kt, bt, at, kh, bh, vh, gam, tt):
    S = rt.shape[1]
    nch = tt // CHUNK
    hm_spec = pl.BlockSpec((RWKV_HEADS, tt, RWKV_HD), lambda i: (0, i, 0))
    hm = jax.ShapeDtypeStruct((RWKV_HEADS, S, RWKV_HD), F32)
    sq_spec = pl.BlockSpec((nch, RWKV_HEADS, RWKV_HD, RWKV_HD), lambda i: (i, 0, 0, 0))
    sq = jax.ShapeDtypeStruct((S // CHUNK, RWKV_HEADS, RWKV_HD, RWKV_HD), F32)
    return pl.pallas_call(
        functools.partial(_rwkv_chunk_kernel, nch=nch),
        grid=(S // tt,),
        in_specs=[hm_spec] * 7 + [pl.BlockSpec((RWKV_HEADS, nch, RWKV_HD), lambda i: (0, i, 0))],
        out_specs=[sq_spec, sq_spec, hm_spec, hm_spec],
        out_shape=[sq, sq, hm, hm],
        compiler_params=_cparams(("parallel",)),
    )(rt, kt, bt, at, kh, bh, vh, gam)


def _rwkv_scan_kernel(p_ref, q_ref, rw_ref, y0_ref, bonus_ref, g_ref, seg_ref, lnw_ref, lnb_ref,
                      o_ref, st_ref, *, nch):
    @pl.when(pl.program_id(0) == 0)
    def _():
        st_ref[...] = jnp.zeros(st_ref.shape, F32)

    L = CHUNK
    st = st_ref[...]
    ys = []
    for c in range(nch):
        rw = rw_ref[:, c * L:(c + 1) * L, :]
        ys.append(_bmm(rw, st, "hld,hde->hle") + y0_ref[:, c * L:(c + 1) * L, :])
        st = _bmm(p_ref[c], st, "hjk,hki->hji") + q_ref[c]
    st_ref[...] = st
    y = jnp.concatenate(ys, axis=1)
    y = jnp.concatenate([y[h] for h in range(RWKV_HEADS)], axis=1)
    seg = seg_ref[...]
    mean = _mm_hi(y, seg)
    d = y - mean
    var = _mm_hi(d * d, seg)
    y = d * lax.rsqrt(var + GN_EPS) * lnw_ref[...] + lnb_ref[...]
    o_ref[...] = (y + bonus_ref[...]) * g_ref[...]


def _rwkv_scan(pm, qm, rw, y0, bonus, g, ln_w, ln_b, tt):
    S = rw.shape[1]
    nch = tt // CHUNK
    hm_spec = pl.BlockSpec((RWKV_HEADS, tt, RWKV_HD), lambda i: (0, i, 0))
    sq_spec = pl.BlockSpec((nch, RWKV_HEADS, RWKV_HD, RWKV_HD), lambda i: (i, 0, 0, 0))
    wide_spec = pl.BlockSpec((tt, RWKV_W), lambda i: (i, 0))
    const = lambda i: (0, 0)
    return pl.pallas_call(
        functools.partial(_rwkv_scan_kernel, nch=nch),
        grid=(S // tt,),
        in_specs=[sq_spec, sq_spec, hm_spec, hm_spec, wide_spec, wide_spec,
                  pl.BlockSpec((RWKV_W, RWKV_W), const), pl.BlockSpec((1, RWKV_W), const),
                  pl.BlockSpec((1, RWKV_W), const)],
        out_specs=wide_spec,
        out_shape=jax.ShapeDtypeStruct((S, RWKV_W), F32),
        scratch_shapes=[pltpu.VMEM((RWKV_HEADS, RWKV_HD, RWKV_HD), F32)],
        compiler_params=_cparams(("arbitrary",)),
    )(pm, qm, rw, y0, bonus, g, _seg_mean_matrix(RWKV_W, RWKV_HD), ln_w.reshape(1, -1),
      ln_b.reshape(1, -1))


def _rwkv_time_mix(proj, p):
    outs = _rwkv_prep(proj, p, 512)
    rt, kt, bt, at, kh, bh, vh, gam, bonus, g = outs
    pm, qm, rw, y0 = _rwkv_chunks(rt, kt, bt, at, kh, bh, vh, gam, 512)
    return _rwkv_scan(pm, qm, rw, y0, bonus, g, p["rwkv_ln_w"], p["rwkv_ln_b"], 512)


def _lane_rmsnorm(x, gain):
    return x * lax.rsqrt(jnp.mean(x * x, axis=-1, keepdims=True) + NORM_EPS) * gain


def _mem_kv_kernel(mem_ref, g_ref, w_ref, kg_ref, kmT_ref, vm_ref):
    m = _lane_rmsnorm(mem_ref[...], g_ref[...])
    kv = _mm(m, w_ref[...])
    km = jnp.concatenate(
        [_lane_rmsnorm(kv[:, h * MEM_HD:(h + 1) * MEM_HD], kg_ref[...]) for h in range(MEM_HEADS)],
        axis=1)
    kmT_ref[...] = km.T.astype(kmT_ref.dtype)
    vm_ref[...] = kv[:, MEM_W:].astype(vm_ref.dtype)


def _mem_kv(mem, gain, w_kv, k_gain):
    M = mem.shape[0]
    return pl.pallas_call(
        _mem_kv_kernel,
        out_shape=[jax.ShapeDtypeStruct((MEM_W, M), MXU_DTYPE),
                   jax.ShapeDtypeStruct((M, MEM_W), MXU_DTYPE)],
        compiler_params=pltpu.CompilerParams(vmem_limit_bytes=VMEM_LIMIT),
    )(mem, gain.reshape(1, -1), w_kv.astype(MXU_DTYPE), k_gain.reshape(1, -1))


def _merge_kernel(x_ref, yatt_ref, yrwkv_ref, memq_ref, gate_ref, kmT_ref, vm_ref, qg_ref,
                  gb_ref, wup_ref, wout_ref, o_ref):
    q = memq_ref[...]
    heads = []
    for h in range(MEM_HEADS):
        sl = slice(h * MEM_HD, (h + 1) * MEM_HD)
        qn = _lane_rmsnorm(q[:, sl], qg_ref[...])
        s = _mm(qn, kmT_ref[sl, :]) * (MEM_HD ** -0.5)
        s = s - jnp.max(s, axis=-1, keepdims=True)
        e = jnp.exp(s)
        p = e / jnp.sum(e, axis=-1, keepdims=True)
        heads.append(_mm(p, vm_ref[:, sl]))
    ymem = jnp.concatenate(heads, axis=1)
    merged = None
    for c, y in enumerate((yatt_ref[...], yrwkv_ref[...], ymem)):
        up = _mm(y, wup_ref[c])
        gate = jax.nn.sigmoid(gate_ref[:, c * D_MODEL:(c + 1) * D_MODEL] + gb_ref[c:c + 1, :])
        merged = gate * up if merged is None else merged + gate * up
    o_ref[...] = x_ref[...] + _mm(merged, wout_ref[...])


def _merge(x, y_att, y_rwkv, proj, kmT, vm, mem_q_gain, gate_bias, w_up, w_out, tq):
    S = x.shape[0]
    M = vm.shape[0]
    gw = N_BRANCH * D_MODEL
    const2 = lambda i: (0, 0)
    return pl.pallas_call(
        _merge_kernel,
        grid=(S // tq,),
        in_specs=[
            pl.BlockSpec((tq, D_MODEL), lambda i: (i, 0)),
            pl.BlockSpec((tq, ATT_W), lambda i: (i, 0)),
            pl.BlockSpec((tq, RWKV_W), lambda i: (i, 0)),
            pl.BlockSpec((tq, MEM_W), lambda i: (i, COL_MEMQ // MEM_W)),
            pl.BlockSpec((tq, gw), lambda i: (i, COL_GATE // gw)),
            pl.BlockSpec((MEM_W, M), const2),
            pl.BlockSpec((M, MEM_W), const2),
            pl.BlockSpec((1, MEM_HD), const2),
            pl.BlockSpec((N_BRANCH, D_MODEL), const2),
            pl.BlockSpec((N_BRANCH, ATT_W, D_MODEL), lambda i: (0, 0, 0)),
            pl.BlockSpec((D_MODEL, D_MODEL), const2),
        ],
        out_specs=pl.BlockSpec((tq, D_MODEL), lambda i: (i, 0)),
        out_shape=jax.ShapeDtypeStruct((S, D_MODEL), F32),
        compiler_params=_cparams(("parallel",)),
    )(x, y_att, y_rwkv, proj, proj, kmT, vm, mem_q_gain.reshape(1, -1), gate_bias,
      w_up.astype(MXU_DTYPE), w_out.astype(MXU_DTYPE))


NO_RANK = 255.0


def _pop_max(cur, rows):
    m = jnp.max(cur, axis=0, keepdims=True)
    first = jnp.min(jnp.where(cur == m, rows, cur.shape[0]), axis=0, keepdims=True)
    hit = rows == first
    return m, hit, jnp.where(hit, -jnp.inf, cur)


def _peer_route_kernel(h_ref, g_ref, wq_ref, k1_ref, k2_ref,
                       xn_ref, bsel_ref, r2_ref, p1_ref, p2_ref):
    tq = h_ref.shape[0]
    xn = _lane_rmsnorm(h_ref[...], g_ref[...]).astype(xn_ref.dtype)
    xn_ref[...] = xn
    qp = jnp.dot(xn, wq_ref[...], preferred_element_type=F32)
    nt = (((1,), (1,)), ((), ()))
    half = PEER_QDIM // 2
    rows = lax.broadcasted_iota(I32, (PEER_KEYS, tq), 0)
    pairs = [(a, b) for a in range(PEER_TOPK) for b in range(PEER_TOPK)
             if (a + 1) * (b + 1) <= PEER_TOPK]
    npad = -len(pairs) % 8
    crow = lax.broadcasted_iota(I32, (len(pairs) + npad, tq), 0)
    for h in range(PEER_HEADS):
        q1 = qp[:, h * PEER_QDIM:h * PEER_QDIM + half].astype(MXU_DTYPE)
        q2 = qp[:, h * PEER_QDIM + half:(h + 1) * PEER_QDIM].astype(MXU_DTYPE)
        s1 = lax.dot_general(k1_ref[...], q1, nt, preferred_element_type=F32)
        s2 = lax.dot_general(k2_ref[...], q2, nt, preferred_element_type=F32)
        tops, ranks = [], []
        for s in (s1, s2):
            cur, vals = s, []
            rank = jnp.full((PEER_KEYS, tq), NO_RANK, F32)
            for r in range(PEER_TOPK):
                m, hit, cur = _pop_max(cur, rows)
                rank = jnp.where(hit, float(r), rank)
                vals.append(m)
            tops.append(vals)
            ranks.append(rank)
        v1, v2 = tops
        rank1, rank2 = ranks
        cand = jnp.concatenate([v1[a] + v2[b] for a, b in pairs]
                               + [jnp.full((npad, tq), -jnp.inf, F32)], axis=0)
        top = v1[0] + v2[0]
        z = jnp.zeros((1, tq), F32)
        for _ in range(PEER_TOPK):
            tau, _, cand = _pop_max(cand, crow)
            z = z + jnp.exp(tau - top)
        bsel = jnp.zeros((PEER_KEYS, tq), F32)
        for a in range(PEER_TOPK):
            n_b = jnp.zeros((1, tq), F32)
            for b in [pb for pa, pb in pairs if pa == a]:
                n_b = n_b + jnp.where(v1[a] + v2[b] >= tau, 1.0, 0.0)
            bsel = jnp.where(rank1 == float(a), n_b, bsel)
        bsel_ref[h] = bsel
        r2_ref[h] = rank2.astype(r2_ref.dtype)
        p1_ref[h] = jnp.exp(s1 - v1[0]) / z
        p2_ref[h] = jnp.exp(s2 - v2[0]).astype(p2_ref.dtype)


def _peer_route(h2, gain, w_q, key1, key2, tq):
    S = h2.shape[0]
    const2 = lambda i: (0, 0)
    half = PEER_QDIM // 2
    kt = jax.ShapeDtypeStruct((PEER_HEADS, PEER_KEYS, S), F32)
    ktn = jax.ShapeDtypeStruct((PEER_HEADS, PEER_KEYS, S), MXU_DTYPE)
    kt_spec = pl.BlockSpec((PEER_HEADS, PEER_KEYS, tq), lambda i: (0, 0, i))
    return pl.pallas_call(
        _peer_route_kernel,
        grid=(S // tq,),
        in_specs=[
            pl.BlockSpec((tq, D_MODEL), lambda i: (i, 0)),
            pl.BlockSpec((1, D_MODEL), const2),
            pl.BlockSpec((D_MODEL, PEER_HEADS * PEER_QDIM), const2),
            pl.BlockSpec((PEER_KEYS, half), const2),
            pl.BlockSpec((PEER_KEYS, half), const2),
        ],
        out_specs=[pl.BlockSpec((tq, D_MODEL), lambda i: (i, 0)), kt_spec, kt_spec, kt_spec,
                   kt_spec],
        out_shape=[jax.ShapeDtypeStruct((S, D_MODEL), MXU_DTYPE), kt, ktn, kt, ktn],
        compiler_params=_cparams(("parallel",)),
    )(h2, gain.reshape(1, -1), w_q.astype(MXU_DTYPE), key1.astype(MXU_DTYPE),
      key2.astype(MXU_DTYPE))


def _peer_dense_kernel(xn_ref, u0_ref, unext_ref, vT_ref, bsel_ref, r2_ref, p1_ref, p2_ref, h_ref,
                       o_ref, acc_ref, hid_a_ref, hid_b_ref, x_ref, *, te, ne):
    j = pl.program_id(1)
    nt = (((1,), (1,)), ((), ()))

    @pl.when(j == 0)
    def _():
        acc_ref[...] = jnp.zeros(acc_ref.shape, F32)
        hid_a_ref[...] = lax.dot_general(u0_ref[...], xn_ref[...], nt, preferred_element_type=F32)

    def step(hid_ref, hid_next_ref):
        hid_next_ref[...] = lax.dot_general(unext_ref[...], xn_ref[...], nt,
                                            preferred_element_type=F32)
        hid = hid_ref[...]
        act = (0.5 * hid * (1.0 + lax.erf(hid * (2.0 ** -0.5)))).astype(x_ref.dtype)
        for el in range(te // PEER_KEYS):
            e1 = j * (te // PEER_KEYS) + el
            g = None
            for h in range(PEER_HEADS):
                n_b = bsel_ref[h, pl.ds(e1, 1), :].astype(x_ref.dtype)
                p1r = p1_ref[h, pl.ds(e1, 1), :].astype(x_ref.dtype)
                t = jnp.where(r2_ref[h] < n_b, p2_ref[h], 0.0) * p1r
                g = t if g is None else g + t
            sl = slice(el * PEER_KEYS, (el + 1) * PEER_KEYS)
            x_ref[sl, :] = g * act[sl, :]
        acc_ref[...] += jnp.dot(vT_ref[...], x_ref[...], preferred_element_type=F32)

    pl.when(j % 2 == 0)(lambda: step(hid_a_ref, hid_b_ref))
    pl.when(j % 2 == 1)(lambda: step(hid_b_ref, hid_a_ref))

    @pl.when(j == ne - 1)
    def _():
        o_ref[...] = h_ref[...] + acc_ref[...].T


def _peer_dense(h2, xn, u, v, bsel, r2, p1, p2, tm, te):
    S = h2.shape[0]
    NE = u.shape[0]
    ne = NE // te
    kt_spec = pl.BlockSpec((PEER_HEADS, PEER_KEYS, tm), lambda i, j: (0, 0, i))
    u_c = u.astype(MXU_DTYPE)
    return pl.pallas_call(
        functools.partial(_peer_dense_kernel, te=te, ne=ne),
        grid=(S // tm, ne),
        in_specs=[
            pl.BlockSpec((tm, D_MODEL), lambda i, j: (i, 0)),
            pl.BlockSpec((te, D_MODEL), lambda i, j: (0, 0)),
            pl.BlockSpec((te, D_MODEL), lambda i, j: (jnp.minimum(j + 1, ne - 1), 0)),
            pl.BlockSpec((D_MODEL, te), lambda i, j: (0, j)),
            kt_spec, kt_spec, kt_spec, kt_spec,
            pl.BlockSpec((tm, D_MODEL), lambda i, j: (i, 0)),
        ],
        out_specs=pl.BlockSpec((tm, D_MODEL), lambda i, j: (i, 0)),
        out_shape=jax.ShapeDtypeStruct((S, D_MODEL), F32),
        scratch_shapes=[pltpu.VMEM((D_MODEL, tm), F32), pltpu.VMEM((te, tm), F32),
                        pltpu.VMEM((te, tm), F32), pltpu.VMEM((te, tm), MXU_DTYPE)],
        compiler_params=_cparams(("parallel", "arbitrary")),
    )(xn, u_c, u_c, v.T.astype(MXU_DTYPE), bsel, r2, p1, p2, h2)


def _peer(h2, gain, w_q, key1, key2, u, v):
    S = h2.shape[0]
    xn, bsel, r2, p1, p2 = _peer_route(h2, gain, w_q, key1, key2, min(256, S))
    return _peer_dense(h2, xn, u, v, bsel, r2, p1, p2, min(512, S), 1024)


def _layer(x, mem, positions, p):
    S = x.shape[0]
    proj = _norm_proj(x, p["mix_norm_gain"], _pack_w_in(p["w_in"]), min(1024, S), 768)
    qT, kh, vT, iqT, ik, iwT = _att_prep(proj, positions, p["att_q_gain"], p["att_k_gain"], 256)
    bias = _dsa_select(iqT, iwT, ik, 256, 512)
    y_att = _dsa_attend(qT, kh, vT, bias, 512, 1024)
    y_rwkv = _rwkv_time_mix(proj, p)
    kmT, vm = _mem_kv(mem, p["mem_norm_gain"], p["w_mem_kv"], p["mem_k_gain"])
    h2 = _merge(x, y_att, y_rwkv, proj, kmT, vm, p["mem_q_gain"], p["gate_bias"], p["w_up"],
                p["w_out"], 256)
    return _peer(h2, p["ffn_norm_gain"], p["peer_w_q"], p["peer_key1"], p["peer_key2"],
                 p["peer_u"], p["peer_v"])


_PARAM_NAMES = ("mix_norm_gain", "w_in", "gate_bias", "att_q_gain", "att_k_gain", "mem_norm_gain",
                "w_mem_kv", "mem_q_gain", "mem_k_gain", "rwkv_mu", "rwkv_w0", "rwkv_w2", "rwkv_a0",
                "rwkv_a2", "rwkv_g2", "rwkv_k_k", "rwkv_k_a", "rwkv_r_k", "rwkv_ln_w", "rwkv_ln_b",
                "w_up", "w_out", "ffn_norm_gain", "peer_w_q", "peer_key1", "peer_key2", "peer_u",
                "peer_v")


def kernel(x, mem, positions, mix_norm_gain, w_in, gate_bias, att_q_gain, att_k_gain,
           mem_norm_gain, w_mem_kv, mem_q_gain, mem_k_gain, rwkv_mu, rwkv_w0, rwkv_w2, rwkv_a0,
           rwkv_a2, rwkv_g2, rwkv_k_k, rwkv_k_a, rwkv_r_k, rwkv_ln_w, rwkv_ln_b, w_up, w_out,
           ffn_norm_gain, peer_w_q, peer_key1, peer_key2, peer_u, peer_v):
    params = (mix_norm_gain, w_in, gate_bias, att_q_gain, att_k_gain, mem_norm_gain, w_mem_kv,
              mem_q_gain, mem_k_gain, rwkv_mu, rwkv_w0, rwkv_w2, rwkv_a0, rwkv_a2, rwkv_g2,
              rwkv_k_k, rwkv_k_a, rwkv_r_k, rwkv_ln_w, rwkv_ln_b, w_up, w_out, ffn_norm_gain,
              peer_w_q, peer_key1, peer_key2, peer_u, peer_v)
    assert x.shape[0] == 1 and all(t.shape[0] == 1 for t in params)
    p = {name: t[0] for name, t in zip(_PARAM_NAMES, params)}
    return _layer(x[0], mem[0], positions[0], p)[None]
```

```python
import functools

import numpy as np
import jax
import jax.numpy as jnp
from jax import lax
from jax.experimental import pallas as pl
from jax.experimental.pallas import tpu as pltpu

F32 = jnp.float32
I32 = jnp.int32
MXU_DTYPE = jnp.bfloat16
HI = lax.Precision.HIGHEST

D_MODEL = 1024
ATT_HEADS, ATT_HD = 8, 64
ATT_W = ATT_HEADS * ATT_HD
IDX_HEADS, IDX_HD = 4, 64
TOPK_MAX = 256
RWKV_HEADS, RWKV_HD = 8, 64
RWKV_W = RWKV_HEADS * RWKV_HD
DECAY_LORA, AAA_LORA, GATE_LORA = 64, 64, 128
GN_EPS = 64e-5
MEM_HEADS, MEM_HD = 4, 128
MEM_W = MEM_HEADS * MEM_HD
N_BRANCH = 3
ROPE_THETA = 500000.0
ROPE_ROT = ATT_HD // 4
NORM_EPS = 1e-6
PEER_KEYS = 128
PEER_HEADS = 8
PEER_QDIM = 256
PEER_TOPK = 16

LOG2E = 1.4426950408889634
V_ROWS = 80
LANES = 128
INT_MIN = -(2 ** 31)
NEG_BIG = -1e30
VMEM_LIMIT = 56 * 1024 * 1024

COL_GATE = 0
COL_Q = 3072
COL_K = 3584
COL_V = 4096
COL_R = 4608
COL_RK = 5120
COL_RV = 5632
COL_MEMQ = 6144
COL_IQ = 6656
COL_LORA = 6912
COL_IKIW = 7168
PROJ_COLS = 7680


def _cparams(sem):
    return pltpu.CompilerParams(dimension_semantics=sem, vmem_limit_bytes=VMEM_LIMIT)


def _mm(a, b):
    return jnp.dot(a.astype(MXU_DTYPE), b.astype(MXU_DTYPE), preferred_element_type=F32)


def _split3(x):
    hi = x.astype(MXU_DTYPE)
    r = x - hi.astype(F32)
    mid = r.astype(MXU_DTYPE)
    return hi, mid, (r - mid.astype(F32)).astype(MXU_DTYPE)


def _mm_sel_r(x, sel):
    sel = sel.astype(MXU_DTYPE)
    return sum(jnp.dot(t, sel, preferred_element_type=F32) for t in _split3(x))


def _mm_sel_l(sel, x):
    sel = sel.astype(MXU_DTYPE)
    return sum(jnp.dot(sel, t, preferred_element_type=F32) for t in _split3(x))


def _normproj_kernel(x_ref, g_ref, w_ref, o_ref, xn_ref):
    @pl.when(pl.program_id(1) == 0)
    def _():
        x = x_ref[...]
        ms = jnp.mean(x * x, axis=-1, keepdims=True)
        xn_ref[...] = (x * lax.rsqrt(ms + NORM_EPS) * g_ref[...]).astype(xn_ref.dtype)

    o_ref[...] = jnp.dot(xn_ref[...], w_ref[...], preferred_element_type=F32)


def _norm_proj(x, gain, w, tm, tn):
    S, D = x.shape
    N = w.shape[1]
    return pl.pallas_call(
        _normproj_kernel,
        grid=(S // tm, N // tn),
        in_specs=[
            pl.BlockSpec((tm, D), lambda i, j: (i, 0)),
            pl.BlockSpec((1, D), lambda i, j: (0, 0)),
            pl.BlockSpec((D, tn), lambda i, j: (0, j)),
        ],
        out_specs=pl.BlockSpec((tm, tn), lambda i, j: (i, j)),
        out_shape=jax.ShapeDtypeStruct((S, N), F32),
        scratch_shapes=[pltpu.VMEM((tm, D), w.dtype)],
        compiler_params=_cparams(("parallel", "arbitrary")),
    )(x, gain.reshape(1, D), w)


def _pack_w_in(w_in):
    D = w_in.shape[0]
    o = 0
    parts = {}
    for name, width in (("q", ATT_W), ("k", ATT_W), ("v", ATT_W), ("iq", IDX_HEADS * IDX_HD),
                        ("ik", IDX_HD), ("iw", IDX_HEADS), ("r", RWKV_W), ("rk", RWKV_W),
                        ("rv", RWKV_W), ("lora", DECAY_LORA + AAA_LORA + GATE_LORA),
                        ("memq", MEM_W), ("gate", N_BRANCH * D_MODEL)):
        parts[name] = w_in[:, o:o + width]
        o += width
    ikiw = jnp.concatenate([parts["ik"], parts["iw"]], axis=1)
    ikiw = jnp.pad(ikiw, ((0, 0), (0, LANES - ikiw.shape[1])))
    packed = jnp.concatenate(
        [parts["gate"], parts["q"], parts["k"], parts["v"], parts["r"], parts["rk"], parts["rv"],
         parts["memq"], parts["iq"], parts["lora"], ikiw], axis=1)
    packed = jnp.pad(packed, ((0, 0), (0, PROJ_COLS - packed.shape[1])))
    return packed.astype(MXU_DTYPE)


def _rope_tables():
    d = np.arange(LANES) % ATT_HD
    inv_freq = 1.0 / (ROPE_THETA ** (np.arange(0, ROPE_ROT, 2, dtype=np.float32) / ROPE_ROT))
    half = ROPE_ROT // 2
    tab = np.zeros((8, LANES), np.float32)
    tab[0] = np.where(d < ROPE_ROT, inv_freq[d % half], 0.0)
    tab[1] = np.where(d < half, -1.0, 0.0)
    tab[2] = np.where((d >= half) & (d < ROPE_ROT), 1.0, 0.0)
    return jnp.asarray(tab)


def _seg_mean_matrix(width, seg):
    g = (np.arange(width)[:, None] // seg) == (np.arange(width)[None, :] // seg)
    return jnp.asarray(g.astype(np.float32) / seg)


def _rope(x, cos, sina, sinb):
    W = x.shape[1]
    reps = W // LANES
    if reps > 1:
        cos = jnp.concatenate([cos] * reps, axis=1)
        sina = jnp.concatenate([sina] * reps, axis=1)
        sinb = jnp.concatenate([sinb] * reps, axis=1)
    half = ROPE_ROT // 2
    up = pltpu.roll(x, W - half, axis=1)
    down = pltpu.roll(x, half, axis=1)
    return x * cos + up * sina + down * sinb


def _att_prep_kernel(q_ref, k_ref, v_ref, iq_ref, ikiw_ref, pos_ref, tab_ref, seg_ref,
                     qg_ref, kg_ref, qT_ref, kh_ref, vT_ref, iqT_ref, ik_ref, iwT_ref):
    tab = tab_ref[...]
    ang = pos_ref[...].astype(F32) * tab[0:1, :]
    cos = jnp.cos(ang)
    sin = jnp.sin(ang)
    sina = sin * tab[1:2, :]
    sinb = sin * tab[2:3, :]
    seg = seg_ref[...]

    def head_norm(x, g):
        ms = _mm_sel_r(x * x, seg)
        return x * lax.rsqrt(ms + NORM_EPS) * g

    q = _rope(head_norm(q_ref[...], qg_ref[...]), cos, sina, sinb) * (ATT_HD ** -0.5 * LOG2E)
    k = _rope(head_norm(k_ref[...], kg_ref[...]), cos, sina, sinb)
    v = v_ref[...]
    iq = _rope(iq_ref[...], cos, sina, sinb)
    ik = _rope(ikiw_ref[...], cos, sina, sinb)
    for h in range(ATT_HEADS):
        kh_ref[h] = k[:, h * ATT_HD:(h + 1) * ATT_HD].astype(kh_ref.dtype)
    qT_ref[...] = q.T.astype(qT_ref.dtype)
    vT = v.T
    tq = vT.shape[1]
    pad = jnp.concatenate([jnp.ones((1, tq), F32), jnp.zeros((V_ROWS - ATT_HD - 1, tq), F32)], axis=0)
    for h in range(ATT_HEADS):
        vT_ref[h] = jnp.concatenate([vT[h * ATT_HD:(h + 1) * ATT_HD, :], pad],
                                    axis=0).astype(vT_ref.dtype)
    iqT_ref[...] = iq.T.astype(iqT_ref.dtype)
    ik_ref[...] = ik[:, :IDX_HD].astype(ik_ref.dtype)
    idx_scale = (IDX_HEADS ** -0.5) * (IDX_HD ** -0.5)
    iwT_ref[...] = ikiw_ref[...].T[IDX_HD:IDX_HD + 8, :] * idx_scale


def _att_prep(proj, positions, att_q_gain, att_k_gain, tq):
    S = proj.shape[0]
    qg = jnp.tile(att_q_gain.reshape(1, ATT_HD), (1, ATT_HEADS))
    kg = jnp.tile(att_k_gain.reshape(1, ATT_HD), (1, ATT_HEADS))
    col = lambda off, w: (lambda i: (i, off // w))
    const = lambda i: (0, 0)
    return pl.pallas_call(
        _att_prep_kernel,
        grid=(S // tq,),
        in_specs=[
            pl.BlockSpec((tq, ATT_W), col(COL_Q, ATT_W)),
            pl.BlockSpec((tq, ATT_W), col(COL_K, ATT_W)),
            pl.BlockSpec((tq, ATT_W), col(COL_V, ATT_W)),
            pl.BlockSpec((tq, IDX_HEADS * IDX_HD), col(COL_IQ, IDX_HEADS * IDX_HD)),
            pl.BlockSpec((tq, LANES), col(COL_IKIW, LANES)),
            pl.BlockSpec((tq, 1), lambda i: (i, 0)),
            pl.BlockSpec((8, LANES), const),
            pl.BlockSpec((ATT_W, ATT_W), const),
            pl.BlockSpec((1, ATT_W), const),
            pl.BlockSpec((1, ATT_W), const),
        ],
        out_specs=[
            pl.BlockSpec((ATT_W, tq), lambda i: (0, i)),
            pl.BlockSpec((ATT_HEADS, tq, ATT_HD), lambda i: (0, i, 0)),
            pl.BlockSpec((ATT_HEADS, V_ROWS, tq), lambda i: (0, 0, i)),
            pl.BlockSpec((IDX_HEADS * IDX_HD, tq), lambda i: (0, i)),
            pl.BlockSpec((tq, IDX_HD), lambda i: (i, 0)),
            pl.BlockSpec((8, tq), lambda i: (0, i)),
        ],
        out_shape=[
            jax.ShapeDtypeStruct((ATT_W, S), MXU_DTYPE),
            jax.ShapeDtypeStruct((ATT_HEADS, S, ATT_HD), MXU_DTYPE),
            jax.ShapeDtypeStruct((ATT_HEADS, V_ROWS, S), MXU_DTYPE),
            jax.ShapeDtypeStruct((IDX_HEADS * IDX_HD, S), MXU_DTYPE),
            jax.ShapeDtypeStruct((S, IDX_HD), MXU_DTYPE),
            jax.ShapeDtypeStruct((8, S), F32),
        ],
        compiler_params=_cparams(("parallel",)),
    )(proj, proj, proj, proj, proj, positions.reshape(S, 1), _rope_tables(),
      _seg_mean_matrix(ATT_W, ATT_HD), qg, kg)


KEY_NEG_INF = -2139095041


def _key_to_float(key):
    f = pltpu.bitcast(key ^ ((key >> 31) & 0x7FFFFFFF), F32)
    return jnp.where(key < KEY_NEG_INF, -jnp.inf, f)


SUB_ACC = 64


LIST_ROWS = 256
LIST_DEPTH = 10


def _sel_kernel(iqT_ref, iwT_ref, ik_ref, tri_ref, bias_ref, sc_ref, cand_ref, *, tq, tk, seq,
                nsel):
    i = pl.program_id(0)
    q0 = i * tq
    nc = (q0 + tq + tk - 1) // tk
    qidx = q0 + lax.broadcasted_iota(I32, (tk, tq), 1)
    krow = lax.broadcasted_iota(I32, (tk, tq), 0)
    iw = iwT_ref[...]

    def score_chunk(c, carry):
        c0 = pl.multiple_of(c * tk, tk)
        ikc = ik_ref[pl.ds(c0, tk), :]
        sc = None
        for h in range(IDX_HEADS):
            logit = jnp.dot(ikc, iqT_ref[h * IDX_HD:(h + 1) * IDX_HD, :],
                            preferred_element_type=F32)
            t = jnp.maximum(logit, 0.0) * iw[h:h + 1, :]
            sc = t if sc is None else sc + t
        sc = jnp.where(c0 + krow <= qidx, sc, -jnp.inf)
        sc_ref[pl.ds(c0, tk), :] = sc
        for part in range(tk // LIST_ROWS):
            x = sc[part * LIST_ROWS:(part + 1) * LIST_ROWS, :]
            for d in range(LIST_DEPTH):
                rows = slice(d * LIST_ROWS, (d + 1) * LIST_ROWS)
                kept = cand_ref[rows, :]
                cand_ref[rows, :] = jnp.maximum(kept, x)
                x = jnp.minimum(kept, x)
        return carry

    cand_ref[...] = jnp.full(cand_ref.shape, -jnp.inf, F32)
    lax.fori_loop(0, nc, score_chunk, 0)

    def count(pred, n_out=1, src_ref=sc_ref, n_chunks=nc):
        sub = SUB_ACC if n_out == 1 else SUB_ACC // 4

        def body(c, accs):
            c0 = pl.multiple_of(c * tk, tk)
            ms = pred(src_ref[pl.ds(c0, tk), :])
            return tuple(a + jnp.sum(m.reshape(tk // sub, sub, tq), axis=0)
                         for a, m in zip(accs, ms))
        accs = lax.fori_loop(0, n_chunks, body, (jnp.zeros((sub, tq), F32),) * n_out)
        return [_col_reduce(a, jnp.add) for a in accs]

    zero = jnp.zeros((1, tq), I32)

    def kth_largest(src_ref, n_chunks):
        def count_ge(cand):
            return count(lambda s: (jnp.where(s >= cand, 1.0, 0.0),), 1, src_ref, n_chunks)[0]

        prefix = jnp.where(count_ge(0.0) >= nsel, zero, zero + INT_MIN)

        def bit_body(b, prefix):
            cand = prefix | jnp.left_shift(jnp.int32(1), 30 - b)
            return jnp.where(count_ge(_key_to_float(cand)) >= nsel, cand, prefix)

        return _key_to_float(lax.fori_loop(0, 31, bit_body, prefix))

    n_cand_chunks = LIST_DEPTH * LIST_ROWS // tk
    tau_cand = kth_largest(cand_ref, n_cand_chunks)
    last_kept = _col_reduce(cand_ref[(LIST_DEPTH - 1) * LIST_ROWS:, :], jnp.maximum)
    lists_ok = jnp.min(jnp.where(last_kept <= tau_cand, 1.0, 0.0)) > 0.5
    tau = lax.cond(lists_ok, lambda: tau_cand, lambda: kth_largest(sc_ref, nc))

    def is_neg(s):
        return pltpu.bitcast(s, I32) < 0

    def tie_classes(s):
        eq = s == tau
        neg = is_neg(s)
        return (jnp.where(eq, jnp.where(neg, 0.0, 1.0), 0.0),
                jnp.where(eq, jnp.where(neg, 1.0, 0.0), 0.0))

    n_gt, n_eq_pos, n_eq_neg = count(
        lambda s: (jnp.where(s > tau, 1.0, 0.0),) + tie_classes(s), n_out=3)
    need = nsel - n_gt
    mixed = jnp.max(jnp.minimum(n_eq_pos, n_eq_neg)) > 0
    tri = tri_ref[...]

    def write_pass(both_signs):
        def body(c, carry):
            c0 = pl.multiple_of(c * tk, tk)
            s = sc_ref[pl.ds(c0, tk), :]
            if both_signs:
                ind_pos, ind_neg = tie_classes(s)
                cum_pos = carry[0] + _mm(tri, ind_pos)
                cum_neg = carry[1] + _mm(tri, ind_neg)
                rank = jnp.where(is_neg(s), n_eq_pos + cum_neg, cum_pos)
                carry = (cum_pos[tk - 1:tk, :], cum_neg[tk - 1:tk, :])
            else:
                rank = carry[0] + _mm(tri, jnp.where(s == tau, 1.0, 0.0))
                carry = (rank[tk - 1:tk, :], carry[1])
            tie = jnp.where(rank <= need, 0.0, NEG_BIG)
            b = jnp.where(s > tau, 0.0, jnp.where(s == tau, tie, NEG_BIG))
            b = jnp.where(c0 + krow <= qidx, b, NEG_BIG)
            bias_ref[pl.ds(c0, tk), :] = b.astype(bias_ref.dtype)
            return carry
        zf = jnp.zeros((1, tq), F32)
        lax.fori_loop(0, nc, body, (zf, zf))

    lax.cond(mixed, lambda: write_pass(True), lambda: write_pass(False))

    def fill_chunk(c, carry):
        c0 = pl.multiple_of(c * tk, tk)
        bias_ref[pl.ds(c0, tk), :] = jnp.full((tk, tq), NEG_BIG, bias_ref.dtype)
        return carry

    lax.fori_loop(nc, seq // tk, fill_chunk, 0)


def _dsa_select(iqT, iwT, ik, tq, tk):
    S = ik.shape[0]
    nsel = min(TOPK_MAX, S // 4)
    kern = functools.partial(_sel_kernel, tq=tq, tk=tk, seq=S, nsel=nsel)
    return pl.pallas_call(
        kern,
        grid=(S // tq,),
        in_specs=[
            pl.BlockSpec((IDX_HEADS * IDX_HD, tq), lambda i: (0, i)),
            pl.BlockSpec((8, tq), lambda i: (0, i)),
            pl.BlockSpec((S, IDX_HD), lambda i: (0, 0)),
            pl.BlockSpec((tk, tk), lambda i: (0, 0)),
        ],
        out_specs=pl.BlockSpec((S, tq), lambda i: (0, i)),
        out_shape=jax.ShapeDtypeStruct((S, S), jnp.bfloat16),
        scratch_shapes=[pltpu.VMEM((S, tq), F32),
                        pltpu.VMEM((LIST_DEPTH * LIST_ROWS, tq), F32)],
        compiler_params=_cparams(("parallel",)),
    )(iqT, iwT, ik, jnp.tril(jnp.ones((tk, tk), MXU_DTYPE)))


def _col_reduce(x, op):
    rows = x.shape[0]
    while rows > 8:
        rows //= 2
        x = op(x[:rows], x[rows:])
    red = jnp.max if op is jnp.maximum else jnp.sum
    return red(x, axis=0, keepdims=True)


def _att_kernel(qi_ref, kj_ref, qT_ref, k_ref, vT_ref, bias_ref, o_ref, m_ref, acc_ref,
                *, tq, tk):
    t = pl.program_id(0)
    i = qi_ref[t]
    j = kj_ref[t]
    last = ((i + 1) * tq - 1) // tk

    @pl.when(j == 0)
    def _():
        m_ref[...] = jnp.full(m_ref.shape, -jnp.inf, F32)
        acc_ref[...] = jnp.zeros(acc_ref.shape, F32)

    bias = bias_ref[...].astype(F32)
    hs = lambda h: slice(h * ATT_HD, (h + 1) * ATT_HD)
    m_all = m_ref[...]
    m_out = []
    halves = (slice(0, tk // 2), slice(tk // 2, tk))
    for h in range(ATT_HEADS):
        qh = qT_ref[hs(h), :]
        ss = [jnp.dot(k_ref[h, ks, :], qh, preferred_element_type=F32) + bias[ks, :]
              for ks in halves]
        m_prev = m_all[h:h + 1, :]
        m_new = jnp.maximum(m_prev, jnp.maximum(_col_reduce(ss[0], jnp.maximum),
                                                _col_reduce(ss[1], jnp.maximum)))
        alpha = jnp.exp2(m_prev - m_new)
        m_out.append(m_new)
        pv = [jnp.dot(vT_ref[h, :, ks], jnp.exp2((s - m_new).astype(vT_ref.dtype)),
                      preferred_element_type=F32) for ks, s in zip(halves, ss)]
        acc_ref[h] = alpha * acc_ref[h] + pv[0] + pv[1]
    m_ref[...] = jnp.concatenate(m_out, axis=0)

    @pl.when(j == last)
    def _():
        out = jnp.concatenate(
            [acc_ref[h, :ATT_HD, :] / acc_ref[h, ATT_HD:ATT_HD + 1, :]
             for h in range(ATT_HEADS)], axis=0)
        o_ref[...] = out.T


def _dsa_attend(qT, kh, vT, bias, tq, tk):
    S = qT.shape[1]
    pairs = [(i, j) for i in range(S // tq) for j in range(((i + 1) * tq - 1) // tk + 1)]
    qi = jnp.asarray([p[0] for p in pairs], I32)
    kj = jnp.asarray([p[1] for p in pairs], I32)
    grid_spec = pltpu.PrefetchScalarGridSpec(
        num_scalar_prefetch=2,
        grid=(len(pairs),),
        in_specs=[
            pl.BlockSpec((ATT_W, tq), lambda t, qi, kj: (0, qi[t])),
            pl.BlockSpec((ATT_HEADS, tk, ATT_HD), lambda t, qi, kj: (0, kj[t], 0)),
            pl.BlockSpec((ATT_HEADS, V_ROWS, tk), lambda t, qi, kj: (0, 0, kj[t])),
            pl.BlockSpec((tk, tq), lambda t, qi, kj: (kj[t], qi[t])),
        ],
        out_specs=pl.BlockSpec((tq, ATT_W), lambda t, qi, kj: (qi[t], 0)),
        scratch_shapes=[
            pltpu.VMEM((ATT_HEADS, tq), F32),
            pltpu.VMEM((ATT_HEADS, V_ROWS, tq), F32),
        ],
    )
    return pl.pallas_call(
        functools.partial(_att_kernel, tq=tq, tk=tk),
        grid_spec=grid_spec,
        out_shape=jax.ShapeDtypeStruct((S, ATT_W), F32),
        compiler_params=_cparams(("arbitrary",)),
    )(qi, kj, qT, kh, vT, bias)


CHUNK = 64


def _shift_rows(x, prev8, first):
    prev_row = jnp.where(first, 0.0, prev8[7:8, :])
    row = lax.broadcasted_iota(I32, x.shape, 0)
    return jnp.where(row == 0, prev_row, pltpu.roll(x, 1, axis=0))


def _softplus(z):
    return jnp.maximum(z, 0.0) + jnp.log1p(jnp.exp(-jnp.abs(z)))


def _rwkv_prep_kernel(r_ref, k_ref, v_ref, lo_ref, rp_ref, kp_ref, vp_ref, lop_ref,
                      mur_ref, muk_ref, muv_ref, mulo_ref, w0_ref, w2_ref, a0_ref, a2_ref,
                      g2_ref, kk_ref, ka_ref, rk_ref, seg_ref, tri_ref, ones_ref, end_ref,
                      rt_ref, kt_ref, bt_ref, at_ref, kh_ref, bh_ref, vh_ref, gam_ref,
                      bonus_ref, g_ref):
    first = pl.program_id(0) == 0

    def mix(x_ref, p_ref, mu_ref):
        x = x_ref[...]
        return x + (_shift_rows(x, p_ref[...], first) - x) * mu_ref[...]

    r = mix(r_ref, rp_ref, mur_ref)
    k = mix(k_ref, kp_ref, muk_ref)
    v = mix(v_ref, vp_ref, muv_ref)
    lo = mix(lo_ref, lop_ref, mulo_ref)
    wd = lo[:, :DECAY_LORA]
    ad = lo[:, DECAY_LORA:DECAY_LORA + AAA_LORA]
    gd = lo[:, DECAY_LORA + AAA_LORA:]
    w_log = -_softplus(-(w0_ref[...] + _mm(jnp.tanh(wd), w2_ref[...]))) - 0.5
    lw = -jnp.exp(w_log)
    a = jax.nn.sigmoid(a0_ref[...] + _mm(ad, a2_ref[...]))
    g_ref[...] = _mm(jax.nn.sigmoid(gd), g2_ref[...])
    seg = seg_ref[...]
    kk = k * kk_ref[...]
    kk = kk / jnp.maximum(jnp.sqrt(_mm_sel_r(kk * kk, seg)), 1e-12)
    k = k * (1.0 + (a - 1.0) * ka_ref[...])
    bonus_ref[...] = _mm_sel_r(r * k * rk_ref[...], seg) * v
    avec = -kk
    bvec = kk * a
    cs = _mm_sel_l(tri_ref[...], lw)
    cs_end = _mm_sel_l(ones_ref[...], lw)
    e_neg = jnp.exp(-cs)
    e_end = jnp.exp(cs_end - cs)
    outs = ((rt_ref, r * jnp.exp(cs)), (kt_ref, k * e_neg), (bt_ref, bvec * e_neg),
            (at_ref, avec * jnp.exp(cs - lw)), (kh_ref, k * e_end), (bh_ref, bvec * e_end),
            (vh_ref, v), (gam_ref, jnp.exp(_mm_sel_l(end_ref[...], lw))))
    for ref, val in outs:
        for h in range(RWKV_HEADS):
            ref[h] = val[:, h * RWKV_HD:(h + 1) * RWKV_HD]


def _rwkv_prep(proj, p, tq):
    S = proj.shape[0]
    nch = tq // CHUNK
    col = lambda off, w: (lambda i: (i, off // w))
    prev = lambda off, w: (lambda i: (jnp.maximum(i * (tq // 8) - 1, 0), off // w))
    const = lambda i: (0, 0)
    lw_ = DECAY_LORA + AAA_LORA + GATE_LORA
    mu = p["rwkv_mu"]
    row = lambda t: t.reshape(1, -1)
    t_idx = np.arange(tq)
    same = (t_idx[:, None] // CHUNK) == (t_idx[None, :] // CHUNK)
    tri = jnp.asarray((same & (t_idx[None, :] <= t_idx[:, None])).astype(np.float32))
    ones = jnp.asarray(same.astype(np.float32))
    end = jnp.asarray(((t_idx[None, :] // CHUNK) == np.arange(nch)[:, None]).astype(np.float32))
    seg = _seg_mean_matrix(RWKV_W, RWKV_HD) * RWKV_HD
    hm = jax.ShapeDtypeStruct((RWKV_HEADS, S, RWKV_HD), F32)
    hm_spec = pl.BlockSpec((RWKV_HEADS, tq, RWKV_HD), lambda i: (0, i, 0))
    wide = jax.ShapeDtypeStruct((S, RWKV_W), F32)
    wide_spec = pl.BlockSpec((tq, RWKV_W), lambda i: (i, 0))
    vec = lambda w: pl.BlockSpec((1, w), const)
    return pl.pallas_call(
        _rwkv_prep_kernel,
        grid=(S // tq,),
        in_specs=[
            pl.BlockSpec((tq, RWKV_W), col(COL_R, RWKV_W)),
            pl.BlockSpec((tq, RWKV_W), col(COL_RK, RWKV_W)),
            pl.BlockSpec((tq, RWKV_W), col(COL_RV, RWKV_W)),
            pl.BlockSpec((tq, lw_), col(COL_LORA, lw_)),
            pl.BlockSpec((8, RWKV_W), prev(COL_R, RWKV_W)),
            pl.BlockSpec((8, RWKV_W), prev(COL_RK, RWKV_W)),
            pl.BlockSpec((8, RWKV_W), prev(COL_RV, RWKV_W)),
            pl.BlockSpec((8, lw_), prev(COL_LORA, lw_)),
            vec(RWKV_W), vec(RWKV_W), vec(RWKV_W), vec(lw_),
            vec(RWKV_W), pl.BlockSpec((DECAY_LORA, RWKV_W), const),
            vec(RWKV_W), pl.BlockSpec((AAA_LORA, RWKV_W), const),
            pl.BlockSpec((GATE_LORA, RWKV_W), const),
            vec(RWKV_W), vec(RWKV_W), vec(RWKV_W),
            pl.BlockSpec((RWKV_W, RWKV_W), const),
            pl.BlockSpec((tq, tq), const), pl.BlockSpec((tq, tq), const),
            pl.BlockSpec((nch, tq), const),
        ],
        out_specs=[hm_spec] * 7 + [pl.BlockSpec((RWKV_HEADS, nch, RWKV_HD), lambda i: (0, i, 0)),
                                   wide_spec, wide_spec],
        out_shape=[hm] * 7 + [jax.ShapeDtypeStruct((RWKV_HEADS, S // CHUNK, RWKV_HD), F32),
                              wide, wide],
        compiler_params=_cparams(("parallel",)),
    )(proj, proj, proj, proj, proj, proj, proj, proj,
      row(mu[:RWKV_W]), row(mu[RWKV_W:2 * RWKV_W]), row(mu[2 * RWKV_W:3 * RWKV_W]),
      row(mu[3 * RWKV_W:]), row(p["rwkv_w0"]), p["rwkv_w2"], row(p["rwkv_a0"]), p["rwkv_a2"],
      p["rwkv_g2"], row(p["rwkv_k_k"]), row(p["rwkv_k_a"]), row(p["rwkv_r_k"]), seg, tri, ones, end)


def _bmm(a, b, dims):
    return jnp.einsum(dims, a, b, preferred_element_type=F32, precision=HI)


def _bmm1(a, b, dims):
    return jnp.einsum(dims, a.astype(MXU_DTYPE), b.astype(MXU_DTYPE), preferred_element_type=F32)


def _rwkv_chunk_kernel(rt_ref, kt_ref, bt_ref, at_ref, kh_ref, bh_ref, v_ref, gam_ref,
                       p_ref, q_ref, rw_ref, y0_ref, *, nch):
    L, N = CHUNK, RWKV_HD
    ri = lax.broadcasted_iota(I32, (nch, L, L), 1)
    ci = lax.broadcasted_iota(I32, (nch, L, L), 2)
    eye_n = (lax.broadcasted_iota(I32, (nch, N, N), 1)
             == lax.broadcasted_iota(I32, (nch, N, N), 2)).astype(F32)
    for h in range(RWKV_HEADS):
        ld = lambda ref: ref[h].reshape(nch, L, N)
        rt, kt, bt, at, kh, bh, v = (ld(x) for x in (rt_ref, kt_ref, bt_ref, at_ref, kh_ref,
                                                      bh_ref, v_ref))
        gam = gam_ref[h].reshape(nch, 1, N)
        mm = _bmm1
        a_ab = jnp.where(ci < ri, mm(at, bt, "cld,cmd->clm"), 0.0)
        a_ak = jnp.where(ci < ri, mm(at, kt, "cld,cmd->clm"), 0.0)
        m_rk = jnp.where(ci <= ri, mm(rt, kt, "cld,cmd->clm"), 0.0)
        m_rb = jnp.where(ci <= ri, mm(rt, bt, "cld,cmd->clm"), 0.0)
        rhs = jnp.concatenate([at, mm(a_ak, v, "clm,cmd->cld")], axis=2)
        pw = a_ab
        step = 1
        while True:
            rhs = rhs + mm(pw, rhs, "clm,cmd->cld")
            step *= 2
            if step >= L:
                break
            pw = mm(pw, pw, "clm,cmn->cln")
        w, u0 = rhs[:, :, :N], rhs[:, :, N:]
        p_ref[:, h] = eye_n * gam + mm(bh, w, "cld,cle->cde")
        q_ref[:, h] = mm(kh, v, "cld,cle->cde") + mm(bh, u0, "cld,cle->cde")
        rw_ref[h] = (rt + mm(m_rb, w, "clm,cmd->cld")).reshape(nch * L, N)
        y0_ref[h] = (mm(m_rk, v, "clm,cmd->cld") + mm(m_rb, u0, "clm,cmd->cld")).reshape(nch * L, N)


def _rwkv_chunks(rt, kt, bt, at, kh, bh, vh, gam, tt):
    S = rt.shape[1]
    nch = tt // CHUNK
    hm_spec = pl.BlockSpec((RWKV_HEADS, tt, RWKV_HD), lambda i: (0, i, 0))
    hm = jax.ShapeDtypeStruct((RWKV_HEADS, S, RWKV_HD), F32)
    sq_spec = pl.BlockSpec((nch, RWKV_HEADS, RWKV_HD, RWKV_HD), lambda i: (i, 0, 0, 0))
    sq = jax.ShapeDtypeStruct((S // CHUNK, RWKV_HEADS, RWKV_HD, RWKV_HD), F32)
    return pl.pallas_call(
        functools.partial(_rwkv_chunk_kernel, nch=nch),
        grid=(S // tt,),
        in_specs=[hm_spec] * 7 + [pl.BlockSpec((RWKV_HEADS, nch, RWKV_HD), lambda i: (0, i, 0))],
        out_specs=[sq_spec, sq_spec, hm_spec, hm_spec],
        out_shape=[sq, sq, hm, hm],
        compiler_params=_cparams(("parallel",)),
    )(rt, kt, bt, at, kh, bh, vh, gam)


def _rwkv_scan_kernel(p_ref, q_ref, rw_ref, y0_ref, bonus_ref, g_ref, seg_ref, lnw_ref, lnb_ref,
                      o_ref, st_ref, *, nch):
    @pl.when(pl.program_id(0) == 0)
    def _():
        st_ref[...] = jnp.zeros(st_ref.shape, F32)

    L = CHUNK
    st = st_ref[...]
    ys = []
    for c in range(nch):
        rw = rw_ref[:, c * L:(c + 1) * L, :]
        ys.append(_bmm(rw, st, "hld,hde->hle") + y0_ref[:, c * L:(c + 1) * L, :])
        st = _bmm(p_ref[c], st, "hjk,hki->hji") + q_ref[c]
    st_ref[...] = st
    y = jnp.concatenate(ys, axis=1)
    y = jnp.concatenate([y[h] for h in range(RWKV_HEADS)], axis=1)
    seg = seg_ref[...]
    mean = _mm_sel_r(y, seg)
    d = y - mean
    var = _mm_sel_r(d * d, seg)
    y = d * lax.rsqrt(var + GN_EPS) * lnw_ref[...] + lnb_ref[...]
    o_ref[...] = (y + bonus_ref[...]) * g_ref[...]


def _rwkv_scan(pm, qm, rw, y0, bonus, g, ln_w, ln_b, tt):
    S = rw.shape[1]
    nch = tt // CHUNK
    hm_spec = pl.BlockSpec((RWKV_HEADS, tt, RWKV_HD), lambda i: (0, i, 0))
    sq_spec = pl.BlockSpec((nch, RWKV_HEADS, RWKV_HD, RWKV_HD), lambda i: (i, 0, 0, 0))
    wide_spec = pl.BlockSpec((tt, RWKV_W), lambda i: (i, 0))
    const = lambda i: (0, 0)
    return pl.pallas_call(
        functools.partial(_rwkv_scan_kernel, nch=nch),
        grid=(S // tt,),
        in_specs=[sq_spec, sq_spec, hm_spec, hm_spec, wide_spec, wide_spec,
                  pl.BlockSpec((RWKV_W, RWKV_W), const), pl.BlockSpec((1, RWKV_W), const),
                  pl.BlockSpec((1, RWKV_W), const)],
        out_specs=wide_spec,
        out_shape=jax.ShapeDtypeStruct((S, RWKV_W), F32),
        scratch_shapes=[pltpu.VMEM((RWKV_HEADS, RWKV_HD, RWKV_HD), F32)],
        compiler_params=_cparams(("arbitrary",)),
    )(pm, qm, rw, y0, bonus, g, _seg_mean_matrix(RWKV_W, RWKV_HD), ln_w.reshape(1, -1),
      ln_b.reshape(1, -1))


def _rwkv_time_mix(proj, p):
    outs = _rwkv_prep(proj, p, 512)
    rt, kt, bt, at, kh, bh, vh, gam, bonus, g = outs
    pm, qm, rw, y0 = _rwkv_chunks(rt, kt, bt, at, kh, bh, vh, gam, 512)
    return _rwkv_scan(pm, qm, rw, y0, bonus, g, p["rwkv_ln_w"], p["rwkv_ln_b"], 512)


def _lane_rmsnorm(x, gain):
    return x * lax.rsqrt(jnp.mean(x * x, axis=-1, keepdims=True) + NORM_EPS) * gain


def _mem_kv_kernel(mem_ref, g_ref, w_ref, kg_ref, kmT_ref, vm_ref):
    m = _lane_rmsnorm(mem_ref[...], g_ref[...])
    kv = _mm(m, w_ref[...])
    km = jnp.concatenate(
        [_lane_rmsnorm(kv[:, h * MEM_HD:(h + 1) * MEM_HD], kg_ref[...]) for h in range(MEM_HEADS)],
        axis=1)
    kmT_ref[...] = km.T.astype(kmT_ref.dtype)
    vm_ref[...] = kv[:, MEM_W:].astype(vm_ref.dtype)


def _mem_kv(mem, gain, w_kv, k_gain):
    M = mem.shape[0]
    return pl.pallas_call(
        _mem_kv_kernel,
        out_shape=[jax.ShapeDtypeStruct((MEM_W, M), MXU_DTYPE),
                   jax.ShapeDtypeStruct((M, MEM_W), MXU_DTYPE)],
        compiler_params=pltpu.CompilerParams(vmem_limit_bytes=VMEM_LIMIT),
    )(mem, gain.reshape(1, -1), w_kv.astype(MXU_DTYPE), k_gain.reshape(1, -1))


def _merge_kernel(x_ref, yatt_ref, yrwkv_ref, memq_ref, gate_ref, kmT_ref, vm_ref, qg_ref,
                  gb_ref, wup_ref, wout_ref, o_ref):
    q = memq_ref[...]
    heads = []
    for h in range(MEM_HEADS):
        sl = slice(h * MEM_HD, (h + 1) * MEM_HD)
        qn = _lane_rmsnorm(q[:, sl], qg_ref[...])
        s = _mm(qn, kmT_ref[sl, :]) * (MEM_HD ** -0.5)
        s = s - jnp.max(s, axis=-1, keepdims=True)
        e = jnp.exp(s)
        p = e / jnp.sum(e, axis=-1, keepdims=True)
        heads.append(_mm(p, vm_ref[:, sl]))
    ymem = jnp.concatenate(heads, axis=1)
    merged = None
    for c, y in enumerate((yatt_ref[...], yrwkv_ref[...], ymem)):
        up = _mm(y, wup_ref[c])
        gate = jax.nn.sigmoid(gate_ref[:, c * D_MODEL:(c + 1) * D_MODEL] + gb_ref[c:c + 1, :])
        merged = gate * up if merged is None else merged + gate * up
    o_ref[...] = x_ref[...] + _mm(merged, wout_ref[...])


def _merge(x, y_att, y_rwkv, proj, kmT, vm, mem_q_gain, gate_bias, w_up, w_out, tq):
    S = x.shape[0]
    M = vm.shape[0]
    gw = N_BRANCH * D_MODEL
    const2 = lambda i: (0, 0)
    return pl.pallas_call(
        _merge_kernel,
        grid=(S // tq,),
        in_specs=[
            pl.BlockSpec((tq, D_MODEL), lambda i: (i, 0)),
            pl.BlockSpec((tq, ATT_W), lambda i: (i, 0)),
            pl.BlockSpec((tq, RWKV_W), lambda i: (i, 0)),
            pl.BlockSpec((tq, MEM_W), lambda i: (i, COL_MEMQ // MEM_W)),
            pl.BlockSpec((tq, gw), lambda i: (i, COL_GATE // gw)),
            pl.BlockSpec((MEM_W, M), const2),
            pl.BlockSpec((M, MEM_W), const2),
            pl.BlockSpec((1, MEM_HD), const2),
            pl.BlockSpec((N_BRANCH, D_MODEL), const2),
            pl.BlockSpec((N_BRANCH, ATT_W, D_MODEL), lambda i: (0, 0, 0)),
            pl.BlockSpec((D_MODEL, D_MODEL), const2),
        ],
        out_specs=pl.BlockSpec((tq, D_MODEL), lambda i: (i, 0)),
        out_shape=jax.ShapeDtypeStruct((S, D_MODEL), F32),
        compiler_params=_cparams(("parallel",)),
    )(x, y_att, y_rwkv, proj, proj, kmT, vm, mem_q_gain.reshape(1, -1), gate_bias,
      w_up.astype(MXU_DTYPE), w_out.astype(MXU_DTYPE))


NO_RANK = 255.0


def _pop_max(cur, rows):
    m = jnp.max(cur, axis=0, keepdims=True)
    first = jnp.min(jnp.where(cur == m, rows, cur.shape[0]), axis=0, keepdims=True)
    hit = rows == first
    return m, hit, jnp.where(hit, -jnp.inf, cur)


def _peer_route_kernel(h_ref, g_ref, wq_ref, k1_ref, k2_ref,
                       xn_ref, bsel_ref, r2_ref, p1_ref, p2_ref):
    tq = h_ref.shape[0]
    xn = _lane_rmsnorm(h_ref[...], g_ref[...]).astype(xn_ref.dtype)
    xn_ref[...] = xn
    qp = jnp.dot(xn, wq_ref[...], preferred_element_type=F32)
    nt = (((1,), (1,)), ((), ()))
    half = PEER_QDIM // 2
    rows = lax.broadcasted_iota(I32, (PEER_KEYS, tq), 0)
    pairs = [(a, b) for a in range(PEER_TOPK) for b in range(PEER_TOPK)
             if (a + 1) * (b + 1) <= PEER_TOPK]
    npad = -len(pairs) % 8
    crow = lax.broadcasted_iota(I32, (len(pairs) + npad, tq), 0)
    for h in range(PEER_HEADS):
        q1 = qp[:, h * PEER_QDIM:h * PEER_QDIM + half].astype(MXU_DTYPE)
        q2 = qp[:, h * PEER_QDIM + half:(h + 1) * PEER_QDIM].astype(MXU_DTYPE)
        s1 = lax.dot_general(k1_ref[...], q1, nt, preferred_element_type=F32)
        s2 = lax.dot_general(k2_ref[...], q2, nt, preferred_element_type=F32)
        tops, ranks = [], []
        for s in (s1, s2):
            cur, vals = s, []
            rank = jnp.full((PEER_KEYS, tq), NO_RANK, F32)
            for r in range(PEER_TOPK):
                m, hit, cur = _pop_max(cur, rows)
                rank = jnp.where(hit, float(r), rank)
                vals.append(m)
            tops.append(vals)
            ranks.append(rank)
        v1, v2 = tops
        rank1, rank2 = ranks
        cand = jnp.concatenate([v1[a] + v2[b] for a, b in pairs]
                               + [jnp.full((npad, tq), -jnp.inf, F32)], axis=0)
        top = v1[0] + v2[0]
        z = jnp.zeros((1, tq), F32)
        for _ in range(PEER_TOPK):
            tau, _, cand = _pop_max(cand, crow)
            z = z + jnp.exp(tau - top)
        bsel = jnp.zeros((PEER_KEYS, tq), F32)
        for a in range(PEER_TOPK):
            n_b = jnp.zeros((1, tq), F32)
            for b in [pb for pa, pb in pairs if pa == a]:
                n_b = n_b + jnp.where(v1[a] + v2[b] >= tau, 1.0, 0.0)
            bsel = jnp.where(rank1 == float(a), n_b, bsel)
        bsel_ref[h] = bsel
        r2_ref[h] = rank2.astype(r2_ref.dtype)
        p1_ref[h] = jnp.exp(s1 - v1[0]) / z
        p2_ref[h] = jnp.exp(s2 - v2[0]).astype(p2_ref.dtype)


def _peer_route(h2, gain, w_q, key1, key2, tq):
    S = h2.shape[0]
    const2 = lambda i: (0, 0)
    half = PEER_QDIM // 2
    kt = jax.ShapeDtypeStruct((PEER_HEADS, PEER_KEYS, S), F32)
    ktn = jax.ShapeDtypeStruct((PEER_HEADS, PEER_KEYS, S), MXU_DTYPE)
    kt_spec = pl.BlockSpec((PEER_HEADS, PEER_KEYS, tq), lambda i: (0, 0, i))
    return pl.pallas_call(
        _peer_route_kernel,
        grid=(S // tq,),
        in_specs=[
            pl.BlockSpec((tq, D_MODEL), lambda i: (i, 0)),
            pl.BlockSpec((1, D_MODEL), const2),
            pl.BlockSpec((D_MODEL, PEER_HEADS * PEER_QDIM), const2),
            pl.BlockSpec((PEER_KEYS, half), const2),
            pl.BlockSpec((PEER_KEYS, half), const2),
        ],
        out_specs=[pl.BlockSpec((tq, D_MODEL), lambda i: (i, 0)), kt_spec, kt_spec, kt_spec,
                   kt_spec],
        out_shape=[jax.ShapeDtypeStruct((S, D_MODEL), MXU_DTYPE), kt, ktn, kt, ktn],
        compiler_params=_cparams(("parallel",)),
    )(h2, gain.reshape(1, -1), w_q.astype(MXU_DTYPE), key1.astype(MXU_DTYPE),
      key2.astype(MXU_DTYPE))


def _peer_dense_kernel(xn_ref, u0_ref, unext_ref, vT_ref, bsel_ref, r2_ref, p1_ref, p2_ref, h_ref,
                       o_ref, acc_ref, hid_a_ref, hid_b_ref, x_ref, *, te, ne):
    j = pl.program_id(1)
    nt = (((1,), (1,)), ((), ()))

    @pl.when(j == 0)
    def _():
        acc_ref[...] = jnp.zeros(acc_ref.shape, F32)
        hid_a_ref[...] = lax.dot_general(u0_ref[...], xn_ref[...], nt, preferred_element_type=F32)

    def step(hid_ref, hid_next_ref):
        hid_next_ref[...] = lax.dot_general(unext_ref[...], xn_ref[...], nt,
                                            preferred_element_type=F32)
        hid = hid_ref[...]
        act = (0.5 * hid * (1.0 + lax.erf(hid * (2.0 ** -0.5)))).astype(x_ref.dtype)
        for el in range(te // PEER_KEYS):
            e1 = j * (te // PEER_KEYS) + el
            g = None
            for h in range(PEER_HEADS):
                n_b = bsel_ref[h, pl.ds(e1, 1), :].astype(x_ref.dtype)
                p1r = p1_ref[h, pl.ds(e1, 1), :].astype(x_ref.dtype)
                t = jnp.where(r2_ref[h] < n_b, p2_ref[h], 0.0) * p1r
                g = t if g is None else g + t
            sl = slice(el * PEER_KEYS, (el + 1) * PEER_KEYS)
            x_ref[sl, :] = g * act[sl, :]
        acc_ref[...] += jnp.dot(vT_ref[0], x_ref[...], preferred_element_type=F32)

    pl.when(j % 2 == 0)(lambda: step(hid_a_ref, hid_b_ref))
    pl.when(j % 2 == 1)(lambda: step(hid_b_ref, hid_a_ref))

    @pl.when(j == ne - 1)
    def _():
        o_ref[...] = h_ref[...] + acc_ref[...].T


def _peer_dense(h2, xn, u, v, bsel, r2, p1, p2, tm, te):
    S = h2.shape[0]
    NE = u.shape[0]
    ne = NE // te
    kt_spec = pl.BlockSpec((PEER_HEADS, PEER_KEYS, tm), lambda i, j: (0, 0, i))
    u_c = u.astype(MXU_DTYPE)
    vT_tiles = v.astype(MXU_DTYPE).reshape(ne, te, D_MODEL).transpose(0, 2, 1)
    return pl.pallas_call(
        functools.partial(_peer_dense_kernel, te=te, ne=ne),
        grid=(S // tm, ne),
        in_specs=[
            pl.BlockSpec((tm, D_MODEL), lambda i, j: (i, 0)),
            pl.BlockSpec((te, D_MODEL), lambda i, j: (0, 0)),
            pl.BlockSpec((te, D_MODEL), lambda i, j: (jnp.minimum(j + 1, ne - 1), 0)),
            pl.BlockSpec((1, D_MODEL, te), lambda i, j: (j, 0, 0)),
            kt_spec, kt_spec, kt_spec, kt_spec,
            pl.BlockSpec((tm, D_MODEL), lambda i, j: (i, 0)),
        ],
        out_specs=pl.BlockSpec((tm, D_MODEL), lambda i, j: (i, 0)),
        out_shape=jax.ShapeDtypeStruct((S, D_MODEL), F32),
        scratch_shapes=[pltpu.VMEM((D_MODEL, tm), F32), pltpu.VMEM((te, tm), F32),
                        pltpu.VMEM((te, tm), F32), pltpu.VMEM((te, tm), MXU_DTYPE)],
        compiler_params=_cparams(("parallel", "arbitrary")),
    )(xn, u_c, u_c, vT_tiles, bsel, r2, p1, p2, h2)


def _peer(h2, gain, w_q, key1, key2, u, v):
    S = h2.shape[0]
    xn, bsel, r2, p1, p2 = _peer_route(h2, gain, w_q, key1, key2, min(256, S))
    return _peer_dense(h2, xn, u, v, bsel, r2, p1, p2, min(512, S), 1024)


def _layer(x, mem, positions, p):
    S = x.shape[0]
    proj = _norm_proj(x, p["mix_norm_gain"], _pack_w_in(p["w_in"]), min(1024, S), 768)
    qT, kh, vT, iqT, ik, iwT = _att_prep(proj, positions, p["att_q_gain"], p["att_k_gain"], 256)
    bias = _dsa_select(iqT, iwT, ik, 256, 512)
    y_att = _dsa_attend(qT, kh, vT, bias, 512, 1024)
    y_rwkv = _rwkv_time_mix(proj, p)
    kmT, vm = _mem_kv(mem, p["mem_norm_gain"], p["w_mem_kv"], p["mem_k_gain"])
    h2 = _merge(x, y_att, y_rwkv, proj, kmT, vm, p["mem_q_gain"], p["gate_bias"], p["w_up"],
                p["w_out"], 256)
    return _peer(h2, p["ffn_norm_gain"], p["peer_w_q"], p["peer_key1"], p["peer_key2"],
                 p["peer_u"], p["peer_v"])


_PARAM_NAMES = ("mix_norm_gain", "w_in", "gate_bias", "att_q_gain", "att_k_gain", "mem_norm_gain",
                "w_mem_kv", "mem_q_gain", "mem_k_gain", "rwkv_mu", "rwkv_w0", "rwkv_w2", "rwkv_a0",
                "rwkv_a2", "rwkv_g2", "rwkv_k_k", "rwkv_k_a", "rwkv_r_k", "rwkv_ln_w", "rwkv_ln_b",
                "w_up", "w_out", "ffn_norm_gain", "peer_w_q", "peer_key1", "peer_key2", "peer_u",
                "peer_v")


def kernel(x, mem, positions, mix_norm_gain, w_in, gate_bias, att_q_gain, att_k_gain,
           mem_norm_gain, w_mem_kv, mem_q_gain, mem_k_gain, rwkv_mu, rwkv_w0, rwkv_w2, rwkv_a0,
           rwkv_a2, rwkv_g2, rwkv_k_k, rwkv_k_a, rwkv_r_k, rwkv_ln_w, rwkv_ln_b, w_up, w_out,
           ffn_norm_gain, peer_w_q, peer_key1, peer_key2, peer_u, peer_v):
    params = (mix_norm_gain, w_in, gate_bias, att_q_gain, att_k_gain, mem_norm_gain, w_mem_kv,
              mem_q_gain, mem_k_gain, rwkv_mu, rwkv_w0, rwkv_w2, rwkv_a0, rwkv_a2, rwkv_g2,
              rwkv_k_k, rwkv_k_a, rwkv_r_k, rwkv_ln_w, rwkv_ln_b, w_up, w_out, ffn_norm_gain,
              peer_w_q, peer_key1, peer_key2, peer_u, peer_v)
    assert x.shape[0] == 1 and all(t.shape[0] == 1 for t in params)
    p = {name: t[0] for name, t in zip(_PARAM_NAMES, params)}
    return _layer(x[0], mem[0], positions[0], p)[None]
```

```python
import functools

import numpy as np
import jax
import jax.numpy as jnp
from jax import lax
from jax.experimental import pallas as pl
from jax.experimental.pallas import tpu as pltpu

F32 = jnp.float32
I32 = jnp.int32
MXU_DTYPE = jnp.bfloat16
HI = lax.Precision.HIGHEST

D_MODEL = 1024
ATT_HEADS, ATT_HD = 8, 64
ATT_W = ATT_HEADS * ATT_HD
IDX_HEADS, IDX_HD = 4, 64
TOPK_MAX = 256
RWKV_HEADS, RWKV_HD = 8, 64
RWKV_W = RWKV_HEADS * RWKV_HD
DECAY_LORA, AAA_LORA, GATE_LORA = 64, 64, 128
GN_EPS = 64e-5
MEM_HEADS, MEM_HD = 4, 128
MEM_W = MEM_HEADS * MEM_HD
N_BRANCH = 3
ROPE_THETA = 500000.0
ROPE_ROT = ATT_HD // 4
NORM_EPS = 1e-6
PEER_KEYS = 128
PEER_HEADS = 8
PEER_QDIM = 256
PEER_TOPK = 16

LOG2E = 1.4426950408889634
V_ROWS = 80
LANES = 128
INT_MIN = -(2 ** 31)
NEG_BIG = -1e30
VMEM_LIMIT = 56 * 1024 * 1024

COL_GATE = 0
COL_Q = 3072
COL_K = 3584
COL_V = 4096
COL_R = 4608
COL_RK = 5120
COL_RV = 5632
COL_MEMQ = 6144
COL_IQ = 6656
COL_LORA = 6912
COL_IKIW = 7168
PROJ_COLS = 7680


def _cparams(sem):
    return pltpu.CompilerParams(dimension_semantics=sem, vmem_limit_bytes=VMEM_LIMIT)


def _mm(a, b):
    return jnp.dot(a.astype(MXU_DTYPE), b.astype(MXU_DTYPE), preferred_element_type=F32)


def _split3(x):
    hi = x.astype(MXU_DTYPE)
    r = x - hi.astype(F32)
    mid = r.astype(MXU_DTYPE)
    return hi, mid, (r - mid.astype(F32)).astype(MXU_DTYPE)


def _mm_sel_r(x, sel):
    sel = sel.astype(MXU_DTYPE)
    return sum(jnp.dot(t, sel, preferred_element_type=F32) for t in _split3(x))


def _mm_sel_l(sel, x):
    sel = sel.astype(MXU_DTYPE)
    return sum(jnp.dot(sel, t, preferred_element_type=F32) for t in _split3(x))


def _normproj_kernel(x_ref, g_ref, w_ref, o_ref, xn_ref):
    @pl.when(pl.program_id(1) == 0)
    def _():
        x = x_ref[...]
        ms = jnp.mean(x * x, axis=-1, keepdims=True)
        xn_ref[...] = (x * lax.rsqrt(ms + NORM_EPS) * g_ref[...]).astype(xn_ref.dtype)

    o_ref[...] = jnp.dot(xn_ref[...], w_ref[...], preferred_element_type=F32)


def _norm_proj(x, gain, w, tm, tn):
    S, D = x.shape
    N = w.shape[1]
    return pl.pallas_call(
        _normproj_kernel,
        grid=(S // tm, N // tn),
        in_specs=[
            pl.BlockSpec((tm, D), lambda i, j: (i, 0)),
            pl.BlockSpec((1, D), lambda i, j: (0, 0)),
            pl.BlockSpec((D, tn), lambda i, j: (0, j)),
        ],
        out_specs=pl.BlockSpec((tm, tn), lambda i, j: (i, j)),
        out_shape=jax.ShapeDtypeStruct((S, N), F32),
        scratch_shapes=[pltpu.VMEM((tm, D), w.dtype)],
        compiler_params=_cparams(("parallel", "arbitrary")),
    )(x, gain.reshape(1, D), w)


def _pack_w_in(w_in):
    D = w_in.shape[0]
    o = 0
    parts = {}
    for name, width in (("q", ATT_W), ("k", ATT_W), ("v", ATT_W), ("iq", IDX_HEADS * IDX_HD),
                        ("ik", IDX_HD), ("iw", IDX_HEADS), ("r", RWKV_W), ("rk", RWKV_W),
                        ("rv", RWKV_W), ("lora", DECAY_LORA + AAA_LORA + GATE_LORA),
                        ("memq", MEM_W), ("gate", N_BRANCH * D_MODEL)):
        parts[name] = w_in[:, o:o + width]
        o += width
    ikiw = jnp.concatenate([parts["ik"], parts["iw"]], axis=1)
    ikiw = jnp.pad(ikiw, ((0, 0), (0, LANES - ikiw.shape[1])))
    packed = jnp.concatenate(
        [parts["gate"], parts["q"], parts["k"], parts["v"], parts["r"], parts["rk"], parts["rv"],
         parts["memq"], parts["iq"], parts["lora"], ikiw], axis=1)
    packed = jnp.pad(packed, ((0, 0), (0, PROJ_COLS - packed.shape[1])))
    return packed.astype(MXU_DTYPE)


def _rope_tables():
    d = np.arange(LANES) % ATT_HD
    inv_freq = 1.0 / (ROPE_THETA ** (np.arange(0, ROPE_ROT, 2, dtype=np.float32) / ROPE_ROT))
    half = ROPE_ROT // 2
    tab = np.zeros((8, LANES), np.float32)
    tab[0] = np.where(d < ROPE_ROT, inv_freq[d % half], 0.0)
    tab[1] = np.where(d < half, -1.0, 0.0)
    tab[2] = np.where((d >= half) & (d < ROPE_ROT), 1.0, 0.0)
    return jnp.asarray(tab)


def _seg_mean_matrix(width, seg):
    g = (np.arange(width)[:, None] // seg) == (np.arange(width)[None, :] // seg)
    return jnp.asarray(g.astype(np.float32) / seg)


def _rope(x, cos, sina, sinb):
    W = x.shape[1]
    reps = W // LANES
    if reps > 1:
        cos = jnp.concatenate([cos] * reps, axis=1)
        sina = jnp.concatenate([sina] * reps, axis=1)
        sinb = jnp.concatenate([sinb] * reps, axis=1)
    half = ROPE_ROT // 2
    up = pltpu.roll(x, W - half, axis=1)
    down = pltpu.roll(x, half, axis=1)
    return x * cos + up * sina + down * sinb


def _att_prep_kernel(q_ref, k_ref, v_ref, iq_ref, ikiw_ref, pos_ref, tab_ref, seg_ref,
                     qg_ref, kg_ref, qT_ref, kh_ref, vT_ref, iqT_ref, ik_ref, iwT_ref):
    tab = tab_ref[...]
    ang = pos_ref[...].astype(F32) * tab[0:1, :]
    cos = jnp.cos(ang)
    sin = jnp.sin(ang)
    sina = sin * tab[1:2, :]
    sinb = sin * tab[2:3, :]
    seg = seg_ref[...]

    def head_norm(x, g):
        ms = _mm_sel_r(x * x, seg)
        return x * lax.rsqrt(ms + NORM_EPS) * g

    q = _rope(head_norm(q_ref[...], qg_ref[...]), cos, sina, sinb) * (ATT_HD ** -0.5 * LOG2E)
    k = _rope(head_norm(k_ref[...], kg_ref[...]), cos, sina, sinb)
    v = v_ref[...]
    iq = _rope(iq_ref[...], cos, sina, sinb)
    ik = _rope(ikiw_ref[...], cos, sina, sinb)
    for h in range(ATT_HEADS):
        kh_ref[h] = k[:, h * ATT_HD:(h + 1) * ATT_HD].astype(kh_ref.dtype)
    qT_ref[...] = q.T.astype(qT_ref.dtype)
    vT = v.T
    tq = vT.shape[1]
    pad = jnp.concatenate([jnp.ones((1, tq), F32), jnp.zeros((V_ROWS - ATT_HD - 1, tq), F32)], axis=0)
    for h in range(ATT_HEADS):
        vT_ref[h] = jnp.concatenate([vT[h * ATT_HD:(h + 1) * ATT_HD, :], pad],
                                    axis=0).astype(vT_ref.dtype)
    iqT_ref[...] = iq.T.astype(iqT_ref.dtype)
    ik_ref[...] = ik[:, :IDX_HD].astype(ik_ref.dtype)
    idx_scale = (IDX_HEADS ** -0.5) * (IDX_HD ** -0.5)
    iwT_ref[...] = ikiw_ref[...].T[IDX_HD:IDX_HD + 8, :] * idx_scale


def _att_prep(proj, positions, att_q_gain, att_k_gain, tq):
    S = proj.shape[0]
    qg = jnp.tile(att_q_gain.reshape(1, ATT_HD), (1, ATT_HEADS))
    kg = jnp.tile(att_k_gain.reshape(1, ATT_HD), (1, ATT_HEADS))
    col = lambda off, w: (lambda i: (i, off // w))
    const = lambda i: (0, 0)
    return pl.pallas_call(
        _att_prep_kernel,
        grid=(S // tq,),
        in_specs=[
            pl.BlockSpec((tq, ATT_W), col(COL_Q, ATT_W)),
            pl.BlockSpec((tq, ATT_W), col(COL_K, ATT_W)),
            pl.BlockSpec((tq, ATT_W), col(COL_V, ATT_W)),
            pl.BlockSpec((tq, IDX_HEADS * IDX_HD), col(COL_IQ, IDX_HEADS * IDX_HD)),
            pl.BlockSpec((tq, LANES), col(COL_IKIW, LANES)),
            pl.BlockSpec((tq, 1), lambda i: (i, 0)),
            pl.BlockSpec((8, LANES), const),
            pl.BlockSpec((ATT_W, ATT_W), const),
            pl.BlockSpec((1, ATT_W), const),
            pl.BlockSpec((1, ATT_W), const),
        ],
        out_specs=[
            pl.BlockSpec((ATT_W, tq), lambda i: (0, i)),
            pl.BlockSpec((ATT_HEADS, tq, ATT_HD), lambda i: (0, i, 0)),
            pl.BlockSpec((ATT_HEADS, V_ROWS, tq), lambda i: (0, 0, i)),
            pl.BlockSpec((IDX_HEADS * IDX_HD, tq), lambda i: (0, i)),
            pl.BlockSpec((tq, IDX_HD), lambda i: (i, 0)),
            pl.BlockSpec((8, tq), lambda i: (0, i)),
        ],
        out_shape=[
            jax.ShapeDtypeStruct((ATT_W, S), MXU_DTYPE),
            jax.ShapeDtypeStruct((ATT_HEADS, S, ATT_HD), MXU_DTYPE),
            jax.ShapeDtypeStruct((ATT_HEADS, V_ROWS, S), MXU_DTYPE),
            jax.ShapeDtypeStruct((IDX_HEADS * IDX_HD, S), MXU_DTYPE),
            jax.ShapeDtypeStruct((S, IDX_HD), MXU_DTYPE),
            jax.ShapeDtypeStruct((8, S), F32),
        ],
        compiler_params=_cparams(("parallel",)),
    )(proj, proj, proj, proj, proj, positions.reshape(S, 1), _rope_tables(),
      _seg_mean_matrix(ATT_W, ATT_HD), qg, kg)


KEY_NEG_INF = -2139095041


def _key_to_float(key):
    f = pltpu.bitcast(key ^ ((key >> 31) & 0x7FFFFFFF), F32)
    return jnp.where(key < KEY_NEG_INF, -jnp.inf, f)


SUB_ACC = 64


LIST_ROWS = 256
LIST_DEPTH = 10


def _sel_kernel(iqT_ref, iwT_ref, ik_ref, tri_ref, bias_ref, sc_ref, cand_ref, *, tq, tk, seq,
                nsel):
    i = pl.program_id(0)
    q0 = i * tq
    nc = (q0 + tq + tk - 1) // tk
    qidx = q0 + lax.broadcasted_iota(I32, (tk, tq), 1)
    krow = lax.broadcasted_iota(I32, (tk, tq), 0)
    iw = iwT_ref[...]

    def score_chunk(c, carry):
        c0 = pl.multiple_of(c * tk, tk)
        ikc = ik_ref[pl.ds(c0, tk), :]
        sc = None
        for h in range(IDX_HEADS):
            logit = jnp.dot(ikc, iqT_ref[h * IDX_HD:(h + 1) * IDX_HD, :],
                            preferred_element_type=F32)
            t = jnp.maximum(logit, 0.0) * iw[h:h + 1, :]
            sc = t if sc is None else sc + t
        sc = jnp.where(c0 + krow <= qidx, sc, -jnp.inf)
        sc_ref[pl.ds(c0, tk), :] = sc
        for part in range(tk // LIST_ROWS):
            x = sc[part * LIST_ROWS:(part + 1) * LIST_ROWS, :]
            for d in range(LIST_DEPTH):
                rows = slice(d * LIST_ROWS, (d + 1) * LIST_ROWS)
                kept = cand_ref[rows, :]
                cand_ref[rows, :] = jnp.maximum(kept, x)
                x = jnp.minimum(kept, x)
        return carry

    cand_ref[...] = jnp.full(cand_ref.shape, -jnp.inf, F32)
    lax.fori_loop(0, nc, score_chunk, 0)

    def count(pred, n_out=1, src_ref=sc_ref, n_chunks=nc):
        sub = SUB_ACC if n_out == 1 else SUB_ACC // 4

        def body(c, accs):
            c0 = pl.multiple_of(c * tk, tk)
            ms = pred(src_ref[pl.ds(c0, tk), :])
            return tuple(a + jnp.sum(m.reshape(tk // sub, sub, tq), axis=0)
                         for a, m in zip(accs, ms))
        accs = lax.fori_loop(0, n_chunks, body, (jnp.zeros((sub, tq), F32),) * n_out)
        return [_col_reduce(a, jnp.add) for a in accs]

    zero = jnp.zeros((1, tq), I32)

    def kth_largest(src_ref, n_chunks):
        def count_ge(cand):
            return count(lambda s: (jnp.where(s >= cand, 1.0, 0.0),), 1, src_ref, n_chunks)[0]

        prefix = jnp.where(count_ge(0.0) >= nsel, zero, zero + INT_MIN)

        def bit_body(b, prefix):
            cand = prefix | jnp.left_shift(jnp.int32(1), 30 - b)
            return jnp.where(count_ge(_key_to_float(cand)) >= nsel, cand, prefix)

        return _key_to_float(lax.fori_loop(0, 31, bit_body, prefix))

    n_cand_chunks = LIST_DEPTH * LIST_ROWS // tk
    tau_cand = kth_largest(cand_ref, n_cand_chunks)
    last_kept = _col_reduce(cand_ref[(LIST_DEPTH - 1) * LIST_ROWS:, :], jnp.maximum)
    lists_ok = jnp.min(jnp.where(last_kept <= tau_cand, 1.0, 0.0)) > 0.5
    tau = lax.cond(lists_ok, lambda: tau_cand, lambda: kth_largest(sc_ref, nc))

    def is_neg(s):
        return pltpu.bitcast(s, I32) < 0

    def tie_classes(s):
        eq = s == tau
        neg = is_neg(s)
        return (jnp.where(eq, jnp.where(neg, 0.0, 1.0), 0.0),
                jnp.where(eq, jnp.where(neg, 1.0, 0.0), 0.0))

    n_gt, n_eq_pos, n_eq_neg = count(
        lambda s: (jnp.where(s > tau, 1.0, 0.0),) + tie_classes(s), n_out=3)
    need = nsel - n_gt
    mixed = jnp.max(jnp.minimum(n_eq_pos, n_eq_neg)) > 0
    tri = tri_ref[...]

    def write_pass(both_signs):
        def body(c, carry):
            c0 = pl.multiple_of(c * tk, tk)
            s = sc_ref[pl.ds(c0, tk), :]
            if both_signs:
                ind_pos, ind_neg = tie_classes(s)
                cum_pos = carry[0] + _mm(tri, ind_pos)
                cum_neg = carry[1] + _mm(tri, ind_neg)
                rank = jnp.where(is_neg(s), n_eq_pos + cum_neg, cum_pos)
                carry = (cum_pos[tk - 1:tk, :], cum_neg[tk - 1:tk, :])
            else:
                rank = carry[0] + _mm(tri, jnp.where(s == tau, 1.0, 0.0))
                carry = (rank[tk - 1:tk, :], carry[1])
            tie = jnp.where(rank <= need, 0.0, NEG_BIG)
            b = jnp.where(s > tau, 0.0, jnp.where(s == tau, tie, NEG_BIG))
            b = jnp.where(c0 + krow <= qidx, b, NEG_BIG)
            bias_ref[pl.ds(c0, tk), :] = b.astype(bias_ref.dtype)
            return carry
        zf = jnp.zeros((1, tq), F32)
        lax.fori_loop(0, nc, body, (zf, zf))

    lax.cond(mixed, lambda: write_pass(True), lambda: write_pass(False))

    def fill_chunk(c, carry):
        c0 = pl.multiple_of(c * tk, tk)
        bias_ref[pl.ds(c0, tk), :] = jnp.full((tk, tq), NEG_BIG, bias_ref.dtype)
        return carry

    lax.fori_loop(nc, seq // tk, fill_chunk, 0)


def _dsa_select(iqT, iwT, ik, tq, tk):
    S = ik.shape[0]
    nsel = min(TOPK_MAX, S // 4)
    kern = functools.partial(_sel_kernel, tq=tq, tk=tk, seq=S, nsel=nsel)
    return pl.pallas_call(
        kern,
        grid=(S // tq,),
        in_specs=[
            pl.BlockSpec((IDX_HEADS * IDX_HD, tq), lambda i: (0, i)),
            pl.BlockSpec((8, tq), lambda i: (0, i)),
            pl.BlockSpec((S, IDX_HD), lambda i: (0, 0)),
            pl.BlockSpec((tk, tk), lambda i: (0, 0)),
        ],
        out_specs=pl.BlockSpec((S, tq), lambda i: (0, i)),
        out_shape=jax.ShapeDtypeStruct((S, S), jnp.bfloat16),
        scratch_shapes=[pltpu.VMEM((S, tq), F32),
                        pltpu.VMEM((LIST_DEPTH * LIST_ROWS, tq), F32)],
        compiler_params=_cparams(("parallel",)),
    )(iqT, iwT, ik, jnp.tril(jnp.ones((tk, tk), MXU_DTYPE)))


def _col_reduce(x, op):
    rows = x.shape[0]
    while rows > 8:
        rows //= 2
        x = op(x[:rows], x[rows:])
    red = jnp.max if op is jnp.maximum else jnp.sum
    return red(x, axis=0, keepdims=True)


def _att_kernel(qi_ref, kj_ref, qT_ref, k_ref, vT_ref, bias_ref, o_ref, m_ref, acc_ref,
                *, tq, tk):
    t = pl.program_id(0)
    i = qi_ref[t]
    j = kj_ref[t]
    last = ((i + 1) * tq - 1) // tk

    @pl.when(j == 0)
    def _():
        m_ref[...] = jnp.full(m_ref.shape, -jnp.inf, F32)
        acc_ref[...] = jnp.zeros(acc_ref.shape, F32)

    bias = bias_ref[...].astype(F32)
    hs = lambda h: slice(h * ATT_HD, (h + 1) * ATT_HD)
    m_all = m_ref[...]
    m_out = []
    halves = (slice(0, tk // 2), slice(tk // 2, tk))
    for h in range(ATT_HEADS):
        qh = qT_ref[hs(h), :]
        ss = [jnp.dot(k_ref[h, ks, :], qh, preferred_element_type=F32) + bias[ks, :]
              for ks in halves]
        m_prev = m_all[h:h + 1, :]
        m_new = jnp.maximum(m_prev, jnp.maximum(_col_reduce(ss[0], jnp.maximum),
                                                _col_reduce(ss[1], jnp.maximum)))
        alpha = jnp.exp2(m_prev - m_new)
        m_out.append(m_new)
        pv = [jnp.dot(vT_ref[h, :, ks], jnp.exp2((s - m_new).astype(vT_ref.dtype)),
                      preferred_element_type=F32) for ks, s in zip(halves, ss)]
        acc_ref[h] = alpha * acc_ref[h] + pv[0] + pv[1]
    m_ref[...] = jnp.concatenate(m_out, axis=0)

    @pl.when(j == last)
    def _():
        out = jnp.concatenate(
            [acc_ref[h, :ATT_HD, :] / acc_ref[h, ATT_HD:ATT_HD + 1, :]
             for h in range(ATT_HEADS)], axis=0)
        o_ref[...] = out.T


def _dsa_attend(qT, kh, vT, bias, tq, tk):
    S = qT.shape[1]
    pairs = [(i, j) for i in range(S // tq) for j in range(((i + 1) * tq - 1) // tk + 1)]
    qi = jnp.asarray([p[0] for p in pairs], I32)
    kj = jnp.asarray([p[1] for p in pairs], I32)
    grid_spec = pltpu.PrefetchScalarGridSpec(
        num_scalar_prefetch=2,
        grid=(len(pairs),),
        in_specs=[
            pl.BlockSpec((ATT_W, tq), lambda t, qi, kj: (0, qi[t])),
            pl.BlockSpec((ATT_HEADS, tk, ATT_HD), lambda t, qi, kj: (0, kj[t], 0)),
            pl.BlockSpec((ATT_HEADS, V_ROWS, tk), lambda t, qi, kj: (0, 0, kj[t])),
            pl.BlockSpec((tk, tq), lambda t, qi, kj: (kj[t], qi[t])),
        ],
        out_specs=pl.BlockSpec((tq, ATT_W), lambda t, qi, kj: (qi[t], 0)),
        scratch_shapes=[
            pltpu.VMEM((ATT_HEADS, tq), F32),
            pltpu.VMEM((ATT_HEADS, V_ROWS, tq), F32),
        ],
    )
    return pl.pallas_call(
        functools.partial(_att_kernel, tq=tq, tk=tk),
        grid_spec=grid_spec,
        out_shape=jax.ShapeDtypeStruct((S, ATT_W), F32),
        compiler_params=_cparams(("arbitrary",)),
    )(qi, kj, qT, kh, vT, bias)


CHUNK = 64


def _shift_rows(x, prev8, first):
    prev_row = jnp.where(first, 0.0, prev8[7:8, :])
    row = lax.broadcasted_iota(I32, x.shape, 0)
    return jnp.where(row == 0, prev_row, pltpu.roll(x, 1, axis=0))


def _softplus(z):
    return jnp.maximum(z, 0.0) + jnp.log1p(jnp.exp(-jnp.abs(z)))


def _rwkv_prep_kernel(r_ref, k_ref, v_ref, lo_ref, rp_ref, kp_ref, vp_ref, lop_ref,
                      mur_ref, muk_ref, muv_ref, mulo_ref, w0_ref, w2_ref, a0_ref, a2_ref,
                      g2_ref, kk_ref, ka_ref, rk_ref, seg_ref, tri_ref, ones_ref, end_ref,
                      rt_ref, kt_ref, bt_ref, at_ref, kh_ref, bh_ref, vh_ref, gam_ref,
                      bonus_ref, g_ref):
    first = pl.program_id(0) == 0

    def mix(x_ref, p_ref, mu_ref):
        x = x_ref[...]
        return x + (_shift_rows(x, p_ref[...], first) - x) * mu_ref[...]

    r = mix(r_ref, rp_ref, mur_ref)
    k = mix(k_ref, kp_ref, muk_ref)
    v = mix(v_ref, vp_ref, muv_ref)
    lo = mix(lo_ref, lop_ref, mulo_ref)
    wd = lo[:, :DECAY_LORA]
    ad = lo[:, DECAY_LORA:DECAY_LORA + AAA_LORA]
    gd = lo[:, DECAY_LORA + AAA_LORA:]
    w_log = -_softplus(-(w0_ref[...] + _mm(jnp.tanh(wd), w2_ref[...]))) - 0.5
    lw = -jnp.exp(w_log)
    a = jax.nn.sigmoid(a0_ref[...] + _mm(ad, a2_ref[...]))
    g_ref[...] = _mm(jax.nn.sigmoid(gd), g2_ref[...])
    seg = seg_ref[...]
    kk = k * kk_ref[...]
    kk = kk / jnp.maximum(jnp.sqrt(_mm_sel_r(kk * kk, seg)), 1e-12)
    k = k * (1.0 + (a - 1.0) * ka_ref[...])
    bonus_ref[...] = _mm_sel_r(r * k * rk_ref[...], seg) * v
    avec = -kk
    bvec = kk * a
    cs = _mm_sel_l(tri_ref[...], lw)
    cs_end = _mm_sel_l(ones_ref[...], lw)
    e_neg = jnp.exp(-cs)
    e_end = jnp.exp(cs_end - cs)
    outs = ((rt_ref, r * jnp.exp(cs)), (kt_ref, k * e_neg), (bt_ref, bvec * e_neg),
            (at_ref, avec * jnp.exp(cs - lw)), (kh_ref, k * e_end), (bh_ref, bvec * e_end),
            (vh_ref, v), (gam_ref, jnp.exp(_mm_sel_l(end_ref[...], lw))))
    for ref, val in outs:
        for h in range(RWKV_HEADS):
            ref[h] = val[:, h * RWKV_HD:(h + 1) * RWKV_HD]


def _rwkv_prep(proj, p, tq):
    S = proj.shape[0]
    nch = tq // CHUNK
    col = lambda off, w: (lambda i: (i, off // w))
    prev = lambda off, w: (lambda i: (jnp.maximum(i * (tq // 8) - 1, 0), off // w))
    const = lambda i: (0, 0)
    lw_ = DECAY_LORA + AAA_LORA + GATE_LORA
    mu = p["rwkv_mu"]
    row = lambda t: t.reshape(1, -1)
    t_idx = np.arange(tq)
    same = (t_idx[:, None] // CHUNK) == (t_idx[None, :] // CHUNK)
    tri = jnp.asarray((same & (t_idx[None, :] <= t_idx[:, None])).astype(np.float32))
    ones = jnp.asarray(same.astype(np.float32))
    end = jnp.asarray(((t_idx[None, :] // CHUNK) == np.arange(nch)[:, None]).astype(np.float32))
    seg = _seg_mean_matrix(RWKV_W, RWKV_HD) * RWKV_HD
    hm = jax.ShapeDtypeStruct((RWKV_HEADS, S, RWKV_HD), F32)
    hm_spec = pl.BlockSpec((RWKV_HEADS, tq, RWKV_HD), lambda i: (0, i, 0))
    wide = jax.ShapeDtypeStruct((S, RWKV_W), F32)
    wide_spec = pl.BlockSpec((tq, RWKV_W), lambda i: (i, 0))
    vec = lambda w: pl.BlockSpec((1, w), const)
    return pl.pallas_call(
        _rwkv_prep_kernel,
        grid=(S // tq,),
        in_specs=[
            pl.BlockSpec((tq, RWKV_W), col(COL_R, RWKV_W)),
            pl.BlockSpec((tq, RWKV_W), col(COL_RK, RWKV_W)),
            pl.BlockSpec((tq, RWKV_W), col(COL_RV, RWKV_W)),
            pl.BlockSpec((tq, lw_), col(COL_LORA, lw_)),
            pl.BlockSpec((8, RWKV_W), prev(COL_R, RWKV_W)),
            pl.BlockSpec((8, RWKV_W), prev(COL_RK, RWKV_W)),
            pl.BlockSpec((8, RWKV_W), prev(COL_RV, RWKV_W)),
            pl.BlockSpec((8, lw_), prev(COL_LORA, lw_)),
            vec(RWKV_W), vec(RWKV_W), vec(RWKV_W), vec(lw_),
            vec(RWKV_W), pl.BlockSpec((DECAY_LORA, RWKV_W), const),
            vec(RWKV_W), pl.BlockSpec((AAA_LORA, RWKV_W), const),
            pl.BlockSpec((GATE_LORA, RWKV_W), const),
            vec(RWKV_W), vec(RWKV_W), vec(RWKV_W),
            pl.BlockSpec((RWKV_W, RWKV_W), const),
            pl.BlockSpec((tq, tq), const), pl.BlockSpec((tq, tq), const),
            pl.BlockSpec((nch, tq), const),
        ],
        out_specs=[hm_spec] * 7 + [pl.BlockSpec((RWKV_HEADS, nch, RWKV_HD), lambda i: (0, i, 0)),
                                   wide_spec, wide_spec],
        out_shape=[hm] * 7 + [jax.ShapeDtypeStruct((RWKV_HEADS, S // CHUNK, RWKV_HD), F32),
                              wide, wide],
        compiler_params=_cparams(("parallel",)),
    )(proj, proj, proj, proj, proj, proj, proj, proj,
      row(mu[:RWKV_W]), row(mu[RWKV_W:2 * RWKV_W]), row(mu[2 * RWKV_W:3 * RWKV_W]),
      row(mu[3 * RWKV_W:]), row(p["rwkv_w0"]), p["rwkv_w2"], row(p["rwkv_a0"]), p["rwkv_a2"],
      p["rwkv_g2"], row(p["rwkv_k_k"]), row(p["rwkv_k_a"]), row(p["rwkv_r_k"]), seg, tri, ones, end)


def _bmm(a, b, dims):
    return jnp.einsum(dims, a, b, preferred_element_type=F32, precision=HI)


def _bmm1(a, b, dims):
    return jnp.einsum(dims, a.astype(MXU_DTYPE), b.astype(MXU_DTYPE), preferred_element_type=F32)


def _rwkv_chunk_kernel(rt_ref, kt_ref, bt_ref, at_ref, kh_ref, bh_ref, v_ref, gam_ref,
                       p_ref, q_ref, rw_ref, y0_ref, *, nch):
    L, N = CHUNK, RWKV_HD
    ri = lax.broadcasted_iota(I32, (nch, L, L), 1)
    ci = lax.broadcasted_iota(I32, (nch, L, L), 2)
    eye_n = (lax.broadcasted_iota(I32, (nch, N, N), 1)
             == lax.broadcasted_iota(I32, (nch, N, N), 2)).astype(F32)
    for h in range(RWKV_HEADS):
        ld = lambda ref: ref[h].reshape(nch, L, N)
        rt, kt, bt, at, kh, bh, v = (ld(x) for x in (rt_ref, kt_ref, bt_ref, at_ref, kh_ref,
                                                      bh_ref, v_ref))
        gam = gam_ref[h].reshape(nch, 1, N)
        mm = _bmm1
        a_ab = jnp.where(ci < ri, mm(at, bt, "cld,cmd->clm"), 0.0)
        a_ak = jnp.where(ci < ri, mm(at, kt, "cld,cmd->clm"), 0.0)
        m_rk = jnp.where(ci <= ri, mm(rt, kt, "cld,cmd->clm"), 0.0)
        m_rb = jnp.where(ci <= ri, mm(rt, bt, "cld,cmd->clm"), 0.0)
        rhs = jnp.concatenate([at, mm(a_ak, v, "clm,cmd->cld")], axis=2)
        pw = a_ab
        step = 1
        while True:
            rhs = rhs + mm(pw, rhs, "clm,cmd->cld")
            step *= 2
            if step >= L:
                break
            pw = mm(pw, pw, "clm,cmn->cln")
        w, u0 = rhs[:, :, :N], rhs[:, :, N:]
        p_ref[:, h] = eye_n * gam + mm(bh, w, "cld,cle->cde")
        q_ref[:, h] = mm(kh, v, "cld,cle->cde") + mm(bh, u0, "cld,cle->cde")
        rw_ref[h] = (rt + mm(m_rb, w, "clm,cmd->cld")).reshape(nch * L, N)
        y0_ref[h] = (mm(m_rk, v, "clm,cmd->cld") + mm(m_rb, u0, "clm,cmd->cld")).reshape(nch * L, N)


def _rwkv_chunks(rt, kt, bt, at, kh, bh, vh, gam, tt):
    S = rt.shape[1]
    nch = tt // CHUNK
    hm_spec = pl.BlockSpec((RWKV_HEADS, tt, RWKV_HD), lambda i: (0, i, 0))
    hm = jax.ShapeDtypeStruct((RWKV_HEADS, S, RWKV_HD), F32)
    sq_spec = pl.BlockSpec((nch, RWKV_HEADS, RWKV_HD, RWKV_HD), lambda i: (i, 0, 0, 0))
    sq = jax.ShapeDtypeStruct((S // CHUNK, RWKV_HEADS, RWKV_HD, RWKV_HD), F32)
    return pl.pallas_call(
        functools.partial(_rwkv_chunk_kernel, nch=nch),
        grid=(S // tt,),
        in_specs=[hm_spec] * 7 + [pl.BlockSpec((RWKV_HEADS, nch, RWKV_HD), lambda i: (0, i, 0))],
        out_specs=[sq_spec, sq_spec, hm_spec, hm_spec],
        out_shape=[sq, sq, hm, hm],
        compiler_params=_cparams(("parallel",)),
    )(rt, kt, bt, at, kh, bh, vh, gam)


def _rwkv_scan_kernel(p_ref, q_ref, rw_ref, y0_ref, bonus_ref, g_ref, seg_ref, lnw_ref, lnb_ref,
                      o_ref, st_ref, *, nch):
    @pl.when(pl.program_id(0) == 0)
    def _():
        st_ref[...] = jnp.zeros(st_ref.shape, F32)

    L = CHUNK
    st = st_ref[...]
    ys = []
    for c in range(nch):
        rw = rw_ref[:, c * L:(c + 1) * L, :]
        ys.append(_bmm(rw, st, "hld,hde->hle") + y0_ref[:, c * L:(c + 1) * L, :])
        st = _bmm(p_ref[c], st, "hjk,hki->hji") + q_ref[c]
    st_ref[...] = st
    y = jnp.concatenate(ys, axis=1)
    y = jnp.concatenate([y[h] for h in range(RWKV_HEADS)], axis=1)
    seg = seg_ref[...]
    mean = _mm_sel_r(y, seg)
    d = y - mean
    var = _mm_sel_r(d * d, seg)
    y = d * lax.rsqrt(var + GN_EPS) * lnw_ref[...] + lnb_ref[...]
    o_ref[...] = (y + bonus_ref[...]) * g_ref[...]


def _rwkv_scan(pm, qm, rw, y0, bonus, g, ln_w, ln_b, tt):
    S = rw.shape[1]
    nch = tt // CHUNK
    hm_spec = pl.BlockSpec((RWKV_HEADS, tt, RWKV_HD), lambda i: (0, i, 0))
    sq_spec = pl.BlockSpec((nch, RWKV_HEADS, RWKV_HD, RWKV_HD), lambda i: (i, 0, 0, 0))
    wide_spec = pl.BlockSpec((tt, RWKV_W), lambda i: (i, 0))
    const = lambda i: (0, 0)
    return pl.pallas_call(
        functools.partial(_rwkv_scan_kernel, nch=nch),
        grid=(S // tt,),
        in_specs=[sq_spec, sq_spec, hm_spec, hm_spec, wide_spec, wide_spec,
                  pl.BlockSpec((RWKV_W, RWKV_W), const), pl.BlockSpec((1, RWKV_W), const),
                  pl.BlockSpec((1, RWKV_W), const)],
        out_specs=wide_spec,
        out_shape=jax.ShapeDtypeStruct((S, RWKV_W), F32),
        scratch_shapes=[pltpu.VMEM((RWKV_HEADS, RWKV_HD, RWKV_HD), F32)],
        compiler_params=_cparams(("arbitrary",)),
    )(pm, qm, rw, y0, bonus, g, _seg_mean_matrix(RWKV_W, RWKV_HD), ln_w.reshape(1, -1),
      ln_b.reshape(1, -1))


def _rwkv_time_mix(proj, p):
    outs = _rwkv_prep(proj, p, 512)
    rt, kt, bt, at, kh, bh, vh, gam, bonus, g = outs
    pm, qm, rw, y0 = _rwkv_chunks(rt, kt, bt, at, kh, bh, vh, gam, 512)
    return _rwkv_scan(pm, qm, rw, y0, bonus, g, p["rwkv_ln_w"], p["rwkv_ln_b"], 512)


def _lane_rmsnorm(x, gain):
    return x * lax.rsqrt(jnp.mean(x * x, axis=-1, keepdims=True) + NORM_EPS) * gain


def _mem_kv_kernel(mem_ref, g_ref, w_ref, kg_ref, kmT_ref, vm_ref):
    m = _lane_rmsnorm(mem_ref[...], g_ref[...])
    kv = _mm(m, w_ref[...])
    km = jnp.concatenate(
        [_lane_rmsnorm(kv[:, h * MEM_HD:(h + 1) * MEM_HD], kg_ref[...]) for h in range(MEM_HEADS)],
        axis=1)
    kmT_ref[...] = km.T.astype(kmT_ref.dtype)
    vm_ref[...] = kv[:, MEM_W:].astype(vm_ref.dtype)


def _mem_kv(mem, gain, w_kv, k_gain):
    M = mem.shape[0]
    return pl.pallas_call(
        _mem_kv_kernel,
        out_shape=[jax.ShapeDtypeStruct((MEM_W, M), MXU_DTYPE),
                   jax.ShapeDtypeStruct((M, MEM_W), MXU_DTYPE)],
        compiler_params=pltpu.CompilerParams(vmem_limit_bytes=VMEM_LIMIT),
    )(mem, gain.reshape(1, -1), w_kv.astype(MXU_DTYPE), k_gain.reshape(1, -1))


def _merge_kernel(x_ref, yatt_ref, yrwkv_ref, memq_ref, gate_ref, kmT_ref, vm_ref, qg_ref,
                  gb_ref, wup_ref, wout_ref, o_ref):
    q = memq_ref[...]
    heads = []
    for h in range(MEM_HEADS):
        sl = slice(h * MEM_HD, (h + 1) * MEM_HD)
        qn = _lane_rmsnorm(q[:, sl], qg_ref[...])
        s = _mm(qn, kmT_ref[sl, :]) * (MEM_HD ** -0.5)
        s = s - jnp.max(s, axis=-1, keepdims=True)
        e = jnp.exp(s)
        p = e / jnp.sum(e, axis=-1, keepdims=True)
        heads.append(_mm(p, vm_ref[:, sl]))
    ymem = jnp.concatenate(heads, axis=1)
    merged = None
    for c, y in enumerate((yatt_ref[...], yrwkv_ref[...], ymem)):
        up = _mm(y, wup_ref[c])
        gate = jax.nn.sigmoid(gate_ref[:, c * D_MODEL:(c + 1) * D_MODEL] + gb_ref[c:c + 1, :])
        merged = gate * up if merged is None else merged + gate * up
    o_ref[...] = x_ref[...] + _mm(merged, wout_ref[...])


def _merge(x, y_att, y_rwkv, proj, kmT, vm, mem_q_gain, gate_bias, w_up, w_out, tq):
    S = x.shape[0]
    M = vm.shape[0]
    gw = N_BRANCH * D_MODEL
    const2 = lambda i: (0, 0)
    return pl.pallas_call(
        _merge_kernel,
        grid=(S // tq,),
        in_specs=[
            pl.BlockSpec((tq, D_MODEL), lambda i: (i, 0)),
            pl.BlockSpec((tq, ATT_W), lambda i: (i, 0)),
            pl.BlockSpec((tq, RWKV_W), lambda i: (i, 0)),
            pl.BlockSpec((tq, MEM_W), lambda i: (i, COL_MEMQ // MEM_W)),
            pl.BlockSpec((tq, gw), lambda i: (i, COL_GATE // gw)),
            pl.BlockSpec((MEM_W, M), const2),
            pl.BlockSpec((M, MEM_W), const2),
            pl.BlockSpec((1, MEM_HD), const2),
            pl.BlockSpec((N_BRANCH, D_MODEL), const2),
            pl.BlockSpec((N_BRANCH, ATT_W, D_MODEL), lambda i: (0, 0, 0)),
            pl.BlockSpec((D_MODEL, D_MODEL), const2),
        ],
        out_specs=pl.BlockSpec((tq, D_MODEL), lambda i: (i, 0)),
        out_shape=jax.ShapeDtypeStruct((S, D_MODEL), F32),
        compiler_params=_cparams(("parallel",)),
    )(x, y_att, y_rwkv, proj, proj, kmT, vm, mem_q_gain.reshape(1, -1), gate_bias,
      w_up.astype(MXU_DTYPE), w_out.astype(MXU_DTYPE))


NO_RANK = 255.0


def _pop_max(cur, rows):
    m = jnp.max(cur, axis=0, keepdims=True)
    first = jnp.min(jnp.where(cur == m, rows, cur.shape[0]), axis=0, keepdims=True)
    hit = rows == first
    return m, hit, jnp.where(hit, -jnp.inf, cur)


def _peer_route_kernel(h_ref, g_ref, wq_ref, k1_ref, k2_ref,
                       xn_ref, bsel_ref, r2_ref, p1_ref, p2_ref):
    tq = h_ref.shape[0]
    xn = _lane_rmsnorm(h_ref[...], g_ref[...]).astype(xn_ref.dtype)
    xn_ref[...] = xn
    qp = jnp.dot(xn, wq_ref[...], preferred_element_type=F32)
    nt = (((1,), (1,)), ((), ()))
    half = PEER_QDIM // 2
    rows = lax.broadcasted_iota(I32, (PEER_KEYS, tq), 0)
    pairs = [(a, b) for a in range(PEER_TOPK) for b in range(PEER_TOPK)
             if (a + 1) * (b + 1) <= PEER_TOPK]
    npad = -len(pairs) % 8
    crow = lax.broadcasted_iota(I32, (len(pairs) + npad, tq), 0)
    for h in range(PEER_HEADS):
        q1 = qp[:, h * PEER_QDIM:h * PEER_QDIM + half].astype(MXU_DTYPE)
        q2 = qp[:, h * PEER_QDIM + half:(h + 1) * PEER_QDIM].astype(MXU_DTYPE)
        s1 = lax.dot_general(k1_ref[...], q1, nt, preferred_element_type=F32)
        s2 = lax.dot_general(k2_ref[...], q2, nt, preferred_element_type=F32)
        tops, ranks = [], []
        for s in (s1, s2):
            cur, vals = s, []
            rank = jnp.full((PEER_KEYS, tq), NO_RANK, F32)
            for r in range(PEER_TOPK):
                m, hit, cur = _pop_max(cur, rows)
                rank = jnp.where(hit, float(r), rank)
                vals.append(m)
            tops.append(vals)
            ranks.append(rank)
        v1, v2 = tops
        rank1, rank2 = ranks
        cand = jnp.concatenate([v1[a] + v2[b] for a, b in pairs]
                               + [jnp.full((npad, tq), -jnp.inf, F32)], axis=0)
        top = v1[0] + v2[0]
        z = jnp.zeros((1, tq), F32)
        for _ in range(PEER_TOPK):
            tau, _, cand = _pop_max(cand, crow)
            z = z + jnp.exp(tau - top)
        bsel = jnp.zeros((PEER_KEYS, tq), F32)
        for a in range(PEER_TOPK):
            n_b = jnp.zeros((1, tq), F32)
            for b in [pb for pa, pb in pairs if pa == a]:
                n_b = n_b + jnp.where(v1[a] + v2[b] >= tau, 1.0, 0.0)
            bsel = jnp.where(rank1 == float(a), n_b, bsel)
        bsel_ref[h] = bsel
        r2_ref[h] = rank2.astype(r2_ref.dtype)
        p1_ref[h] = jnp.exp(s1 - v1[0]) / z
        p2_ref[h] = jnp.exp(s2 - v2[0]).astype(p2_ref.dtype)


def _peer_route(h2, gain, w_q, key1, key2, tq):
    S = h2.shape[0]
    const2 = lambda i: (0, 0)
    half = PEER_QDIM // 2
    kt = jax.ShapeDtypeStruct((PEER_HEADS, PEER_KEYS, S), F32)
    ktn = jax.ShapeDtypeStruct((PEER_HEADS, PEER_KEYS, S), MXU_DTYPE)
    kt_spec = pl.BlockSpec((PEER_HEADS, PEER_KEYS, tq), lambda i: (0, 0, i))
    return pl.pallas_call(
        _peer_route_kernel,
        grid=(S // tq,),
        in_specs=[
            pl.BlockSpec((tq, D_MODEL), lambda i: (i, 0)),
            pl.BlockSpec((1, D_MODEL), const2),
            pl.BlockSpec((D_MODEL, PEER_HEADS * PEER_QDIM), const2),
            pl.BlockSpec((PEER_KEYS, half), const2),
            pl.BlockSpec((PEER_KEYS, half), const2),
        ],
        out_specs=[pl.BlockSpec((tq, D_MODEL), lambda i: (i, 0)), kt_spec, kt_spec, kt_spec,
                   kt_spec],
        out_shape=[jax.ShapeDtypeStruct((S, D_MODEL), MXU_DTYPE), kt, ktn, kt, ktn],
        compiler_params=_cparams(("parallel",)),
    )(h2, gain.reshape(1, -1), w_q.astype(MXU_DTYPE), key1.astype(MXU_DTYPE),
      key2.astype(MXU_DTYPE))


def _peer_dense_kernel(xn_ref, u0_ref, unext_ref, vT_ref, bsel_ref, r2_ref, p1_ref, p2_ref, h_ref,
                       o_ref, acc_ref, hid_ref, x_ref, *, te, ne):
    j = pl.program_id(1)
    nt = (((1,), (1,)), ((), ()))

    @pl.when(j == 0)
    def _():
        acc_ref[...] = jnp.zeros(acc_ref.shape, F32)
        hid_ref[0] = lax.dot_general(u0_ref[...], xn_ref[...], nt, preferred_element_type=F32)

    slot = j % 2
    hid_ref[1 - slot] = lax.dot_general(unext_ref[...], xn_ref[...], nt,
                                        preferred_element_type=F32)
    hid = hid_ref[slot]
    act = (0.5 * hid * (1.0 + lax.erf(hid * (2.0 ** -0.5)))).astype(x_ref.dtype)
    for el in range(te // PEER_KEYS):
        e1 = j * (te // PEER_KEYS) + el
        g = None
        for h in range(PEER_HEADS):
            n_b = bsel_ref[h, pl.ds(e1, 1), :].astype(x_ref.dtype)
            p1r = p1_ref[h, pl.ds(e1, 1), :].astype(x_ref.dtype)
            t = jnp.where(r2_ref[h] < n_b, p2_ref[h], 0.0) * p1r
            g = t if g is None else g + t
        sl = slice(el * PEER_KEYS, (el + 1) * PEER_KEYS)
        x_ref[sl, :] = g * act[sl, :]
    acc_ref[...] += jnp.dot(vT_ref[0], x_ref[...], preferred_element_type=F32)

    @pl.when(j == ne - 1)
    def _():
        o_ref[...] = h_ref[...] + acc_ref[...].T


def _peer_dense(h2, xn, u, v, bsel, r2, p1, p2, tm, te):
    S = h2.shape[0]
    NE = u.shape[0]
    ne = NE // te
    kt_spec = pl.BlockSpec((PEER_HEADS, PEER_KEYS, tm), lambda i, j: (0, 0, i))
    u_c = u.astype(MXU_DTYPE)
    vT_tiles = v.astype(MXU_DTYPE).reshape(ne, te, D_MODEL).transpose(0, 2, 1)
    return pl.pallas_call(
        functools.partial(_peer_dense_kernel, te=te, ne=ne),
        grid=(S // tm, ne),
        in_specs=[
            pl.BlockSpec((tm, D_MODEL), lambda i, j: (i, 0)),
            pl.BlockSpec((te, D_MODEL), lambda i, j: (0, 0)),
            pl.BlockSpec((te, D_MODEL), lambda i, j: (jnp.minimum(j + 1, ne - 1), 0)),
            pl.BlockSpec((1, D_MODEL, te), lambda i, j: (j, 0, 0)),
            kt_spec, kt_spec, kt_spec, kt_spec,
            pl.BlockSpec((tm, D_MODEL), lambda i, j: (i, 0)),
        ],
        out_specs=pl.BlockSpec((tm, D_MODEL), lambda i, j: (i, 0)),
        out_shape=jax.ShapeDtypeStruct((S, D_MODEL), F32),
        scratch_shapes=[pltpu.VMEM((D_MODEL, tm), F32), pltpu.VMEM((2, te, tm), F32),
                        pltpu.VMEM((te, tm), MXU_DTYPE)],
        compiler_params=_cparams(("parallel", "arbitrary")),
    )(xn, u_c, u_c, vT_tiles, bsel, r2, p1, p2, h2)


def _peer(h2, gain, w_q, key1, key2, u, v):
    S = h2.shape[0]
    xn, bsel, r2, p1, p2 = _peer_route(h2, gain, w_q, key1, key2, min(256, S))
    return _peer_dense(h2, xn, u, v, bsel, r2, p1, p2, min(512, S), 1024)


def _layer(x, mem, positions, p):
    S = x.shape[0]
    proj = _norm_proj(x, p["mix_norm_gain"], _pack_w_in(p["w_in"]), min(1024, S), 768)
    qT, kh, vT, iqT, ik, iwT = _att_prep(proj, positions, p["att_q_gain"], p["att_k_gain"], 256)
    bias = _dsa_select(iqT, iwT, ik, 256, 512)
    y_att = _dsa_attend(qT, kh, vT, bias, 512, 1024)
    y_rwkv = _rwkv_time_mix(proj, p)
    kmT, vm = _mem_kv(mem, p["mem_norm_gain"], p["w_mem_kv"], p["mem_k_gain"])
    h2 = _merge(x, y_att, y_rwkv, proj, kmT, vm, p["mem_q_gain"], p["gate_bias"], p["w_up"],
                p["w_out"], 256)
    return _peer(h2, p["ffn_norm_gain"], p["peer_w_q"], p["peer_key1"], p["peer_key2"],
                 p["peer_u"], p["peer_v"])


_PARAM_NAMES = ("mix_norm_gain", "w_in", "gate_bias", "att_q_gain", "att_k_gain", "mem_norm_gain",
                "w_mem_kv", "mem_q_gain", "mem_k_gain", "rwkv_mu", "rwkv_w0", "rwkv_w2", "rwkv_a0",
                "rwkv_a2", "rwkv_g2", "rwkv_k_k", "rwkv_k_a", "rwkv_r_k", "rwkv_ln_w", "rwkv_ln_b",
                "w_up", "w_out", "ffn_norm_gain", "peer_w_q", "peer_key1", "peer_key2", "peer_u",
                "peer_v")


def kernel(x, mem, positions, mix_norm_gain, w_in, gate_bias, att_q_gain, att_k_gain,
           mem_norm_gain, w_mem_kv, mem_q_gain, mem_k_gain, rwkv_mu, rwkv_w0, rwkv_w2, rwkv_a0,
           rwkv_a2, rwkv_g2, rwkv_k_k, rwkv_k_a, rwkv_r_k, rwkv_ln_w, rwkv_ln_b, w_up, w_out,
           ffn_norm_gain, peer_w_q, peer_key1, peer_key2, peer_u, peer_v):
    params = (mix_norm_gain, w_in, gate_bias, att_q_gain, att_k_gain, mem_norm_gain, w_mem_kv,
              mem_q_gain, mem_k_gain, rwkv_mu, rwkv_w0, rwkv_w2, rwkv_a0, rwkv_a2, rwkv_g2,
              rwkv_k_k, rwkv_k_a, rwkv_r_k, rwkv_ln_w, rwkv_ln_b, w_up, w_out, ffn_norm_gain,
              peer_w_q, peer_key1, peer_key2, peer_u, peer_v)
    assert x.shape[0] == 1 and all(t.shape[0] == 1 for t in params)
    p = {name: t[0] for name, t in zip(_PARAM_NAMES, params)}
    return _layer(x[0], mem[0], positions[0], p)[None]
```

```python
import functools

import numpy as np
import jax
import jax.numpy as jnp
from jax import lax
from jax.experimental import pallas as pl
from jax.experimental.pallas import tpu as pltpu

F32 = jnp.float32
I32 = jnp.int32
MXU_DTYPE = jnp.bfloat16
HI = lax.Precision.HIGHEST

D_MODEL = 1024
ATT_HEADS, ATT_HD = 8, 64
ATT_W = ATT_HEADS * ATT_HD
IDX_HEADS, IDX_HD = 4, 64
TOPK_MAX = 256
RWKV_HEADS, RWKV_HD = 8, 64
RWKV_W = RWKV_HEADS * RWKV_HD
DECAY_LORA, AAA_LORA, GATE_LORA = 64, 64, 128
GN_EPS = 64e-5
MEM_HEADS, MEM_HD = 4, 128
MEM_W = MEM_HEADS * MEM_HD
N_BRANCH = 3
ROPE_THETA = 500000.0
ROPE_ROT = ATT_HD // 4
NORM_EPS = 1e-6
PEER_KEYS = 128
PEER_HEADS = 8
PEER_QDIM = 256
PEER_TOPK = 16

LOG2E = 1.4426950408889634
V_ROWS = 80
LANES = 128
INT_MIN = -(2 ** 31)
NEG_BIG = -1e30
VMEM_LIMIT = 56 * 1024 * 1024

COL_GATE = 0
COL_Q = 3072
COL_K = 3584
COL_V = 4096
COL_R = 4608
COL_RK = 5120
COL_RV = 5632
COL_MEMQ = 6144
COL_IQ = 6656
COL_LORA = 6912
COL_IKIW = 7168
PROJ_COLS = 7680


def _cparams(sem):
    return pltpu.CompilerParams(dimension_semantics=sem, vmem_limit_bytes=VMEM_LIMIT)


def _mm(a, b):
    return jnp.dot(a.astype(MXU_DTYPE), b.astype(MXU_DTYPE), preferred_element_type=F32)


def _split3(x):
    hi = x.astype(MXU_DTYPE)
    r = x - hi.astype(F32)
    mid = r.astype(MXU_DTYPE)
    return hi, mid, (r - mid.astype(F32)).astype(MXU_DTYPE)


def _mm_sel_r(x, sel):
    sel = sel.astype(MXU_DTYPE)
    return sum(jnp.dot(t, sel, preferred_element_type=F32) for t in _split3(x))


def _mm_sel_l(sel, x):
    sel = sel.astype(MXU_DTYPE)
    return sum(jnp.dot(sel, t, preferred_element_type=F32) for t in _split3(x))


def _normproj_kernel(x_ref, g_ref, w_ref, o_ref, xn_ref):
    @pl.when(pl.program_id(1) == 0)
    def _():
        x = x_ref[...]
        ms = jnp.mean(x * x, axis=-1, keepdims=True)
        xn_ref[...] = (x * lax.rsqrt(ms + NORM_EPS) * g_ref[...]).astype(xn_ref.dtype)

    o_ref[...] = jnp.dot(xn_ref[...], w_ref[...], preferred_element_type=F32)


def _norm_proj(x, gain, w, tm, tn):
    S, D = x.shape
    N = w.shape[1]
    return pl.pallas_call(
        _normproj_kernel,
        grid=(S // tm, N // tn),
        in_specs=[
            pl.BlockSpec((tm, D), lambda i, j: (i, 0)),
            pl.BlockSpec((1, D), lambda i, j: (0, 0)),
            pl.BlockSpec((D, tn), lambda i, j: (0, j)),
        ],
        out_specs=pl.BlockSpec((tm, tn), lambda i, j: (i, j)),
        out_shape=jax.ShapeDtypeStruct((S, N), F32),
        scratch_shapes=[pltpu.VMEM((tm, D), w.dtype)],
        compiler_params=_cparams(("parallel", "arbitrary")),
    )(x, gain.reshape(1, D), w)


def _pack_w_in(w_in):
    D = w_in.shape[0]
    o = 0
    parts = {}
    for name, width in (("q", ATT_W), ("k", ATT_W), ("v", ATT_W), ("iq", IDX_HEADS * IDX_HD),
                        ("ik", IDX_HD), ("iw", IDX_HEADS), ("r", RWKV_W), ("rk", RWKV_W),
                        ("rv", RWKV_W), ("lora", DECAY_LORA + AAA_LORA + GATE_LORA),
                        ("memq", MEM_W), ("gate", N_BRANCH * D_MODEL)):
        parts[name] = w_in[:, o:o + width]
        o += width
    ikiw = jnp.concatenate([parts["ik"], parts["iw"]], axis=1)
    ikiw = jnp.pad(ikiw, ((0, 0), (0, LANES - ikiw.shape[1])))
    packed = jnp.concatenate(
        [parts["gate"], parts["q"], parts["k"], parts["v"], parts["r"], parts["rk"], parts["rv"],
         parts["memq"], parts["iq"], parts["lora"], ikiw], axis=1)
    packed = jnp.pad(packed, ((0, 0), (0, PROJ_COLS - packed.shape[1])))
    return packed.astype(MXU_DTYPE)


def _rope_tables():
    d = np.arange(LANES) % ATT_HD
    inv_freq = 1.0 / (ROPE_THETA ** (np.arange(0, ROPE_ROT, 2, dtype=np.float32) / ROPE_ROT))
    half = ROPE_ROT // 2
    tab = np.zeros((8, LANES), np.float32)
    tab[0] = np.where(d < ROPE_ROT, inv_freq[d % half], 0.0)
    tab[1] = np.where(d < half, -1.0, 0.0)
    tab[2] = np.where((d >= half) & (d < ROPE_ROT), 1.0, 0.0)
    return jnp.asarray(tab)


def _seg_mean_matrix(width, seg):
    g = (np.arange(width)[:, None] // seg) == (np.arange(width)[None, :] // seg)
    return jnp.asarray(g.astype(np.float32) / seg)


def _rope(x, cos, sina, sinb):
    W = x.shape[1]
    reps = W // LANES
    if reps > 1:
        cos = jnp.concatenate([cos] * reps, axis=1)
        sina = jnp.concatenate([sina] * reps, axis=1)
        sinb = jnp.concatenate([sinb] * reps, axis=1)
    half = ROPE_ROT // 2
    up = pltpu.roll(x, W - half, axis=1)
    down = pltpu.roll(x, half, axis=1)
    return x * cos + up * sina + down * sinb


def _att_prep_kernel(q_ref, k_ref, v_ref, iq_ref, ikiw_ref, pos_ref, tab_ref, seg_ref,
                     qg_ref, kg_ref, qT_ref, kh_ref, vT_ref, iqT_ref, ik_ref, iwT_ref):
    tab = tab_ref[...]
    ang = pos_ref[...].astype(F32) * tab[0:1, :]
    cos = jnp.cos(ang)
    sin = jnp.sin(ang)
    sina = sin * tab[1:2, :]
    sinb = sin * tab[2:3, :]
    seg = seg_ref[...]

    def head_norm(x, g):
        ms = _mm_sel_r(x * x, seg)
        return x * lax.rsqrt(ms + NORM_EPS) * g

    q = _rope(head_norm(q_ref[...], qg_ref[...]), cos, sina, sinb) * (ATT_HD ** -0.5 * LOG2E)
    k = _rope(head_norm(k_ref[...], kg_ref[...]), cos, sina, sinb)
    v = v_ref[...]
    iq = _rope(iq_ref[...], cos, sina, sinb)
    ik = _rope(ikiw_ref[...], cos, sina, sinb)
    for h in range(ATT_HEADS):
        kh_ref[h] = k[:, h * ATT_HD:(h + 1) * ATT_HD].astype(kh_ref.dtype)
    qT_ref[...] = q.T.astype(qT_ref.dtype)
    vT = v.T
    tq = vT.shape[1]
    pad = jnp.concatenate([jnp.ones((1, tq), F32), jnp.zeros((V_ROWS - ATT_HD - 1, tq), F32)], axis=0)
    for h in range(ATT_HEADS):
        vT_ref[h] = jnp.concatenate([vT[h * ATT_HD:(h + 1) * ATT_HD, :], pad],
                                    axis=0).astype(vT_ref.dtype)
    iqT_ref[...] = iq.T.astype(iqT_ref.dtype)
    ik_ref[...] = ik[:, :IDX_HD].astype(ik_ref.dtype)
    idx_scale = (IDX_HEADS ** -0.5) * (IDX_HD ** -0.5)
    iwT_ref[...] = ikiw_ref[...].T[IDX_HD:IDX_HD + 8, :] * idx_scale


def _att_prep(proj, positions, att_q_gain, att_k_gain, tq):
    S = proj.shape[0]
    qg = jnp.tile(att_q_gain.reshape(1, ATT_HD), (1, ATT_HEADS))
    kg = jnp.tile(att_k_gain.reshape(1, ATT_HD), (1, ATT_HEADS))
    col = lambda off, w: (lambda i: (i, off // w))
    const = lambda i: (0, 0)
    return pl.pallas_call(
        _att_prep_kernel,
        grid=(S // tq,),
        in_specs=[
            pl.BlockSpec((tq, ATT_W), col(COL_Q, ATT_W)),
            pl.BlockSpec((tq, ATT_W), col(COL_K, ATT_W)),
            pl.BlockSpec((tq, ATT_W), col(COL_V, ATT_W)),
            pl.BlockSpec((tq, IDX_HEADS * IDX_HD), col(COL_IQ, IDX_HEADS * IDX_HD)),
            pl.BlockSpec((tq, LANES), col(COL_IKIW, LANES)),
            pl.BlockSpec((tq, 1), lambda i: (i, 0)),
            pl.BlockSpec((8, LANES), const),
            pl.BlockSpec((ATT_W, ATT_W), const),
            pl.BlockSpec((1, ATT_W), const),
            pl.BlockSpec((1, ATT_W), const),
        ],
        out_specs=[
            pl.BlockSpec((ATT_W, tq), lambda i: (0, i)),
            pl.BlockSpec((ATT_HEADS, tq, ATT_HD), lambda i: (0, i, 0)),
            pl.BlockSpec((ATT_HEADS, V_ROWS, tq), lambda i: (0, 0, i)),
            pl.BlockSpec((IDX_HEADS * IDX_HD, tq), lambda i: (0, i)),
            pl.BlockSpec((tq, IDX_HD), lambda i: (i, 0)),
            pl.BlockSpec((8, tq), lambda i: (0, i)),
        ],
        out_shape=[
            jax.ShapeDtypeStruct((ATT_W, S), MXU_DTYPE),
            jax.ShapeDtypeStruct((ATT_HEADS, S, ATT_HD), MXU_DTYPE),
            jax.ShapeDtypeStruct((ATT_HEADS, V_ROWS, S), MXU_DTYPE),
            jax.ShapeDtypeStruct((IDX_HEADS * IDX_HD, S), MXU_DTYPE),
            jax.ShapeDtypeStruct((S, IDX_HD), MXU_DTYPE),
            jax.ShapeDtypeStruct((8, S), F32),
        ],
        compiler_params=_cparams(("parallel",)),
    )(proj, proj, proj, proj, proj, positions.reshape(S, 1), _rope_tables(),
      _seg_mean_matrix(ATT_W, ATT_HD), qg, kg)


KEY_NEG_INF = -2139095041


def _key_to_float(key):
    f = pltpu.bitcast(key ^ ((key >> 31) & 0x7FFFFFFF), F32)
    return jnp.where(key < KEY_NEG_INF, -jnp.inf, f)


SUB_ACC = 64


LIST_ROWS = 256
LIST_DEPTH = 10


def _sel_kernel(iqT_ref, iwT_ref, ik_ref, tri_ref, bias_ref, sc_ref, cand_ref, *, tq, tk, seq,
                nsel):
    i = pl.program_id(0)
    q0 = i * tq
    nc = (q0 + tq + tk - 1) // tk
    qidx = q0 + lax.broadcasted_iota(I32, (tk, tq), 1)
    krow = lax.broadcasted_iota(I32, (tk, tq), 0)
    iw = iwT_ref[...]

    def score_chunk(c, carry):
        c0 = pl.multiple_of(c * tk, tk)
        ikc = ik_ref[pl.ds(c0, tk), :]
        sc = None
        for h in range(IDX_HEADS):
            logit = jnp.dot(ikc, iqT_ref[h * IDX_HD:(h + 1) * IDX_HD, :],
                            preferred_element_type=F32)
            t = jnp.maximum(logit, 0.0) * iw[h:h + 1, :]
            sc = t if sc is None else sc + t
        sc = jnp.where(c0 + krow <= qidx, sc, -jnp.inf)
        sc_ref[pl.ds(c0, tk), :] = sc
        for part in range(tk // LIST_ROWS):
            x = sc[part * LIST_ROWS:(part + 1) * LIST_ROWS, :]
            for d in range(LIST_DEPTH):
                rows = slice(d * LIST_ROWS, (d + 1) * LIST_ROWS)
                kept = cand_ref[rows, :]
                cand_ref[rows, :] = jnp.maximum(kept, x)
                x = jnp.minimum(kept, x)
        return carry

    cand_ref[...] = jnp.full(cand_ref.shape, -jnp.inf, F32)
    lax.fori_loop(0, nc, score_chunk, 0)

    def count(pred, n_out=1, src_ref=sc_ref, n_chunks=nc):
        sub = SUB_ACC if n_out == 1 else SUB_ACC // 4

        def body(c, accs):
            c0 = pl.multiple_of(c * tk, tk)
            ms = pred(src_ref[pl.ds(c0, tk), :])
            return tuple(a + jnp.sum(m.reshape(tk // sub, sub, tq), axis=0)
                         for a, m in zip(accs, ms))
        accs = lax.fori_loop(0, n_chunks, body, (jnp.zeros((sub, tq), F32),) * n_out)
        return [_col_reduce(a, jnp.add) for a in accs]

    zero = jnp.zeros((1, tq), I32)

    def kth_largest(src_ref, n_chunks):
        def count_ge(cand):
            return count(lambda s: (jnp.where(s >= cand, 1.0, 0.0),), 1, src_ref, n_chunks)[0]

        prefix = jnp.where(count_ge(0.0) >= nsel, zero, zero + INT_MIN)

        def bit_body(b, prefix):
            cand = prefix | jnp.left_shift(jnp.int32(1), 30 - b)
            return jnp.where(count_ge(_key_to_float(cand)) >= nsel, cand, prefix)

        return _key_to_float(lax.fori_loop(0, 31, bit_body, prefix))

    n_cand_chunks = LIST_DEPTH * LIST_ROWS // tk
    tau_cand = kth_largest(cand_ref, n_cand_chunks)
    last_kept = _col_reduce(cand_ref[(LIST_DEPTH - 1) * LIST_ROWS:, :], jnp.maximum)
    lists_ok = jnp.min(jnp.where(last_kept <= tau_cand, 1.0, 0.0)) > 0.5
    tau = lax.cond(lists_ok, lambda: tau_cand, lambda: kth_largest(sc_ref, nc))

    def is_neg(s):
        return pltpu.bitcast(s, I32) < 0

    def tie_classes(s):
        eq = s == tau
        neg = is_neg(s)
        return (jnp.where(eq, jnp.where(neg, 0.0, 1.0), 0.0),
                jnp.where(eq, jnp.where(neg, 1.0, 0.0), 0.0))

    n_gt, n_eq_pos, n_eq_neg = count(
        lambda s: (jnp.where(s > tau, 1.0, 0.0),) + tie_classes(s), n_out=3)
    need = nsel - n_gt
    mixed = jnp.max(jnp.minimum(n_eq_pos, n_eq_neg)) > 0
    tri = tri_ref[...]

    def write_pass(both_signs):
        def body(c, carry):
            c0 = pl.multiple_of(c * tk, tk)
            s = sc_ref[pl.ds(c0, tk), :]
            if both_signs:
                ind_pos, ind_neg = tie_classes(s)
                cum_pos = carry[0] + _mm(tri, ind_pos)
                cum_neg = carry[1] + _mm(tri, ind_neg)
                rank = jnp.where(is_neg(s), n_eq_pos + cum_neg, cum_pos)
                carry = (cum_pos[tk - 1:tk, :], cum_neg[tk - 1:tk, :])
            else:
                rank = carry[0] + _mm(tri, jnp.where(s == tau, 1.0, 0.0))
                carry = (rank[tk - 1:tk, :], carry[1])
            tie = jnp.where(rank <= need, 0.0, NEG_BIG)
            b = jnp.where(s > tau, 0.0, jnp.where(s == tau, tie, NEG_BIG))
            b = jnp.where(c0 + krow <= qidx, b, NEG_BIG)
            bias_ref[pl.ds(c0, tk), :] = b.astype(bias_ref.dtype)
            return carry
        zf = jnp.zeros((1, tq), F32)
        lax.fori_loop(0, nc, body, (zf, zf))

    lax.cond(mixed, lambda: write_pass(True), lambda: write_pass(False))

    def fill_chunk(c, carry):
        c0 = pl.multiple_of(c * tk, tk)
        bias_ref[pl.ds(c0, tk), :] = jnp.full((tk, tq), NEG_BIG, bias_ref.dtype)
        return carry

    lax.fori_loop(nc, seq // tk, fill_chunk, 0)


def _dsa_select(iqT, iwT, ik, tq, tk):
    S = ik.shape[0]
    nsel = min(TOPK_MAX, S // 4)
    kern = functools.partial(_sel_kernel, tq=tq, tk=tk, seq=S, nsel=nsel)
    return pl.pallas_call(
        kern,
        grid=(S // tq,),
        in_specs=[
            pl.BlockSpec((IDX_HEADS * IDX_HD, tq), lambda i: (0, i)),
            pl.BlockSpec((8, tq), lambda i: (0, i)),
            pl.BlockSpec((S, IDX_HD), lambda i: (0, 0)),
            pl.BlockSpec((tk, tk), lambda i: (0, 0)),
        ],
        out_specs=pl.BlockSpec((S, tq), lambda i: (0, i)),
        out_shape=jax.ShapeDtypeStruct((S, S), jnp.bfloat16),
        scratch_shapes=[pltpu.VMEM((S, tq), F32),
                        pltpu.VMEM((LIST_DEPTH * LIST_ROWS, tq), F32)],
        compiler_params=_cparams(("parallel",)),
    )(iqT, iwT, ik, jnp.tril(jnp.ones((tk, tk), MXU_DTYPE)))


def _col_reduce(x, op):
    rows = x.shape[0]
    while rows > 8:
        rows //= 2
        x = op(x[:rows], x[rows:])
    red = jnp.max if op is jnp.maximum else jnp.sum
    return red(x, axis=0, keepdims=True)


def _att_kernel(qi_ref, kj_ref, qT_ref, k_ref, vT_ref, bias_ref, o_ref, m_ref, acc_ref,
                *, tq, tk):
    t = pl.program_id(0)
    i = qi_ref[t]
    j = kj_ref[t]
    last = ((i + 1) * tq - 1) // tk

    @pl.when(j == 0)
    def _():
        m_ref[...] = jnp.full(m_ref.shape, -jnp.inf, F32)
        acc_ref[...] = jnp.zeros(acc_ref.shape, F32)

    bias = bias_ref[...].astype(F32)
    hs = lambda h: slice(h * ATT_HD, (h + 1) * ATT_HD)
    m_all = m_ref[...]
    m_out = []
    halves = (slice(0, tk // 2), slice(tk // 2, tk))
    for h in range(ATT_HEADS):
        qh = qT_ref[hs(h), :]
        ss = [jnp.dot(k_ref[h, ks, :], qh, preferred_element_type=F32) + bias[ks, :]
              for ks in halves]
        m_prev = m_all[h:h + 1, :]
        m_new = jnp.maximum(m_prev, jnp.maximum(_col_reduce(ss[0], jnp.maximum),
                                                _col_reduce(ss[1], jnp.maximum)))
        alpha = jnp.exp2(m_prev - m_new)
        m_out.append(m_new)
        pv = [jnp.dot(vT_ref[h, :, ks], jnp.exp2((s - m_new).astype(vT_ref.dtype)),
                      preferred_element_type=F32) for ks, s in zip(halves, ss)]
        acc_ref[h] = alpha * acc_ref[h] + pv[0] + pv[1]
    m_ref[...] = jnp.concatenate(m_out, axis=0)

    @pl.when(j == last)
    def _():
        out = jnp.concatenate(
            [acc_ref[h, :ATT_HD, :] / acc_ref[h, ATT_HD:ATT_HD + 1, :]
             for h in range(ATT_HEADS)], axis=0)
        o_ref[...] = out.T


def _dsa_attend(qT, kh, vT, bias, tq, tk):
    S = qT.shape[1]
    pairs = [(i, j) for i in range(S // tq) for j in range(((i + 1) * tq - 1) // tk + 1)]
    qi = jnp.asarray([p[0] for p in pairs], I32)
    kj = jnp.asarray([p[1] for p in pairs], I32)
    grid_spec = pltpu.PrefetchScalarGridSpec(
        num_scalar_prefetch=2,
        grid=(len(pairs),),
        in_specs=[
            pl.BlockSpec((ATT_W, tq), lambda t, qi, kj: (0, qi[t])),
            pl.BlockSpec((ATT_HEADS, tk, ATT_HD), lambda t, qi, kj: (0, kj[t], 0)),
            pl.BlockSpec((ATT_HEADS, V_ROWS, tk), lambda t, qi, kj: (0, 0, kj[t])),
            pl.BlockSpec((tk, tq), lambda t, qi, kj: (kj[t], qi[t])),
        ],
        out_specs=pl.BlockSpec((tq, ATT_W), lambda t, qi, kj: (qi[t], 0)),
        scratch_shapes=[
            pltpu.VMEM((ATT_HEADS, tq), F32),
            pltpu.VMEM((ATT_HEADS, V_ROWS, tq), F32),
        ],
    )
    return pl.pallas_call(
        functools.partial(_att_kernel, tq=tq, tk=tk),
        grid_spec=grid_spec,
        out_shape=jax.ShapeDtypeStruct((S, ATT_W), F32),
        compiler_params=_cparams(("arbitrary",)),
    )(qi, kj, qT, kh, vT, bias)


CHUNK = 64


def _shift_rows(x, prev8, first):
    prev_row = jnp.where(first, 0.0, prev8[7:8, :])
    row = lax.broadcasted_iota(I32, x.shape, 0)
    return jnp.where(row == 0, prev_row, pltpu.roll(x, 1, axis=0))


def _softplus(z):
    return jnp.maximum(z, 0.0) + jnp.log1p(jnp.exp(-jnp.abs(z)))


def _rwkv_prep_kernel(r_ref, k_ref, v_ref, lo_ref, rp_ref, kp_ref, vp_ref, lop_ref,
                      mur_ref, muk_ref, muv_ref, mulo_ref, w0_ref, w2_ref, a0_ref, a2_ref,
                      g2_ref, kk_ref, ka_ref, rk_ref, seg_ref, tri_ref, ones_ref, end_ref,
                      rt_ref, kt_ref, bt_ref, at_ref, kh_ref, bh_ref, vh_ref, gam_ref,
                      bonus_ref, g_ref):
    first = pl.program_id(0) == 0

    def mix(x_ref, p_ref, mu_ref):
        x = x_ref[...]
        return x + (_shift_rows(x, p_ref[...], first) - x) * mu_ref[...]

    r = mix(r_ref, rp_ref, mur_ref)
    k = mix(k_ref, kp_ref, muk_ref)
    v = mix(v_ref, vp_ref, muv_ref)
    lo = mix(lo_ref, lop_ref, mulo_ref)
    wd = lo[:, :DECAY_LORA]
    ad = lo[:, DECAY_LORA:DECAY_LORA + AAA_LORA]
    gd = lo[:, DECAY_LORA + AAA_LORA:]
    w_log = -_softplus(-(w0_ref[...] + _mm(jnp.tanh(wd), w2_ref[...]))) - 0.5
    lw = -jnp.exp(w_log)
    a = jax.nn.sigmoid(a0_ref[...] + _mm(ad, a2_ref[...]))
    g_ref[...] = _mm(jax.nn.sigmoid(gd), g2_ref[...])
    seg = seg_ref[...]
    kk = k * kk_ref[...]
    kk = kk / jnp.maximum(jnp.sqrt(_mm_sel_r(kk * kk, seg)), 1e-12)
    k = k * (1.0 + (a - 1.0) * ka_ref[...])
    bonus_ref[...] = _mm_sel_r(r * k * rk_ref[...], seg) * v
    avec = -kk
    bvec = kk * a
    cs = _mm_sel_l(tri_ref[...], lw)
    cs_end = _mm_sel_l(ones_ref[...], lw)
    e_neg = jnp.exp(-cs)
    e_end = jnp.exp(cs_end - cs)
    outs = ((rt_ref, r * jnp.exp(cs)), (kt_ref, k * e_neg), (bt_ref, bvec * e_neg),
            (at_ref, avec * jnp.exp(cs - lw)), (kh_ref, k * e_end), (bh_ref, bvec * e_end),
            (vh_ref, v), (gam_ref, jnp.exp(_mm_sel_l(end_ref[...], lw))))
    for ref, val in outs:
        for h in range(RWKV_HEADS):
            ref[h] = val[:, h * RWKV_HD:(h + 1) * RWKV_HD]


def _rwkv_prep(proj, p, tq):
    S = proj.shape[0]
    nch = tq // CHUNK
    col = lambda off, w: (lambda i: (i, off // w))
    prev = lambda off, w: (lambda i: (jnp.maximum(i * (tq // 8) - 1, 0), off // w))
    const = lambda i: (0, 0)
    lw_ = DECAY_LORA + AAA_LORA + GATE_LORA
    mu = p["rwkv_mu"]
    row = lambda t: t.reshape(1, -1)
    t_idx = np.arange(tq)
    same = (t_idx[:, None] // CHUNK) == (t_idx[None, :] // CHUNK)
    tri = jnp.asarray((same & (t_idx[None, :] <= t_idx[:, None])).astype(np.float32))
    ones = jnp.asarray(same.astype(np.float32))
    end = jnp.asarray(((t_idx[None, :] // CHUNK) == np.arange(nch)[:, None]).astype(np.float32))
    seg = _seg_mean_matrix(RWKV_W, RWKV_HD) * RWKV_HD
    hm = jax.ShapeDtypeStruct((RWKV_HEADS, S, RWKV_HD), F32)
    hm_spec = pl.BlockSpec((RWKV_HEADS, tq, RWKV_HD), lambda i: (0, i, 0))
    wide = jax.ShapeDtypeStruct((S, RWKV_W), F32)
    wide_spec = pl.BlockSpec((tq, RWKV_W), lambda i: (i, 0))
    vec = lambda w: pl.BlockSpec((1, w), const)
    return pl.pallas_call(
        _rwkv_prep_kernel,
        grid=(S // tq,),
        in_specs=[
            pl.BlockSpec((tq, RWKV_W), col(COL_R, RWKV_W)),
            pl.BlockSpec((tq, RWKV_W), col(COL_RK, RWKV_W)),
            pl.BlockSpec((tq, RWKV_W), col(COL_RV, RWKV_W)),
            pl.BlockSpec((tq, lw_), col(COL_LORA, lw_)),
            pl.BlockSpec((8, RWKV_W), prev(COL_R, RWKV_W)),
            pl.BlockSpec((8, RWKV_W), prev(COL_RK, RWKV_W)),
            pl.BlockSpec((8, RWKV_W), prev(COL_RV, RWKV_W)),
            pl.BlockSpec((8, lw_), prev(COL_LORA, lw_)),
            vec(RWKV_W), vec(RWKV_W), vec(RWKV_W), vec(lw_),
            vec(RWKV_W), pl.BlockSpec((DECAY_LORA, RWKV_W), const),
            vec(RWKV_W), pl.BlockSpec((AAA_LORA, RWKV_W), const),
            pl.BlockSpec((GATE_LORA, RWKV_W), const),
            vec(RWKV_W), vec(RWKV_W), vec(RWKV_W),
            pl.BlockSpec((RWKV_W, RWKV_W), const),
            pl.BlockSpec((tq, tq), const), pl.BlockSpec((tq, tq), const),
            pl.BlockSpec((nch, tq), const),
        ],
        out_specs=[hm_spec] * 7 + [pl.BlockSpec((RWKV_HEADS, nch, RWKV_HD), lambda i: (0, i, 0)),
                                   wide_spec, wide_spec],
        out_shape=[hm] * 7 + [jax.ShapeDtypeStruct((RWKV_HEADS, S // CHUNK, RWKV_HD), F32),
                              wide, wide],
        compiler_params=_cparams(("parallel",)),
    )(proj, proj, proj, proj, proj, proj, proj, proj,
      row(mu[:RWKV_W]), row(mu[RWKV_W:2 * RWKV_W]), row(mu[2 * RWKV_W:3 * RWKV_W]),
      row(mu[3 * RWKV_W:]), row(p["rwkv_w0"]), p["rwkv_w2"], row(p["rwkv_a0"]), p["rwkv_a2"],
      p["rwkv_g2"], row(p["rwkv_k_k"]), row(p["rwkv_k_a"]), row(p["rwkv_r_k"]), seg, tri, ones, end)


def _bmm(a, b, dims):
    return jnp.einsum(dims, a, b, preferred_element_type=F32, precision=HI)


def _bmm1(a, b, dims):
    return jnp.einsum(dims, a.astype(MXU_DTYPE), b.astype(MXU_DTYPE), preferred_element_type=F32)


def _rwkv_chunk_kernel(rt_ref, kt_ref, bt_ref, at_ref, kh_ref, bh_ref, v_ref, gam_ref,
                       p_ref, q_ref, rw_ref, y0_ref, *, nch):
    L, N = CHUNK, RWKV_HD
    ri = lax.broadcasted_iota(I32, (nch, L, L), 1)
    ci = lax.broadcasted_iota(I32, (nch, L, L), 2)
    eye_n = (lax.broadcasted_iota(I32, (nch, N, N), 1)
             == lax.broadcasted_iota(I32, (nch, N, N), 2)).astype(F32)
    for h in range(RWKV_HEADS):
        ld = lambda ref: ref[h].reshape(nch, L, N)
        rt, kt, bt, at, kh, bh, v = (ld(x) for x in (rt_ref, kt_ref, bt_ref, at_ref, kh_ref,
                                                      bh_ref, v_ref))
        gam = gam_ref[h].reshape(nch, 1, N)
        mm = _bmm1
        a_ab = jnp.where(ci < ri, mm(at, bt, "cld,cmd->clm"), 0.0)
        a_ak = jnp.where(ci < ri, mm(at, kt, "cld,cmd->clm"), 0.0)
        m_rk = jnp.where(ci <= ri, mm(rt, kt, "cld,cmd->clm"), 0.0)
        m_rb = jnp.where(ci <= ri, mm(rt, bt, "cld,cmd->clm"), 0.0)
        rhs = jnp.concatenate([at, mm(a_ak, v, "clm,cmd->cld")], axis=2)
        pw = a_ab
        step = 1
        while True:
            rhs = rhs + mm(pw, rhs, "clm,cmd->cld")
            step *= 2
            if step >= L:
                break
            pw = mm(pw, pw, "clm,cmn->cln")
        w, u0 = rhs[:, :, :N], rhs[:, :, N:]
        p_ref[:, h] = eye_n * gam + mm(bh, w, "cld,cle->cde")
        q_ref[:, h] = mm(kh, v, "cld,cle->cde") + mm(bh, u0, "cld,cle->cde")
        rw_ref[h] = (rt + mm(m_rb, w, "clm,cmd->cld")).reshape(nch * L, N)
        y0_ref[h] = (mm(m_rk, v, "clm,cmd->cld") + mm(m_rb, u0, "clm,cmd->cld")).reshape(nch * L, N)


def _rwkv_chunks(rt, kt, bt, at, kh, bh, vh, gam, tt):
    S = rt.shape[1]
    nch = tt // CHUNK
    hm_spec = pl.BlockSpec((RWKV_HEADS, tt, RWKV_HD), lambda i: (0, i, 0))
    hm = jax.ShapeDtypeStruct((RWKV_HEADS, S, RWKV_HD), F32)
    sq_spec = pl.BlockSpec((nch, RWKV_HEADS, RWKV_HD, RWKV_HD), lambda i: (i, 0, 0, 0))
    sq = jax.ShapeDtypeStruct((S // CHUNK, RWKV_HEADS, RWKV_HD, RWKV_HD), F32)
    return pl.pallas_call(
        functools.partial(_rwkv_chunk_kernel, nch=nch),
        grid=(S // tt,),
        in_specs=[hm_spec] * 7 + [pl.BlockSpec((RWKV_HEADS, nch, RWKV_HD), lambda i: (0, i, 0))],
        out_specs=[sq_spec, sq_spec, hm_spec, hm_spec],
        out_shape=[sq, sq, hm, hm],
        compiler_params=_cparams(("parallel",)),
    )(rt, kt, bt, at, kh, bh, vh, gam)


def _rwkv_scan_kernel(p_ref, q_ref, rw_ref, y0_ref, bonus_ref, g_ref, seg_ref, lnw_ref, lnb_ref,
                      o_ref, st_ref, *, nch):
    @pl.when(pl.program_id(0) == 0)
    def _():
        st_ref[...] = jnp.zeros(st_ref.shape, F32)

    L = CHUNK
    st = st_ref[...]
    ys = []
    for c in range(nch):
        rw = rw_ref[:, c * L:(c + 1) * L, :]
        ys.append(_bmm(rw, st, "hld,hde->hle") + y0_ref[:, c * L:(c + 1) * L, :])
        st = _bmm(p_ref[c], st, "hjk,hki->hji") + q_ref[c]
    st_ref[...] = st
    y = jnp.concatenate(ys, axis=1)
    y = jnp.concatenate([y[h] for h in range(RWKV_HEADS)], axis=1)
    seg = seg_ref[...]
    mean = _mm_sel_r(y, seg)
    d = y - mean
    var = _mm_sel_r(d * d, seg)
    y = d * lax.rsqrt(var + GN_EPS) * lnw_ref[...] + lnb_ref[...]
    o_ref[...] = (y + bonus_ref[...]) * g_ref[...]


def _rwkv_scan(pm, qm, rw, y0, bonus, g, ln_w, ln_b, tt):
    S = rw.shape[1]
    nch = tt // CHUNK
    hm_spec = pl.BlockSpec((RWKV_HEADS, tt, RWKV_HD), lambda i: (0, i, 0))
    sq_spec = pl.BlockSpec((nch, RWKV_HEADS, RWKV_HD, RWKV_HD), lambda i: (i, 0, 0, 0))
    wide_spec = pl.BlockSpec((tt, RWKV_W), lambda i: (i, 0))
    const = lambda i: (0, 0)
    return pl.pallas_call(
        functools.partial(_rwkv_scan_kernel, nch=nch),
        grid=(S // tt,),
        in_specs=[sq_spec, sq_spec, hm_spec, hm_spec, wide_spec, wide_spec,
                  pl.BlockSpec((RWKV_W, RWKV_W), const), pl.BlockSpec((1, RWKV_W), const),
                  pl.BlockSpec((1, RWKV_W), const)],
        out_specs=wide_spec,
        out_shape=jax.ShapeDtypeStruct((S, RWKV_W), F32),
        scratch_shapes=[pltpu.VMEM((RWKV_HEADS, RWKV_HD, RWKV_HD), F32)],
        compiler_params=_cparams(("arbitrary",)),
    )(pm, qm, rw, y0, bonus, g, _seg_mean_matrix(RWKV_W, RWKV_HD), ln_w.reshape(1, -1),
      ln_b.reshape(1, -1))


def _rwkv_time_mix(proj, p, tt):
    outs = _rwkv_prep(proj, p, tt)
    rt, kt, bt, at, kh, bh, vh, gam, bonus, g = outs
    pm, qm, rw, y0 = _rwkv_chunks(rt, kt, bt, at, kh, bh, vh, gam, tt)
    return _rwkv_scan(pm, qm, rw, y0, bonus, g, p["rwkv_ln_w"], p["rwkv_ln_b"], tt)


def _lane_rmsnorm(x, gain):
    return x * lax.rsqrt(jnp.mean(x * x, axis=-1, keepdims=True) + NORM_EPS) * gain


def _mem_kv_kernel(mem_ref, g_ref, w_ref, kg_ref, kmT_ref, vm_ref):
    m = _lane_rmsnorm(mem_ref[...], g_ref[...])
    kv = _mm(m, w_ref[...])
    km = jnp.concatenate(
        [_lane_rmsnorm(kv[:, h * MEM_HD:(h + 1) * MEM_HD], kg_ref[...]) for h in range(MEM_HEADS)],
        axis=1)
    kmT_ref[...] = km.T.astype(kmT_ref.dtype)
    vm_ref[...] = kv[:, MEM_W:].astype(vm_ref.dtype)


def _mem_kv(mem, gain, w_kv, k_gain):
    M = mem.shape[0]
    return pl.pallas_call(
        _mem_kv_kernel,
        out_shape=[jax.ShapeDtypeStruct((MEM_W, M), MXU_DTYPE),
                   jax.ShapeDtypeStruct((M, MEM_W), MXU_DTYPE)],
        compiler_params=pltpu.CompilerParams(vmem_limit_bytes=VMEM_LIMIT),
    )(mem, gain.reshape(1, -1), w_kv.astype(MXU_DTYPE), k_gain.reshape(1, -1))


def _merge_kernel(x_ref, yatt_ref, yrwkv_ref, memq_ref, gate_ref, kmT_ref, vm_ref, qg_ref,
                  gb_ref, wup_ref, wout_ref, o_ref):
    q = memq_ref[...]
    heads = []
    for h in range(MEM_HEADS):
        sl = slice(h * MEM_HD, (h + 1) * MEM_HD)
        qn = _lane_rmsnorm(q[:, sl], qg_ref[...])
        s = _mm(qn, kmT_ref[sl, :]) * (MEM_HD ** -0.5)
        s = s - jnp.max(s, axis=-1, keepdims=True)
        e = jnp.exp(s)
        p = e / jnp.sum(e, axis=-1, keepdims=True)
        heads.append(_mm(p, vm_ref[:, sl]))
    ymem = jnp.concatenate(heads, axis=1)
    merged = None
    for c, y in enumerate((yatt_ref[...], yrwkv_ref[...], ymem)):
        up = _mm(y, wup_ref[c])
        gate = jax.nn.sigmoid(gate_ref[:, c * D_MODEL:(c + 1) * D_MODEL] + gb_ref[c:c + 1, :])
        merged = gate * up if merged is None else merged + gate * up
    o_ref[...] = x_ref[...] + _mm(merged, wout_ref[...])


def _merge(x, y_att, y_rwkv, proj, kmT, vm, mem_q_gain, gate_bias, w_up, w_out, tq):
    S = x.shape[0]
    M = vm.shape[0]
    gw = N_BRANCH * D_MODEL
    const2 = lambda i: (0, 0)
    return pl.pallas_call(
        _merge_kernel,
        grid=(S // tq,),
        in_specs=[
            pl.BlockSpec((tq, D_MODEL), lambda i: (i, 0)),
            pl.BlockSpec((tq, ATT_W), lambda i: (i, 0)),
            pl.BlockSpec((tq, RWKV_W), lambda i: (i, 0)),
            pl.BlockSpec((tq, MEM_W), lambda i: (i, COL_MEMQ // MEM_W)),
            pl.BlockSpec((tq, gw), lambda i: (i, COL_GATE // gw)),
            pl.BlockSpec((MEM_W, M), const2),
            pl.BlockSpec((M, MEM_W), const2),
            pl.BlockSpec((1, MEM_HD), const2),
            pl.BlockSpec((N_BRANCH, D_MODEL), const2),
            pl.BlockSpec((N_BRANCH, ATT_W, D_MODEL), lambda i: (0, 0, 0)),
            pl.BlockSpec((D_MODEL, D_MODEL), const2),
        ],
        out_specs=pl.BlockSpec((tq, D_MODEL), lambda i: (i, 0)),
        out_shape=jax.ShapeDtypeStruct((S, D_MODEL), F32),
        compiler_params=_cparams(("parallel",)),
    )(x, y_att, y_rwkv, proj, proj, kmT, vm, mem_q_gain.reshape(1, -1), gate_bias,
      w_up.astype(MXU_DTYPE), w_out.astype(MXU_DTYPE))


NO_RANK = 255.0


def _pop_max(cur, rows, first_only=True):
    m = jnp.max(cur, axis=0, keepdims=True)
    hit = cur == m
    if first_only:
        first = jnp.min(jnp.where(hit, rows, cur.shape[0]), axis=0, keepdims=True)
        hit = rows == first
    return m, hit, jnp.where(hit, -jnp.inf, cur)


def _peer_route_kernel(h_ref, g_ref, wq_ref, k1_ref, k2_ref,
                       xn_ref, bsel_ref, r2_ref, p1_ref, p2_ref):
    tq = h_ref.shape[0]
    xn = _lane_rmsnorm(h_ref[...], g_ref[...]).astype(xn_ref.dtype)
    xn_ref[...] = xn
    qp = jnp.dot(xn, wq_ref[...], preferred_element_type=F32)
    nt = (((1,), (1,)), ((), ()))
    half = PEER_QDIM // 2
    rows = lax.broadcasted_iota(I32, (PEER_KEYS, tq), 0)
    pairs = [(a, b) for a in range(PEER_TOPK) for b in range(PEER_TOPK)
             if (a + 1) * (b + 1) <= PEER_TOPK]
    npad = -len(pairs) % 8
    crow = lax.broadcasted_iota(I32, (len(pairs) + npad, tq), 0)
    def route_head(h, first_only):
        q1 = qp[:, h * PEER_QDIM:h * PEER_QDIM + half].astype(MXU_DTYPE)
        q2 = qp[:, h * PEER_QDIM + half:(h + 1) * PEER_QDIM].astype(MXU_DTYPE)
        s1 = lax.dot_general(k1_ref[...], q1, nt, preferred_element_type=F32)
        s2 = lax.dot_general(k2_ref[...], q2, nt, preferred_element_type=F32)
        tops, ranks = [], []
        repeated = jnp.zeros((1, tq), F32)
        for s in (s1, s2):
            cur, vals = s, []
            rank = jnp.full((PEER_KEYS, tq), NO_RANK, F32)
            for r in range(PEER_TOPK):
                m, hit, cur = _pop_max(cur, rows, first_only)
                rank = jnp.where(hit, float(r), rank)
                vals.append(m)
            if not first_only:
                taken = _col_reduce(jnp.where(cur == -jnp.inf, 1.0, 0.0), jnp.add)
                repeated = jnp.maximum(repeated, jnp.where(taken != PEER_TOPK, 1.0, 0.0))
            tops.append(vals)
            ranks.append(rank)
        v1, v2 = tops
        rank1, rank2 = ranks
        cand = jnp.concatenate([v1[a] + v2[b] for a, b in pairs]
                               + [jnp.full((npad, tq), -jnp.inf, F32)], axis=0)
        top = v1[0] + v2[0]
        z = jnp.zeros((1, tq), F32)
        taken = jnp.zeros(cand.shape, F32)
        for _ in range(PEER_TOPK):
            m, hit, cand = _pop_max(cand, crow)
            z = z + jnp.exp(m - top)
            taken = taken + jnp.where(hit, 1.0, 0.0)
        bsel = jnp.zeros((PEER_KEYS, tq), F32)
        for a in range(PEER_TOPK):
            idx = [i for i, (pa, _) in enumerate(pairs) if pa == a]
            n_b = jnp.sum(taken[idx[0]:idx[-1] + 1, :], axis=0, keepdims=True)
            bsel = jnp.where(rank1 == float(a), n_b, bsel)
        bsel_ref[h] = bsel
        r2_ref[h] = rank2.astype(r2_ref.dtype)
        p1_ref[h] = jnp.exp(s1 - v1[0]) / z
        p2_ref[h] = jnp.exp(s2 - v2[0]).astype(p2_ref.dtype)
        return repeated

    repeated = jnp.zeros((1, tq), F32)
    for h in range(PEER_HEADS):
        repeated = jnp.maximum(repeated, route_head(h, False))

    @pl.when(jnp.max(repeated) > 0.5)
    def _():
        for h in range(PEER_HEADS):
            route_head(h, True)


def _peer_route(h2, gain, w_q, key1, key2, tq):
    S = h2.shape[0]
    const2 = lambda i: (0, 0)
    half = PEER_QDIM // 2
    kt = jax.ShapeDtypeStruct((PEER_HEADS, PEER_KEYS, S), F32)
    ktn = jax.ShapeDtypeStruct((PEER_HEADS, PEER_KEYS, S), MXU_DTYPE)
    kt_spec = pl.BlockSpec((PEER_HEADS, PEER_KEYS, tq), lambda i: (0, 0, i))
    return pl.pallas_call(
        _peer_route_kernel,
        grid=(S // tq,),
        in_specs=[
            pl.BlockSpec((tq, D_MODEL), lambda i: (i, 0)),
            pl.BlockSpec((1, D_MODEL), const2),
            pl.BlockSpec((D_MODEL, PEER_HEADS * PEER_QDIM), const2),
            pl.BlockSpec((PEER_KEYS, half), const2),
            pl.BlockSpec((PEER_KEYS, half), const2),
        ],
        out_specs=[pl.BlockSpec((tq, D_MODEL), lambda i: (i, 0)), kt_spec, kt_spec, kt_spec,
                   kt_spec],
        out_shape=[jax.ShapeDtypeStruct((S, D_MODEL), MXU_DTYPE), kt, ktn, kt, ktn],
        compiler_params=_cparams(("parallel",)),
    )(h2, gain.reshape(1, -1), w_q.astype(MXU_DTYPE), key1.astype(MXU_DTYPE),
      key2.astype(MXU_DTYPE))


def _peer_dense_kernel(xn_ref, u0_ref, unext_ref, vT_ref, bsel_ref, r2_ref, p1_ref, p2_ref, h_ref,
                       o_ref, acc_ref, hid_ref, x_ref, *, te, ne):
    j = pl.program_id(1)
    nt = (((1,), (1,)), ((), ()))

    @pl.when(j == 0)
    def _():
        acc_ref[...] = jnp.zeros(acc_ref.shape, F32)
        hid_ref[0] = lax.dot_general(u0_ref[...], xn_ref[...], nt, preferred_element_type=F32)

    slot = j % 2
    hid_ref[1 - slot] = lax.dot_general(unext_ref[...], xn_ref[...], nt,
                                        preferred_element_type=F32)
    hid = hid_ref[slot]
    act = (0.5 * hid * (1.0 + lax.erf(hid * (2.0 ** -0.5)))).astype(x_ref.dtype)
    for el in range(te // PEER_KEYS):
        e1 = j * (te // PEER_KEYS) + el
        g = None
        for h in range(PEER_HEADS):
            n_b = bsel_ref[h, pl.ds(e1, 1), :].astype(x_ref.dtype)
            p1r = p1_ref[h, pl.ds(e1, 1), :].astype(x_ref.dtype)
            t = jnp.where(r2_ref[h] < n_b, p2_ref[h], 0.0) * p1r
            g = t if g is None else g + t
        sl = slice(el * PEER_KEYS, (el + 1) * PEER_KEYS)
        x_ref[sl, :] = g * act[sl, :]
    acc_ref[...] += jnp.dot(vT_ref[0], x_ref[...], preferred_element_type=F32)

    @pl.when(j == ne - 1)
    def _():
        o_ref[...] = h_ref[...] + acc_ref[...].T


def _peer_dense(h2, xn, u, v, bsel, r2, p1, p2, tm, te):
    S = h2.shape[0]
    NE = u.shape[0]
    ne = NE // te
    kt_spec = pl.BlockSpec((PEER_HEADS, PEER_KEYS, tm), lambda i, j: (0, 0, i))
    u_c = u.astype(MXU_DTYPE)
    vT_tiles = v.astype(MXU_DTYPE).reshape(ne, te, D_MODEL).transpose(0, 2, 1)
    return pl.pallas_call(
        functools.partial(_peer_dense_kernel, te=te, ne=ne),
        grid=(S // tm, ne),
        in_specs=[
            pl.BlockSpec((tm, D_MODEL), lambda i, j: (i, 0)),
            pl.BlockSpec((te, D_MODEL), lambda i, j: (0, 0)),
            pl.BlockSpec((te, D_MODEL), lambda i, j: (jnp.minimum(j + 1, ne - 1), 0)),
            pl.BlockSpec((1, D_MODEL, te), lambda i, j: (j, 0, 0)),
            kt_spec, kt_spec, kt_spec, kt_spec,
            pl.BlockSpec((tm, D_MODEL), lambda i, j: (i, 0)),
        ],
        out_specs=pl.BlockSpec((tm, D_MODEL), lambda i, j: (i, 0)),
        out_shape=jax.ShapeDtypeStruct((S, D_MODEL), F32),
        scratch_shapes=[pltpu.VMEM((D_MODEL, tm), F32), pltpu.VMEM((2, te, tm), F32),
                        pltpu.VMEM((te, tm), MXU_DTYPE)],
        compiler_params=_cparams(("parallel", "arbitrary")),
    )(xn, u_c, u_c, vT_tiles, bsel, r2, p1, p2, h2)


def _tiles(S):
    cap = lambda t: min(t, S)
    return dict(
        proj=(cap(1024), 768),
        att_prep=cap(256),
        select=(cap(256), cap(512)),
        attend=(cap(512), cap(1024)),
        rwkv=cap(512),
        merge=cap(256),
        route=cap(256),
        peer=(cap(512), 1024),
    )


def _layer(x, mem, positions, p):
    t = _tiles(x.shape[0])
    proj = _norm_proj(x, p["mix_norm_gain"], _pack_w_in(p["w_in"]), *t["proj"])
    qT, kh, vT, iqT, ik, iwT = _att_prep(proj, positions, p["att_q_gain"], p["att_k_gain"],
                                         t["att_prep"])
    bias = _dsa_select(iqT, iwT, ik, *t["select"])
    y_att = _dsa_attend(qT, kh, vT, bias, *t["attend"])
    y_rwkv = _rwkv_time_mix(proj, p, t["rwkv"])
    kmT, vm = _mem_kv(mem, p["mem_norm_gain"], p["w_mem_kv"], p["mem_k_gain"])
    h2 = _merge(x, y_att, y_rwkv, proj, kmT, vm, p["mem_q_gain"], p["gate_bias"], p["w_up"],
                p["w_out"], t["merge"])
    xn, bsel, r2, p1, p2 = _peer_route(h2, p["ffn_norm_gain"], p["peer_w_q"], p["peer_key1"],
                                       p["peer_key2"], t["route"])
    return _peer_dense(h2, xn, p["peer_u"], p["peer_v"], bsel, r2, p1, p2, *t["peer"])


_PARAM_NAMES = ("mix_norm_gain", "w_in", "gate_bias", "att_q_gain", "att_k_gain", "mem_norm_gain",
                "w_mem_kv", "mem_q_gain", "mem_k_gain", "rwkv_mu", "rwkv_w0", "rwkv_w2", "rwkv_a0",
                "rwkv_a2", "rwkv_g2", "rwkv_k_k", "rwkv_k_a", "rwkv_r_k", "rwkv_ln_w", "rwkv_ln_b",
                "w_up", "w_out", "ffn_norm_gain", "peer_w_q", "peer_key1", "peer_key2", "peer_u",
                "peer_v")


def kernel(x, mem, positions, mix_norm_gain, w_in, gate_bias, att_q_gain, att_k_gain,
           mem_norm_gain, w_mem_kv, mem_q_gain, mem_k_gain, rwkv_mu, rwkv_w0, rwkv_w2, rwkv_a0,
           rwkv_a2, rwkv_g2, rwkv_k_k, rwkv_k_a, rwkv_r_k, rwkv_ln_w, rwkv_ln_b, w_up, w_out,
           ffn_norm_gain, peer_w_q, peer_key1, peer_key2, peer_u, peer_v):
    params = (mix_norm_gain, w_in, gate_bias, att_q_gain, att_k_gain, mem_norm_gain, w_mem_kv,
              mem_q_gain, mem_k_gain, rwkv_mu, rwkv_w0, rwkv_w2, rwkv_a0, rwkv_a2, rwkv_g2,
              rwkv_k_k, rwkv_k_a, rwkv_r_k, rwkv_ln_w, rwkv_ln_b, w_up, w_out, ffn_norm_gain,
              peer_w_q, peer_key1, peer_key2, peer_u, peer_v)
    assert x.shape[0] == 1 and all(t.shape[0] == 1 for t in params)
    p = {name: t[0] for name, t in zip(_PARAM_NAMES, params)}
    return _layer(x[0], mem[0], positions[0], p)[None]
```
